```python
import math
import jax, jax.numpy as jnp
from jax import lax
import numpy as np

D_MODEL = 2048
BATCH = 8
SEQ = 2048
DEPTH = 2

GRID_W = 64
CTX_LEN = 256
NORM_EPS = 1e-6

RW_HEADS = 8
RW_HD = 64
RW_W = RW_HEADS * RW_HD
RW_DECAY_RANK = 64
RW_ICLR_RANK = 64
RW_GATE_RANK = 128
RW_GN_EPS = 64e-5
RW_IN = 3 * RW_W + 2 * RW_DECAY_RANK + 2 * RW_ICLR_RANK + RW_GATE_RANK

SSD_HEADS = 8
SSD_HD = 64
SSD_W = SSD_HEADS * SSD_HD
SSD_GROUPS = 2
SSD_STATE = 128
SSD_CONV = 3
SSD_CHUNK = 128
SSD_XBC = SSD_W + 2 * SSD_GROUPS * SSD_STATE
SSD_IN = SSD_W + SSD_XBC + 2 * SSD_HEADS

DA_HEADS = 4
DA_HD = 64
DA_W = DA_HEADS * 2 * DA_HD
DA_IN = 3 * DA_W
DA_BLOCK = 128
DA_SUBLN_EPS = 1e-5

NA_HEADS = 8
NA_HD = 64
NA_W = NA_HEADS * NA_HD
NA_IN = 3 * NA_W
NA_WIN_R = 8
NA_WIN_C = 16

N_BRANCH = 4
BR_W = 512
IN_SIZES = (RW_IN, SSD_IN, DA_IN, NA_IN)
IN_TOTAL = RW_IN + SSD_IN + DA_IN + NA_IN

D_FF = 5632
FFN_CONV = 3
ROPE_BASE = 10000.0

kernel_name = 'hybrid_prefix_dit_block'


def _split(x, sizes):
    return jnp.split(x, np.cumsum(sizes)[:-1].tolist(), axis=-1)


def rmsnorm(x, g, eps=NORM_EPS):
    xf = x.astype(jnp.float32)
    y = xf * lax.rsqrt(jnp.mean(xf * xf, axis=-1, keepdims=True) + eps)
    return (y * g.astype(jnp.float32)).astype(x.dtype)


def dwconv_centred(x, w, b):
    k_w = w.shape[0]
    length = x.shape[1]
    pad = k_w // 2
    xp = jnp.pad(x, ((0, 0), (pad, pad), (0, 0)))
    y = b + xp[:, 0:length] * w[0]
    for j in range(1, k_w):
        y = y + xp[:, j:j + length] * w[j]
    return y


def token_shift_bi(p, mu_prev, mu_next):
    zero = jnp.zeros_like(p[:, :1])
    prev = jnp.concatenate([zero, p[:, :-1]], axis=1)
    nxt = jnp.concatenate([p[:, 1:], zero], axis=1)
    return p + mu_prev * (prev - p) + mu_next * (nxt - p)


def rope2d(x):
    length, dh = x.shape[1], x.shape[-1]
    n_freq = dh // 4
    t = jnp.arange(length)
    pos = jnp.stack([t // GRID_W, t % GRID_W], axis=-1).astype(jnp.float32)
    inv = ROPE_BASE ** (-jnp.arange(n_freq, dtype=jnp.float32) / n_freq)
    ang = pos[:, :, None] * inv
    cos = jnp.cos(ang)[:, None, None]
    sin = jnp.sin(ang)[:, None, None]
    xr = x.astype(jnp.float32).reshape(*x.shape[:-1], 2, 2, n_freq)
    x1, x2 = xr[..., 0, :], xr[..., 1, :]
    out = jnp.stack([x1 * cos - x2 * sin, x2 * cos + x1 * sin], axis=-2)
    return out.reshape(x.shape).astype(x.dtype)


def rwkv_prep(p, mu, w0, w_up, a0, a_up, g_up, k_k, k_a):
    bsz, length, _ = p.shape
    p = token_shift_bi(p, mu[0], mu[1]).astype(jnp.float32)
    r, k, v, wd, ad, gd = _split(p, (RW_W, RW_W, RW_W, 2 * RW_DECAY_RANK, 2 * RW_ICLR_RANK, RW_GATE_RANK))
    wd = wd.reshape(bsz, length, 2, RW_DECAY_RANK)
    ad = ad.reshape(bsz, length, 2, RW_ICLR_RANK)
    w_raw = w0 + jnp.einsum('bldr,drc->bldc', jnp.tanh(wd), w_up)
    decay = jnp.exp(-jnp.exp(-jax.nn.softplus(-w_raw) - 0.5))
    a = jax.nn.sigmoid(a0 + jnp.einsum('bldr,drc->bldc', ad, a_up))
    g = jax.nn.sigmoid(gd) @ g_up
    heads = lambda t: t.reshape(*t.shape[:-1], RW_HEADS, RW_HD)
    kk = heads(k * k_k)
    kk = kk / jnp.maximum(jnp.linalg.norm(kk, axis=-1, keepdims=True), 1e-12)
    k_dir = heads(k[:, :, None] * (1.0 + (a - 1.0) * k_a))
    return (heads(r), heads(decay), k_dir, heads(v), kk, heads(a), g)


def wkv_scan(r, w, k, v, kk, b, s0, reverse):
    def step(s, inp):
        r_t, w_t, k_t, v_t, kk_t, b_t = inp
        sa = jnp.einsum('bhvk,bhk->bhv', s, kk_t)
        s = s * w_t[:, :, None, :] - sa[..., None] * b_t[:, :, None, :] + v_t[..., None] * k_t[:, :, None, :]
        return s, jnp.einsum('bhvk,bhk->bhv', s, r_t)
    xs = tuple(jnp.swapaxes(t, 0, 1) for t in (r, w, k, v, kk, b))
    s, o = lax.scan(step, s0, xs, reverse=reverse)
    return jnp.swapaxes(o, 0, 1), s


def rwkv_bidir(prep, inits):
    r, decay, k_dir, v, kk, a, _ = prep
    b = kk[:, :, None] * a
    o_f, s_f = wkv_scan(r, decay[:, :, 0], k_dir[:, :, 0], v, kk, b[:, :, 0], inits[0], False)
    o_b, s_b = wkv_scan(r, decay[:, :, 1], k_dir[:, :, 1], v, kk, b[:, :, 1], inits[1], True)
    return o_f + o_b, (s_f, s_b)


def rwkv_readout(o, prep, r_k, ln_g, ln_b):
    r, _, k_dir, v, _, _, g = prep
    bsz, length = o.shape[:2]
    mu = jnp.mean(o, axis=-1, keepdims=True)
    var = jnp.mean(jnp.square(o - mu), axis=-1, keepdims=True)
    on = ((o - mu) * lax.rsqrt(var + RW_GN_EPS)).reshape(bsz, length, RW_W) * ln_g + ln_b
    bonus = jnp.sum(jnp.sum(r[:, :, None] * k_dir * r_k, axis=-1, keepdims=True) * v[:, :, None], axis=2)
    return (on + bonus.reshape(bsz, length, RW_W)) * g


def rwkv7_mixer(pl, pc, lp, ctx_out):
    prep_args = (lp['rw_mu'], lp['rw_w0'], lp['rw_w_up'], lp['rw_a0'], lp['rw_a_up'],
                 lp['rw_g_up'], lp['rw_k_k'], lp['rw_k_a'])
    prep_c = rwkv_prep(pc, *prep_args)
    prep_l = rwkv_prep(pl, *prep_args)
    s0 = jnp.zeros((pl.shape[0], RW_HEADS, RW_HD, RW_HD), jnp.float32)
    o_c, fin_c = rwkv_bidir(prep_c, (s0, s0))
    o_l, _ = rwkv_bidir(prep_l, fin_c)
    ro = (lp['rw_r_k'], lp['rw_ln_g'], lp['rw_ln_b'])
    out_l = rwkv_readout(o_l, prep_l, *ro)
    out_c = rwkv_readout(o_c, prep_c, *ro) if ctx_out else None
    return out_l, out_c


def ssd_prep(p, conv_w, conv_b, dt_bias):
    bsz, length, _ = p.shape
    z, xbc, dt_raw = _split(p, (SSD_W, SSD_XBC, 2 * SSD_HEADS))
    xbc = jax.nn.silu(dwconv_centred(xbc, conv_w, conv_b)).astype(jnp.float32)
    xs, bm, cm = _split(xbc, (SSD_W, SSD_GROUPS * SSD_STATE, SSD_GROUPS * SSD_STATE))
    xs = xs.reshape(bsz, length, SSD_HEADS, SSD_HD)
    bm = bm.reshape(bsz, length, SSD_GROUPS, SSD_STATE)
    cm = cm.reshape(bsz, length, SSD_GROUPS, SSD_STATE)
    dt = jax.nn.softplus(dt_raw.astype(jnp.float32).reshape(bsz, length, 2, SSD_HEADS) + dt_bias)
    return (z.astype(jnp.float32), xs, bm, cm, dt)


def ssd_chunked(x, dt, a, bm, cm, h0):
    bsz, length, n_h, n_p = x.shape
    q = SSD_CHUNK
    nc = length // q
    rep = n_h // SSD_GROUPS
    xq = x.reshape(bsz, nc, q, n_h, n_p)
    dtq = dt.reshape(bsz, nc, q, n_h)
    bh = jnp.repeat(bm, rep, axis=2).reshape(bsz, nc, q, n_h, SSD_STATE)
    ch = jnp.repeat(cm, rep, axis=2).reshape(bsz, nc, q, n_h, SSD_STATE)
    acs = jnp.cumsum(jnp.swapaxes(dtq * a, 2, 3), axis=-1)
    seg = acs[..., :, None] - acs[..., None, :]
    tri = jnp.tril(jnp.ones((q, q), dtype=bool))
    decay_ij = jnp.exp(jnp.where(tri, seg, -jnp.inf))
    scores = jnp.einsum('bcihn,bcjhn->bchij', ch, bh) * decay_ij
    y_diag = jnp.einsum('bchij,bcjh,bcjhp->bcihp', scores, dtq, xq)
    to_end = jnp.exp(acs[..., -1:] - acs)
    states = jnp.einsum('bchj,bcjh,bcjhn,bcjhp->bchpn', to_end, dtq, bh, xq)
    chunk_decay = jnp.exp(acs[..., -1])

    def step(h, inp):
        st, dec = inp
        return h * dec[..., None, None] + st, h
    h_last, h_in = lax.scan(step, h0, (jnp.swapaxes(states, 0, 1), jnp.swapaxes(chunk_decay, 0, 1)))
    h_in = jnp.swapaxes(h_in, 0, 1)
    y_off = jnp.einsum('bcihn,bchpn,bchi->bcihp', ch, h_in, jnp.exp(acs))
    return (y_diag + y_off).reshape(bsz, length, n_h, n_p), h_last


def ssd_bidir(prep, a_log, inits):
    _, xs, bm, cm, dt = prep
    a = -jnp.exp(a_log.astype(jnp.float32))
    flip = lambda t: jnp.flip(t, axis=1)
    y_f, h_f = ssd_chunked(xs, dt[:, :, 0], a[0], bm, cm, inits[0])
    y_b, h_b = ssd_chunked(flip(xs), flip(dt[:, :, 1]), a[1], flip(bm), flip(cm), inits[1])
    return y_f + flip(y_b), (h_f, h_b)


def ssd_readout(y, prep, d_skip, norm_g):
    z, xs = prep[0], prep[1]
    bsz, length = y.shape[:2]
    y = (y + d_skip[:, None] * xs).reshape(bsz, length, SSD_W)
    return rmsnorm(y * jax.nn.silu(z), norm_g)


def ssd_mixer(pl, pc, lp, ctx_out):
    prep_c = ssd_prep(pc, lp['ssd_conv_w'], lp['ssd_conv_b'], lp['ssd_dt_bias'])
    prep_l = ssd_prep(pl, lp['ssd_conv_w'], lp['ssd_conv_b'], lp['ssd_dt_bias'])
    h0 = jnp.zeros((pl.shape[0], SSD_HEADS, SSD_HD, SSD_STATE), jnp.float32)
    y_c, fin_c = ssd_bidir(prep_c, lp['ssd_a_log'], (h0, h0))
    y_l, _ = ssd_bidir(prep_l, lp['ssd_a_log'], fin_c)
    out_l = ssd_readout(y_l, prep_l, lp['ssd_d'], lp['ssd_norm_g'])
    out_c = ssd_readout(y_c, prep_c, lp['ssd_d'], lp['ssd_norm_g']) if ctx_out else None
    return out_l, out_c


def diff_attention(pl, pc, lp, layer_idx, ctx_out):
    bsz, seq, _ = pl.shape

    def qkv(p):
        q, k, v = _split(p, (DA_W, DA_W, DA_W))
        sh = (p.shape[0], p.shape[1], DA_HEADS, 2, DA_HD)
        return q.reshape(sh), k.reshape(sh), v.reshape(p.shape[0], p.shape[1], DA_HEADS, 2 * DA_HD)
    ql, kl, vl = qkv(pl)
    qc, kc, vc = qkv(pc)
    ql, kl = rope2d(ql), rope2d(kl)
    lam_init = 0.8 - 0.6 * math.exp(-0.3 * layer_idx)
    lam_p = lp['da_lambda'].astype(jnp.float32)
    lam = jnp.exp(jnp.sum(lam_p[0] * lam_p[1])) - jnp.exp(jnp.sum(lam_p[2] * lam_p[3])) + lam_init
    scale = DA_HD ** -0.5
    k_all = jnp.concatenate([kl, kc], axis=1)
    v_all = jnp.concatenate([vl, vc], axis=1)

    def attend(q, k, v):
        s = jnp.einsum('bqhmd,bkhmd->bhmqk', q, k).astype(jnp.float32) * scale
        p = jax.nn.softmax(s, axis=-1)
        a = p[:, :, 0] - lam * p[:, :, 1]
        o = jnp.einsum('bhqk,bkhe->bqhe', a, v.astype(jnp.float32))
        return rmsnorm(o, lp['da_subln_g'], eps=DA_SUBLN_EPS) * (1.0 - lam_init)
    nb = seq // DA_BLOCK
    qb = jnp.moveaxis(ql.reshape(bsz, nb, DA_BLOCK, DA_HEADS, 2, DA_HD), 1, 0)
    ol = lax.map(lambda qblk: attend(qblk, k_all, v_all), qb)
    out_l = jnp.moveaxis(ol, 0, 1).reshape(bsz, seq, DA_W)
    out_c = attend(qc, kc, vc).reshape(bsz, pc.shape[1], DA_W) if ctx_out else None
    return out_l, out_c


def neighbourhood_attention(pl, pc, lp, ctx_out):
    bsz, seq, _ = pl.shape
    rows = seq // GRID_W
    wr = min(NA_WIN_R, rows)

    def qkv(p):
        sh = (p.shape[0], p.shape[1], NA_HEADS, NA_HD)
        return tuple(t.reshape(sh) for t in _split(p, (NA_W, NA_W, NA_W)))
    ql, kl, vl = qkv(pl)
    qc, kc, vc = qkv(pc)
    scale = NA_HD ** -0.5
    rpb = lp['na_rpb'].astype(jnp.float32)
    qg = ql.reshape(bsz, rows, GRID_W, NA_HEADS, NA_HD)
    r_ids = jnp.arange(rows)
    row_start = jnp.clip(r_ids - wr // 2, 0, rows - wr)
    row_idx = row_start[:, None] + jnp.arange(wr)
    kr = kl.reshape(bsz, rows, GRID_W, NA_HEADS, NA_HD)[:, row_idx]
    vr = vl.reshape(bsz, rows, GRID_W, NA_HEADS, NA_HD)[:, row_idx]
    s_win = jnp.einsum('brqhd,brwkhd->bhrqwk', qg, kr).astype(jnp.float32) * scale
    c_ids = jnp.arange(GRID_W)
    col_start = jnp.clip(c_ids - NA_WIN_C // 2, 0, GRID_W - NA_WIN_C)
    in_win = (c_ids[None, :] >= col_start[:, None]) & (c_ids[None, :] < col_start[:, None] + NA_WIN_C)
    ri = row_idx - r_ids[:, None] + NA_WIN_R - 1
    ci = jnp.clip(c_ids[None, :] - c_ids[:, None], -(NA_WIN_C - 1), NA_WIN_C - 1) + NA_WIN_C - 1
    bias = rpb[:, ri[:, None, :, None], ci[None, :, None, :]]
    s_win = jnp.where(in_win[:, None, :], s_win + bias, -jnp.inf)
    s_ctx = jnp.einsum('brqhd,bkhd->bhrqk', qg, kc).astype(jnp.float32) * scale
    n_win = wr * GRID_W
    s = jnp.concatenate([s_win.reshape(bsz, NA_HEADS, rows, GRID_W, n_win), s_ctx], axis=-1)
    p = jax.nn.softmax(s, axis=-1)
    p_win = p[..., :n_win].reshape(bsz, NA_HEADS, rows, GRID_W, wr, GRID_W)
    p_ctx = p[..., n_win:]
    o = (jnp.einsum('bhrqwk,brwkhd->brqhd', p_win, vr.astype(jnp.float32))
         + jnp.einsum('bhrqk,bkhd->brqhd', p_ctx, vc.astype(jnp.float32)))
    out_l = o.reshape(bsz, seq, NA_W)
    out_c = None
    if ctx_out:
        sc = jnp.einsum('bqhd,bkhd->bhqk', qc, kc).astype(jnp.float32) * scale
        oc = jnp.einsum('bhqk,bkhd->bqhd', jax.nn.softmax(sc, axis=-1), vc.astype(jnp.float32))
        out_c = oc.reshape(bsz, pc.shape[1], NA_W)
    return out_l, out_c


def gated_merge(h, branches, lp):
    merged = None
    for n, o in enumerate(branches):
        gate = jax.nn.sigmoid(h @ lp['w_gate'][n] + lp['gate_b'][n])
        term = gate * (o.astype(h.dtype) @ lp['w_br'][n])
        merged = term if merged is None else merged + term
    return merged @ lp['w_out']


def token_mixing(hl, hc, lp, layer_idx, ctx_out):
    zl = _split(hl @ lp['w_in'], IN_SIZES)
    zc = _split(hc @ lp['w_in'], IN_SIZES)
    a_l, a_c = rwkv7_mixer(zl[0], zc[0], lp, ctx_out)
    b_l, b_c = ssd_mixer(zl[1], zc[1], lp, ctx_out)
    c_l, c_c = diff_attention(zl[2], zc[2], lp, layer_idx, ctx_out)
    d_l, d_c = neighbourhood_attention(zl[3], zc[3], lp, ctx_out)
    out_l = gated_merge(hl, (a_l, b_l, c_l, d_l), lp)
    out_c = gated_merge(hc, (a_c, b_c, c_c, d_c), lp) if ctx_out else None
    return out_l, out_c


def conv_ffn(h, up, conv_w, conv_b, down):
    u = dwconv_centred(h @ up, conv_w, conv_b)
    gate, val = jnp.split(u, 2, axis=-1)
    return (jax.nn.silu(gate) * val) @ down


def setup_inputs(seed: int = 0) -> dict:
    key = jax.random.key(seed)
    ks = iter(jax.random.split(key, 48))
    nrm = lambda shape, s: jax.random.normal(next(ks), shape, jnp.float32) * s
    uni = lambda shape, lo, hi: jax.random.uniform(next(ks), shape, jnp.float32, lo, hi)
    dp = DEPTH
    dt0 = jnp.exp(uni((dp, 2, SSD_HEADS), math.log(1e-3), math.log(1e-1)))
    return {
        'x': nrm((BATCH, SEQ, D_MODEL), 1.0),
        'c': nrm((BATCH, D_MODEL), 1.0),
        'ctx': nrm((BATCH, CTX_LEN, D_MODEL), 1.0),
        'c_ctx': nrm((D_MODEL,), 1.0),
        'ada_w': nrm((dp, D_MODEL, 6 * D_MODEL), 0.5 * D_MODEL ** -0.5),
        'ada_b': nrm((dp, 6 * D_MODEL), 0.02),
        'norm1_g': 1.0 + nrm((dp, D_MODEL), 0.05),
        'norm2_g': 1.0 + nrm((dp, D_MODEL), 0.05),
        'w_in': nrm((dp, D_MODEL, IN_TOTAL), D_MODEL ** -0.5),
        'rw_mu': uni((dp, 2, RW_IN), 0.0, 0.5),
        'rw_w0': uni((dp, 2, RW_W), -5.0, -0.5),
        'rw_w_up': nrm((dp, 2, RW_DECAY_RANK, RW_W), 0.5 * RW_DECAY_RANK ** -0.5),
        'rw_a0': nrm((dp, 2, RW_W), 0.5),
        'rw_a_up': nrm((dp, 2, RW_ICLR_RANK, RW_W), RW_ICLR_RANK ** -0.5),
        'rw_g_up': nrm((dp, RW_GATE_RANK, RW_W), RW_GATE_RANK ** -0.5),
        'rw_k_k': 0.85 + nrm((dp, RW_W), 0.05),
        'rw_k_a': 1.0 + nrm((dp, RW_W), 0.05),
        'rw_r_k': nrm((dp, RW_HEADS, RW_HD), 0.1),
        'rw_ln_g': 1.0 + nrm((dp, RW_W), 0.05),
        'rw_ln_b': nrm((dp, RW_W), 0.02),
        'ssd_conv_w': nrm((dp, SSD_CONV, SSD_XBC), SSD_CONV ** -0.5),
        'ssd_conv_b': nrm((dp, SSD_XBC), 0.02),
        'ssd_dt_bias': dt0 + jnp.log(-jnp.expm1(-dt0)),
        'ssd_a_log': jnp.log(uni((dp, 2, SSD_HEADS), 1.0, 16.0)),
        'ssd_d': 1.0 + nrm((dp, SSD_HEADS), 0.1),
        'ssd_norm_g': 1.0 + nrm((dp, SSD_W), 0.05),
        'da_lambda': nrm((dp, 4, DA_HD), 0.1),
        'da_subln_g': 1.0 + nrm((dp, 2 * DA_HD), 0.05),
        'na_rpb': nrm((dp, NA_HEADS, 2 * NA_WIN_R - 1, 2 * NA_WIN_C - 1), 0.02),
        'w_gate': nrm((dp, N_BRANCH, D_MODEL, D_MODEL), D_MODEL ** -0.5),
        'gate_b': nrm((dp, N_BRANCH, D_MODEL), 0.02),
        'w_br': nrm((dp, N_BRANCH, BR_W, D_MODEL), BR_W ** -0.5),
        'w_out': nrm((dp, D_MODEL, D_MODEL), D_MODEL ** -0.5),
        'ffn_up': nrm((dp, D_MODEL, 2 * D_FF), D_MODEL ** -0.5),
        'ffn_conv_w': nrm((dp, FFN_CONV, 2 * D_FF), FFN_CONV ** -0.5),
        'ffn_conv_b': nrm((dp, 2 * D_FF), 0.02),
        'ffn_down': nrm((dp, D_FF, D_MODEL), D_FF ** -0.5),
        'final_norm_g': 1.0 + nrm((D_MODEL,), 0.05),
    }


def reference(x, c, ctx, c_ctx, ada_w, ada_b, norm1_g, norm2_g, w_in, rw_mu, rw_w0, rw_w_up,
              rw_a0, rw_a_up, rw_g_up, rw_k_k, rw_k_a, rw_r_k, rw_ln_g, rw_ln_b, ssd_conv_w,
              ssd_conv_b, ssd_dt_bias, ssd_a_log, ssd_d, ssd_norm_g, da_lambda, da_subln_g, na_rpb,
              w_gate, gate_b, w_br, w_out, ffn_up, ffn_conv_w, ffn_conv_b, ffn_down, final_norm_g):
    xl, xc = x, ctx
    for i in range(DEPTH):
        last = i == DEPTH - 1
        lp = {
            'w_in': w_in[i], 'rw_mu': rw_mu[i], 'rw_w0': rw_w0[i], 'rw_w_up': rw_w_up[i],
            'rw_a0': rw_a0[i], 'rw_a_up': rw_a_up[i], 'rw_g_up': rw_g_up[i], 'rw_k_k': rw_k_k[i],
            'rw_k_a': rw_k_a[i], 'rw_r_k': rw_r_k[i], 'rw_ln_g': rw_ln_g[i], 'rw_ln_b': rw_ln_b[i],
            'ssd_conv_w': ssd_conv_w[i], 'ssd_conv_b': ssd_conv_b[i], 'ssd_dt_bias': ssd_dt_bias[i],
            'ssd_a_log': ssd_a_log[i], 'ssd_d': ssd_d[i], 'ssd_norm_g': ssd_norm_g[i],
            'da_lambda': da_lambda[i], 'da_subln_g': da_subln_g[i], 'na_rpb': na_rpb[i],
            'w_gate': w_gate[i], 'gate_b': gate_b[i], 'w_br': w_br[i], 'w_out': w_out[i],
        }
        mod_l = jax.nn.silu(c) @ ada_w[i] + ada_b[i]
        mod_c = jax.nn.silu(c_ctx) @ ada_w[i] + ada_b[i]
        ml = jnp.split(mod_l[:, None, :], 6, axis=-1)
        mc = jnp.split(mod_c[None, None, :], 6, axis=-1)
        hl = rmsnorm(xl, norm1_g[i]) * (1.0 + ml[1]) + ml[0]
        hc = rmsnorm(xc, norm1_g[i]) * (1.0 + mc[1]) + mc[0]
        ol, oc = token_mixing(hl, hc, lp, i, not last)
        xl = xl + ml[2] * ol
        hl2 = rmsnorm(xl, norm2_g[i]) * (1.0 + ml[4]) + ml[3]
        xl = xl + ml[5] * conv_ffn(hl2, ffn_up[i], ffn_conv_w[i], ffn_conv_b[i], ffn_down[i])
        if not last:
            xc = xc + mc[2] * oc
            hc2 = rmsnorm(xc, norm2_g[i]) * (1.0 + mc[4]) + mc[3]
            xc = xc + mc[5] * conv_ffn(hc2, ffn_up[i], ffn_conv_w[i], ffn_conv_b[i], ffn_down[i])
    return rmsnorm(xl, final_norm_g)
```

```python
import functools
import math

import numpy as np
import jax
import jax.numpy as jnp
from jax import lax
from jax.experimental import pallas as pl
from jax.experimental.pallas import tpu as pltpu

F32 = jnp.float32
BF16 = jnp.bfloat16

GRID_W = 64
NORM_EPS = 1e-6
HD = 64
BR_W = 512
RW_HEADS = 8
RW_RANK = 64
RW_GATE_RANK = 128
RW_GN_EPS = 64e-5
RW_IN = 3 * BR_W + 4 * RW_RANK + RW_GATE_RANK
RW_CHUNK = 64
RW_NARR = 11
SSD_HEADS = 8
SSD_GROUPS = 2
SSD_STATE = 128
SSD_XBC = BR_W + 2 * SSD_GROUPS * SSD_STATE
SSD_IN = BR_W + SSD_XBC + 2 * SSD_HEADS
SSD_IN_PAD = BR_W + SSD_XBC + 128
SSD_CHUNK = 128
DA_HEADS = 4
DA_IN = 3 * BR_W
DA_SUBLN_EPS = 1e-5
NA_HEADS = 8
NA_IN = 3 * BR_W
NA_WIN_R = 8
NA_WIN_C = 16
NA_QROWS = 4
NA_KROWS = 12
ROPE_BASE = 10000.0
NEG = -1e30
LANES = 128
VMEM_LIMIT = 56 * 1024 * 1024


def _cp(*sem):
    return pltpu.CompilerParams(dimension_semantics=sem, vmem_limit_bytes=VMEM_LIMIT)


def _tile(n, pref, mult):
    t = min(n, pref)
    t -= t % mult
    while t >= mult:
        if n % t == 0:
            return t
        t -= mult
    return n


def _dot(a, b):
    return jnp.dot(a.astype(BF16), b.astype(BF16), preferred_element_type=F32)


def _dot_nt(a, b):
    return lax.dot_general(a.astype(BF16), b.astype(BF16), (((1,), (1,)), ((), ())),
                           preferred_element_type=F32)


def _dot_tn(a, b):
    return lax.dot_general(a.astype(BF16), b.astype(BF16), (((0,), (0,)), ((), ())),
                           preferred_element_type=F32)


def _split3(x):
    hi = x.astype(BF16)
    r1 = x - hi.astype(F32)
    mid = r1.astype(BF16)
    lo = (r1 - mid.astype(F32)).astype(BF16)
    return hi, mid, lo


def _dot01(m01, x):
    hi, mid, lo = _split3(x)
    return (jnp.dot(m01, hi, preferred_element_type=F32)
            + jnp.dot(m01, mid, preferred_element_type=F32)
            + jnp.dot(m01, lo, preferred_element_type=F32))


def _x_dot01(x, m01):
    hi, mid, lo = _split3(x)
    return (jnp.dot(hi, m01, preferred_element_type=F32)
            + jnp.dot(mid, m01, preferred_element_type=F32)
            + jnp.dot(lo, m01, preferred_element_type=F32))


def _sigmoid(x):
    return 1.0 / (1.0 + jnp.exp(-x))


def _silu(x):
    return x * _sigmoid(x)


def _softplus(x):
    return jnp.maximum(x, 0.0) + jnp.log(1.0 + jnp.exp(-jnp.abs(x)))


def _mod_kernel(c_ref, w_ref, b_ref, o_ref):
    a = _silu(c_ref[...])
    o_ref[...] = _dot(a, w_ref[...]) + b_ref[...]


def _modulation(cc, w, b):
    m, d = cc.shape
    n = w.shape[1]
    tn = _tile(n, 1024, LANES)
    return pl.pallas_call(
        _mod_kernel,
        grid=(n // tn,),
        in_specs=[pl.BlockSpec((m, d), lambda j: (0, 0)),
                  pl.BlockSpec((d, tn), lambda j: (0, j)),
                  pl.BlockSpec((1, tn), lambda j: (0, j))],
        out_specs=pl.BlockSpec((m, tn), lambda j: (0, j)),
        out_shape=jax.ShapeDtypeStruct((m, n), F32),
        compiler_params=_cp("parallel"),
        name="adaln_mod",
    )(cc, w, b.reshape(1, n))


def _norm_mod_kernel(x_ref, g_ref, sh_ref, sc_ref, o_ref):
    x = x_ref[0]
    y = x * lax.rsqrt(jnp.mean(x * x, axis=-1, keepdims=True) + NORM_EPS) * g_ref[...]
    o_ref[0] = (y * (1.0 + sc_ref[0]) + sh_ref[0]).astype(o_ref.dtype)


def _norm_mod(x, g, shift, scale):
    bsz, length, d = x.shape
    tr = _tile(length, 256, 16)
    bm = shift.shape[0]
    mod_map = (lambda b, j: (b, 0, 0)) if bm == bsz else (lambda b, j: (0, 0, 0))
    return pl.pallas_call(
        _norm_mod_kernel,
        grid=(bsz, length // tr),
        in_specs=[pl.BlockSpec((1, tr, d), lambda b, j: (b, j, 0)),
                  pl.BlockSpec((1, d), lambda b, j: (0, 0)),
                  pl.BlockSpec((1, 1, d), mod_map),
                  pl.BlockSpec((1, 1, d), mod_map)],
        out_specs=pl.BlockSpec((1, tr, d), lambda b, j: (b, j, 0)),
        out_shape=jax.ShapeDtypeStruct((bsz, length, d), BF16),
        compiler_params=_cp("parallel", "parallel"),
        name="norm_mod",
    )(x, g.reshape(1, d), shift, scale)


def _final_norm_kernel(x_ref, g_ref, o_ref):
    x = x_ref[0]
    o_ref[0] = x * lax.rsqrt(jnp.mean(x * x, axis=-1, keepdims=True) + NORM_EPS) * g_ref[...]


def _final_norm(x, g):
    bsz, length, d = x.shape
    tr = _tile(length, 256, 8)
    return pl.pallas_call(
        _final_norm_kernel,
        grid=(bsz, length // tr),
        in_specs=[pl.BlockSpec((1, tr, d), lambda b, j: (b, j, 0)),
                  pl.BlockSpec((1, d), lambda b, j: (0, 0))],
        out_specs=pl.BlockSpec((1, tr, d), lambda b, j: (b, j, 0)),
        out_shape=jax.ShapeDtypeStruct((bsz, length, d), F32),
        compiler_params=_cp("parallel", "parallel"),
        name="final_norm",
    )(x, g.reshape(1, d))


def _mm_kernel(a_ref, w_ref, o_ref):
    o_ref[...] = jnp.dot(a_ref[...], w_ref[...], preferred_element_type=F32).astype(o_ref.dtype)


def _matmul(a, w, out_dtype, tm_pref=512, tn_pref=1024):
    m, k = a.shape
    n = w.shape[1]
    tm = _tile(m, tm_pref, 16)
    tn = _tile(n, tn_pref, LANES)
    return pl.pallas_call(
        _mm_kernel,
        grid=(m // tm, n // tn),
        in_specs=[pl.BlockSpec((tm, k), lambda i, j: (i, 0)),
                  pl.BlockSpec((k, tn), lambda i, j: (0, j))],
        out_specs=pl.BlockSpec((tm, tn), lambda i, j: (i, j)),
        out_shape=jax.ShapeDtypeStruct((m, n), out_dtype),
        compiler_params=_cp("parallel", "parallel"),
        name="matmul",
    )(a, w)


def _mm_res_kernel(a_ref, w_ref, x_ref, g_ref, o_ref):
    y = jnp.dot(a_ref[0], w_ref[...], preferred_element_type=F32)
    o_ref[0] = x_ref[0] + g_ref[0] * y


def _matmul_residual(a, w, x, gate):
    bsz, length, k = a.shape
    n = w.shape[1]
    tm = _tile(length, 512, 16)
    tn = _tile(n, 512, LANES)
    bm = gate.shape[0]
    gmap = (lambda b, i, j: (b, 0, j)) if bm == bsz else (lambda b, i, j: (0, 0, j))
    return pl.pallas_call(
        _mm_res_kernel,
        grid=(bsz, length // tm, n // tn),
        in_specs=[pl.BlockSpec((1, tm, k), lambda b, i, j: (b, i, 0)),
                  pl.BlockSpec((k, tn), lambda b, i, j: (0, j)),
                  pl.BlockSpec((1, tm, tn), lambda b, i, j: (b, i, j)),
                  pl.BlockSpec((1, 1, tn), gmap)],
        out_specs=pl.BlockSpec((1, tm, tn), lambda b, i, j: (b, i, j)),
        out_shape=jax.ShapeDtypeStruct((bsz, length, n), F32),
        compiler_params=_cp("parallel", "parallel", "parallel"),
        name="matmul_residual",
    )(a, w, x, gate)


def _merge_kernel(h_ref, o0_ref, o1_ref, o2_ref, o3_ref, wg_ref, gb_ref, wbr_ref, out_ref):
    h = h_ref[...]
    acc = None
    for n, o_ref in enumerate((o0_ref, o1_ref, o2_ref, o3_ref)):
        gate = _sigmoid(jnp.dot(h, wg_ref[n], preferred_element_type=F32) + gb_ref[n])
        term = gate * jnp.dot(o_ref[...], wbr_ref[n], preferred_element_type=F32)
        acc = term if acc is None else acc + term
    out_ref[...] = acc.astype(out_ref.dtype)


def _gated_merge(h, branches, wg, gb, wbr):
    m, d = h.shape
    nb = len(branches)
    tm = _tile(m, 1024, 16)
    tn = _tile(d, 256, LANES)
    bspec = pl.BlockSpec((tm, BR_W), lambda i, j: (i, 0))
    return pl.pallas_call(
        _merge_kernel,
        grid=(m // tm, d // tn),
        in_specs=[pl.BlockSpec((tm, d), lambda i, j: (i, 0)), bspec, bspec, bspec, bspec,
                  pl.BlockSpec((nb, d, tn), lambda i, j: (0, 0, j)),
                  pl.BlockSpec((nb, 1, tn), lambda i, j: (0, 0, j)),
                  pl.BlockSpec((nb, BR_W, tn), lambda i, j: (0, 0, j))],
        out_specs=pl.BlockSpec((tm, tn), lambda i, j: (i, j)),
        out_shape=jax.ShapeDtypeStruct((m, d), BF16),
        compiler_params=_cp("parallel", "parallel"),
        name="gated_merge",
    )(h, *branches, wg, gb, wbr)


def _conv3(x, prev_row, next_row, w_ref, b_ref):
    rows = x.shape[0]
    rid = lax.broadcasted_iota(jnp.int32, x.shape, 0)
    prev = jnp.where(rid == 0, prev_row, pltpu.roll(x, 1, 0))
    nxt = jnp.where(rid == rows - 1, next_row, pltpu.roll(x, rows - 1, 0))
    return b_ref[...] + prev * w_ref[0:1, :] + x * w_ref[1:2, :] + nxt * w_ref[2:3, :]


def _conv_swiglu_kernel(g_ref, gp_ref, gn_ref, v_ref, vp_ref, vn_ref, wg_ref, bg_ref,
                        wv_ref, bv_ref, o_ref, *, nt, halo):
    j = pl.program_id(1)
    has_prev = j > 0
    has_next = j < nt - 1

    def conv(x_ref, p_ref, n_ref, w_ref, b_ref):
        x = x_ref[0].astype(F32)
        prow = jnp.where(has_prev, p_ref[0, halo - 1:halo, :].astype(F32), 0.0)
        nrow = jnp.where(has_next, n_ref[0, 0:1, :].astype(F32), 0.0)
        return _conv3(x, prow, nrow, w_ref, b_ref)

    gate = conv(g_ref, gp_ref, gn_ref, wg_ref, bg_ref)
    val = conv(v_ref, vp_ref, vn_ref, wv_ref, bv_ref)
    o_ref[0] = (_silu(gate) * val).astype(o_ref.dtype)


def _conv_swiglu(u, conv_w, conv_b):
    bsz, length, f2 = u.shape
    f = f2 // 2
    halo = 16
    tr = _tile(length, 512, halo)
    tc = _tile(f, 512, LANES)
    nt = length // tr
    nc = f // tc
    nh = length // halo
    rb = tr // halo

    def main(off):
        return pl.BlockSpec((1, tr, tc), lambda b, j, c: (b, j, c + off))

    def prev(off):
        return pl.BlockSpec((1, halo, tc), lambda b, j, c: (b, jnp.maximum(j * rb - 1, 0), c + off))

    def nxt(off):
        return pl.BlockSpec((1, halo, tc), lambda b, j, c: (b, jnp.minimum((j + 1) * rb, nh - 1), c + off))

    def wspec(off):
        return pl.BlockSpec((3, tc), lambda b, j, c: (0, c + off))

    def bspec(off):
        return pl.BlockSpec((1, tc), lambda b, j, c: (0, c + off))

    cb = conv_b.reshape(1, f2)
    return pl.pallas_call(
        functools.partial(_conv_swiglu_kernel, nt=nt, halo=halo),
        grid=(bsz, nt, nc),
        in_specs=[main(0), prev(0), nxt(0), main(nc), prev(nc), nxt(nc),
                  wspec(0), bspec(0), wspec(nc), bspec(nc)],
        out_specs=pl.BlockSpec((1, tr, tc), lambda b, j, c: (b, j, c)),
        out_shape=jax.ShapeDtypeStruct((bsz, length, f), BF16),
        compiler_params=_cp("parallel", "parallel", "parallel"),
        name="conv_swiglu",
    )(u, u, u, u, u, u, conv_w, cb, conv_w, cb)


def _rw_prep_kernel(z_ref, zp_ref, zn_ref, mu_ref, w0_ref, wup_ref, a0_ref, aup_ref, gup_ref,
                    kk_ref, ka_ref, rk_ref, ones_ref, o_ref, *, nt):
    j = pl.program_id(1)
    p = z_ref[0]
    rows = p.shape[0]
    rid = lax.broadcasted_iota(jnp.int32, p.shape, 0)
    prow = jnp.where(j > 0, zp_ref[0, 7:8, :], 0.0)
    nrow = jnp.where(j < nt - 1, zn_ref[0, 0:1, :], 0.0)
    prev = jnp.where(rid == 0, prow, pltpu.roll(p, 1, 0))
    nxt = jnp.where(rid == rows - 1, nrow, pltpu.roll(p, rows - 1, 0))
    ps = p + mu_ref[0:1, :] * (prev - p) + mu_ref[1:2, :] * (nxt - p)

    w = BR_W
    r = ps[:, 0:w]
    k = ps[:, w:2 * w]
    v = ps[:, 2 * w:3 * w]
    wd = ps[:, 3 * w:3 * w + 128]
    ad = ps[:, 3 * w + 128:3 * w + 256]
    gd = ps[:, 3 * w + 256:3 * w + 384]

    wraw = _dot(jnp.tanh(wd), wup_ref[...])
    araw = _dot(ad, aup_ref[...])
    g = _dot(_sigmoid(gd), gup_ref[...])
    ones = ones_ref[...]
    kkv = k * kk_ref[...]
    ss = _x_dot01(kkv * kkv, ones)
    kkn = kkv / jnp.maximum(jnp.sqrt(ss), 1e-12)

    def put(i, x):
        o_ref[0, :, i * w:(i + 1) * w] = x

    put(0, r)
    put(1, v)
    put(2, kkn)
    ksum = None
    for d in range(2):
        lw = -math.exp(-0.5) * _sigmoid(w0_ref[d:d + 1, :] + wraw[:, d * w:(d + 1) * w])
        a = _sigmoid(a0_ref[d:d + 1, :] + araw[:, d * w:(d + 1) * w])
        kd = k * (1.0 + (a - 1.0) * ka_ref[...])
        put(3 + 3 * d, lw)
        put(4 + 3 * d, kd)
        put(5 + 3 * d, kkn * a)
        ksum = kd if ksum is None else ksum + kd
    put(9, g)
    put(10, _x_dot01(r * rk_ref[...] * ksum, ones) * v)


def _rw_prep(z, lp):
    bsz, length, fin = z.shape
    tr = _tile(length, 256, 8)
    nt = length // tr
    rb = tr // 8
    nh = length // 8
    full = lambda a: pl.BlockSpec(a.shape, lambda b, j: (0,) * a.ndim)
    params = (lp['rw_mu'], lp['rw_w0'], lp['rw_wup'], lp['rw_a0'], lp['rw_aup'], lp['rw_gup'],
              lp['rw_kk'], lp['rw_ka'], lp['rw_rk'], lp['ones_hd'])
    return pl.pallas_call(
        functools.partial(_rw_prep_kernel, nt=nt),
        grid=(bsz, nt),
        in_specs=[pl.BlockSpec((1, tr, fin), lambda b, j: (b, j, 0)),
                  pl.BlockSpec((1, 8, fin), lambda b, j: (b, jnp.maximum(j * rb - 1, 0), 0)),
                  pl.BlockSpec((1, 8, fin), lambda b, j: (b, jnp.minimum((j + 1) * rb, nh - 1), 0))]
                 + [full(a) for a in params],
        out_specs=pl.BlockSpec((1, tr, RW_NARR * BR_W), lambda b, j: (b, j, 0)),
        out_shape=jax.ShapeDtypeStruct((bsz, length, RW_NARR * BR_W), F32),
        compiler_params=_cp("parallel", "parallel"),
        name="rwkv_prep",
    )(z, z, z, *params)


def _block_same(ti, ii, shift):
    return (ti >> shift) == (ii >> shift)


def _rw_scan_kernel(r_ref, v_ref, kk_ref, lw_ref, k_ref, b_ref, s0_ref, o_ref, sfin_ref, s_scr,
                    *, nc):
    d = pl.program_id(1)
    c = pl.program_id(2)

    @pl.when(c == 0)
    def _():
        s_scr[...] = s0_ref[0, 0]

    cs = RW_CHUNK
    ti = lax.broadcasted_iota(jnp.int32, (cs, cs), 0)
    ii = lax.broadcasted_iota(jnp.int32, (cs, cs), 1)
    sgn = 1 - 2 * d
    rel = (ii - ti) * sgn
    strict = rel < 0
    incl = rel <= 0
    eye = jnp.where(ii == ti, 1.0, 0.0)
    tri = jnp.where(incl, 1.0, 0.0).astype(BF16)

    lw = lw_ref[0]
    cum = _dot01(tri, lw)
    tot = jnp.sum(lw, axis=0, keepdims=True)
    qt = kk_ref[0] * jnp.exp(cum - lw)
    rt = r_ref[0] * jnp.exp(cum)
    e_neg = jnp.exp(-cum)
    e_end = jnp.exp(tot - cum)
    kt = k_ref[0] * e_neg
    bt = b_ref[0] * e_neg
    kh = k_ref[0] * e_end
    bh = b_ref[0] * e_end
    p_tot = jnp.exp(tot)
    v = v_ref[0]

    same = {s: _block_same(ti, ii, int(math.log2(s))) for s in (8, 16, 32)}

    for h in range(RW_HEADS):
        sl = slice(h * HD, (h + 1) * HD)
        s0 = s_scr[h]
        qh, rh, vh = qt[:, sl], rt[:, sl], v[:, sl]
        kth, bth = kt[:, sl], bt[:, sl]
        a_qk = jnp.where(strict, _dot_nt(qh, kth), 0.0)
        lmat = jnp.where(strict, _dot_nt(qh, bth), 0.0)
        a_rk = jnp.where(incl, _dot_nt(rh, kth), 0.0)
        a_rb = jnp.where(incl, _dot_nt(rh, bth), 0.0)
        ld = jnp.where(same[8], lmat, 0.0)
        ld2 = _dot(ld, ld)
        ld4 = _dot(ld2, ld2)
        x = _dot(_dot(eye - ld, eye + ld2), eye + ld4)
        for s in (8, 16, 32):
            outer = same[2 * s] if 2 * s < cs else None
            off_mask = jnp.logical_not(same[s]) if outer is None else jnp.logical_and(
                outer, jnp.logical_not(same[s]))
            x = x - _dot(_dot(x, jnp.where(off_mask, lmat, 0.0)), x)
        sa = _dot(x, _dot_nt(qh, s0) + _dot(a_qk, vh))
        o = _dot_nt(rh, s0) + _dot(a_rk, vh) - _dot(a_rb, sa)
        o_ref[0, 0, :, sl] = o
        s_scr[h] = s0 * p_tot[:, sl] + _dot_tn(vh, kh[:, sl]) - _dot_tn(sa, bh[:, sl])

    @pl.when(c == nc - 1)
    def _():
        sfin_ref[0, 0] = s_scr[...]


def _rw_scan(rwp, s_init):
    bsz, length, _ = rwp.shape
    cs = RW_CHUNK
    nc = length // cs

    def cidx(d, c):
        return c + d * (nc - 1 - 2 * c)

    def arr(i):
        return pl.BlockSpec((1, cs, BR_W), lambda b, d, c: (b, cidx(d, c), i))

    def darr(i):
        return pl.BlockSpec((1, cs, BR_W), lambda b, d, c: (b, cidx(d, c), 3 + 3 * d + i))

    sspec = pl.BlockSpec((1, 1, RW_HEADS, HD, HD), lambda b, d, c: (b, d, 0, 0, 0))
    return pl.pallas_call(
        functools.partial(_rw_scan_kernel, nc=nc),
        grid=(bsz, 2, nc),
        in_specs=[arr(0), arr(1), arr(2), darr(0), darr(1), darr(2), sspec],
        out_specs=[pl.BlockSpec((1, 1, cs, BR_W), lambda b, d, c: (b, d, cidx(d, c), 0)), sspec],
        out_shape=[jax.ShapeDtypeStruct((bsz, 2, length, BR_W), F32),
                   jax.ShapeDtypeStruct((bsz, 2, RW_HEADS, HD, HD), F32)],
        scratch_shapes=[pltpu.VMEM((RW_HEADS, HD, HD), F32)],
        compiler_params=_cp("parallel", "parallel", "arbitrary"),
        name="rwkv_scan",
    )(rwp, rwp, rwp, rwp, rwp, rwp, s_init)


def _rw_readout_kernel(o_ref, g_ref, bonus_ref, lng_ref, lnb_ref, mean_ref, out_ref):
    o = o_ref[0, 0] + o_ref[0, 1]
    mean_m = mean_ref[...]
    mu = _x_dot01(o, mean_m)
    dlt = o - mu
    var = _x_dot01(dlt * dlt, mean_m)
    on = dlt * lax.rsqrt(var + RW_GN_EPS) * lng_ref[...] + lnb_ref[...]
    out_ref[0] = ((on + bonus_ref[0]) * g_ref[0]).astype(out_ref.dtype)


def _rw_readout(o, rwp, lp):
    bsz, _, length, w = o.shape
    tr = _tile(length, 256, 16)
    return pl.pallas_call(
        _rw_readout_kernel,
        grid=(bsz, length // tr),
        in_specs=[pl.BlockSpec((1, 2, tr, w), lambda b, j: (b, 0, j, 0)),
                  pl.BlockSpec((1, tr, w), lambda b, j: (b, j, 9)),
                  pl.BlockSpec((1, tr, w), lambda b, j: (b, j, 10)),
                  pl.BlockSpec((1, w), lambda b, j: (0, 0)),
                  pl.BlockSpec((1, w), lambda b, j: (0, 0)),
                  pl.BlockSpec((w, w), lambda b, j: (0, 0))],
        out_specs=pl.BlockSpec((1, tr, w), lambda b, j: (b, j, 0)),
        out_shape=jax.ShapeDtypeStruct((bsz, length, w), BF16),
        compiler_params=_cp("parallel", "parallel"),
        name="rwkv_readout",
    )(o, rwp, rwp, lp['rw_lng'], lp['rw_lnb'], lp['mean_hd'])


def _ssd_prep_kernel(z_ref, zp_ref, zn_ref, cw_ref, cb_ref, dtb_ref, xbc_ref, dt_ref, *, nt):
    j = pl.program_id(1)
    x = z_ref[0, :, BR_W:BR_W + SSD_XBC]
    prow = jnp.where(j > 0, zp_ref[0, 7:8, BR_W:BR_W + SSD_XBC], 0.0)
    nrow = jnp.where(j < nt - 1, zn_ref[0, 0:1, BR_W:BR_W + SSD_XBC], 0.0)
    xbc_ref[0] = _silu(_conv3(x, prow, nrow, cw_ref, cb_ref))
    dt_raw = z_ref[0, :, BR_W + SSD_XBC:BR_W + SSD_XBC + LANES]
    dt_ref[0, 0] = _softplus(dt_raw + dtb_ref[0:1, :])
    dt_ref[0, 1] = _softplus(pltpu.roll(dt_raw, LANES - SSD_HEADS, 1) + dtb_ref[1:2, :])


def _ssd_prep(z, lp):
    bsz, length, fin = z.shape
    tr = _tile(length, 256, 8)
    nt = length // tr
    rb = tr // 8
    nh = length // 8
    return pl.pallas_call(
        functools.partial(_ssd_prep_kernel, nt=nt),
        grid=(bsz, nt),
        in_specs=[pl.BlockSpec((1, tr, fin), lambda b, j: (b, j, 0)),
                  pl.BlockSpec((1, 8, fin), lambda b, j: (b, jnp.maximum(j * rb - 1, 0), 0)),
                  pl.BlockSpec((1, 8, fin), lambda b, j: (b, jnp.minimum((j + 1) * rb, nh - 1), 0)),
                  pl.BlockSpec((3, SSD_XBC), lambda b, j: (0, 0)),
                  pl.BlockSpec((1, SSD_XBC), lambda b, j: (0, 0)),
                  pl.BlockSpec((2, LANES), lambda b, j: (0, 0))],
        out_specs=[pl.BlockSpec((1, tr, SSD_XBC), lambda b, j: (b, j, 0)),
                   pl.BlockSpec((1, 2, tr, LANES), lambda b, j: (b, 0, j, 0))],
        out_shape=[jax.ShapeDtypeStruct((bsz, length, SSD_XBC), F32),
                   jax.ShapeDtypeStruct((bsz, 2, length, LANES), F32)],
        compiler_params=_cp("parallel", "parallel"),
        name="ssd_prep",
    )(z, z, z, lp['ssd_cw'], lp['ssd_cb'], lp['ssd_dtb'])


def _ssd_scan_kernel(xbc_ref, dt_ref, a_ref, h0_ref, y_ref, hfin_ref, h_scr, *, nc):
    d = pl.program_id(1)
    c = pl.program_id(2)

    @pl.when(c == 0)
    def _():
        h_scr[...] = h0_ref[0, 0]

    q = SSD_CHUNK
    ti = lax.broadcasted_iota(jnp.int32, (q, q), 0)
    ii = lax.broadcasted_iota(jnp.int32, (q, q), 1)
    incl = (ii - ti) * (1 - 2 * d) <= 0
    tri = jnp.where(incl, 1.0, 0.0).astype(BF16)

    dt = dt_ref[0, 0]
    dta = dt * a_ref[0]
    acs = _dot01(tri, dta)
    acs_t = acs.T
    tot = jnp.sum(dta, axis=0, keepdims=True)
    e_acs = jnp.exp(acs)
    e_end = jnp.exp(tot - acs)
    e_tot = jnp.exp(tot)
    xbc = xbc_ref[0]
    n = SSD_STATE
    rep = SSD_HEADS // SSD_GROUPS
    for g in range(SSD_GROUPS):
        bg = xbc[:, BR_W + g * n:BR_W + (g + 1) * n]
        cg = xbc[:, BR_W + SSD_GROUPS * n + g * n:BR_W + SSD_GROUPS * n + (g + 1) * n]
        cb = _dot_nt(cg, bg)
        for hh in range(rep):
            h = g * rep + hh
            a_col = acs[:, h:h + 1]
            dec = jnp.exp(jnp.where(incl, a_col - acs_t[h:h + 1, :], NEG))
            xdt = xbc[:, h * HD:(h + 1) * HD] * dt[:, h:h + 1]
            ht = h_scr[h]
            y = _dot(cb * dec, xdt) + e_acs[:, h:h + 1] * _dot(cg, ht)
            y_ref[0, 0, :, h * HD:(h + 1) * HD] = y
            h_scr[h] = ht * e_tot[:, h:h + 1] + _dot_tn(bg, xdt * e_end[:, h:h + 1])

    @pl.when(c == nc - 1)
    def _():
        hfin_ref[0, 0] = h_scr[...]


def _ssd_scan(xbc, dt, a_neg, h_init):
    bsz, length, _ = xbc.shape
    q = SSD_CHUNK
    nc = length // q

    def cidx(d, c):
        return c + d * (nc - 1 - 2 * c)

    hspec = pl.BlockSpec((1, 1, SSD_HEADS, SSD_STATE, HD), lambda b, d, c: (b, d, 0, 0, 0))
    return pl.pallas_call(
        functools.partial(_ssd_scan_kernel, nc=nc),
        grid=(bsz, 2, nc),
        in_specs=[pl.BlockSpec((1, q, SSD_XBC), lambda b, d, c: (b, cidx(d, c), 0)),
                  pl.BlockSpec((1, 1, q, LANES), lambda b, d, c: (b, d, cidx(d, c), 0)),
                  pl.BlockSpec((1, 1, LANES), lambda b, d, c: (d, 0, 0)),
                  hspec],
        out_specs=[pl.BlockSpec((1, 1, q, BR_W), lambda b, d, c: (b, d, cidx(d, c), 0)), hspec],
        out_shape=[jax.ShapeDtypeStruct((bsz, 2, length, BR_W), F32),
                   jax.ShapeDtypeStruct((bsz, 2, SSD_HEADS, SSD_STATE, HD), F32)],
        scratch_shapes=[pltpu.VMEM((SSD_HEADS, SSD_STATE, HD), F32)],
        compiler_params=_cp("parallel", "parallel", "arbitrary"),
        name="ssd_scan",
    )(xbc, dt, a_neg, h_init)


def _ssd_readout_kernel(y_ref, xs_ref, z_ref, dsk_ref, ng_ref, out_ref):
    y = y_ref[0, 0] + y_ref[0, 1] + dsk_ref[...] * xs_ref[0]
    y = y * _silu(z_ref[0])
    y = y * lax.rsqrt(jnp.mean(y * y, axis=-1, keepdims=True) + NORM_EPS) * ng_ref[...]
    out_ref[0] = y.astype(out_ref.dtype)


def _ssd_readout(y, xbc, z, lp):
    bsz, _, length, w = y.shape
    tr = _tile(length, 256, 16)
    return pl.pallas_call(
        _ssd_readout_kernel,
        grid=(bsz, length // tr),
        in_specs=[pl.BlockSpec((1, 2, tr, w), lambda b, j: (b, 0, j, 0)),
                  pl.BlockSpec((1, tr, w), lambda b, j: (b, j, 0)),
                  pl.BlockSpec((1, tr, w), lambda b, j: (b, j, 0)),
                  pl.BlockSpec((1, w), lambda b, j: (0, 0)),
                  pl.BlockSpec((1, w), lambda b, j: (0, 0))],
        out_specs=pl.BlockSpec((1, tr, w), lambda b, j: (b, j, 0)),
        out_shape=jax.ShapeDtypeStruct((bsz, length, w), BF16),
        compiler_params=_cp("parallel", "parallel"),
        name="ssd_readout",
    )(y, xbc, z, lp['ssd_dskip'], lp['ssd_ng'])


def _da_prep_kernel(z_ref, cos_ref, sin_ref, q_ref, k_ref, v_ref, *, rope):
    w = BR_W

    def rot(x):
        if not rope:
            return x
        lane = lax.broadcasted_iota(jnp.int32, (1, LANES), 1)
        first = (lane & 31) < 16
        outs = []
        for cb in range(w // LANES):
            xb = x[:, cb * LANES:(cb + 1) * LANES]
            sw = jnp.where(first, pltpu.roll(xb, LANES - 16, 1), pltpu.roll(xb, 16, 1))
            outs.append(xb * cos_ref[:, cb * LANES:(cb + 1) * LANES]
                        + sw * sin_ref[:, cb * LANES:(cb + 1) * LANES])
        return jnp.concatenate(outs, axis=1)

    q_ref[0] = (rot(z_ref[0, :, 0:w]) * (HD ** -0.5)).astype(q_ref.dtype)
    k_ref[0] = rot(z_ref[0, :, w:2 * w]).astype(k_ref.dtype)
    v_ref[0] = z_ref[0, :, 2 * w:3 * w].astype(v_ref.dtype)


def _da_prep(z, cos, sin, rope):
    bsz, length, fin = z.shape
    tr = _tile(length, 256, 16)
    w = BR_W
    ospec = pl.BlockSpec((1, tr, w), lambda b, j: (b, j, 0))
    oshape = jax.ShapeDtypeStruct((bsz, length, w), BF16)
    return pl.pallas_call(
        functools.partial(_da_prep_kernel, rope=rope),
        grid=(bsz, length // tr),
        in_specs=[pl.BlockSpec((1, tr, fin), lambda b, j: (b, j, 0)),
                  pl.BlockSpec((tr, w), lambda b, j: (j, 0)),
                  pl.BlockSpec((tr, w), lambda b, j: (j, 0))],
        out_specs=[ospec, ospec, ospec],
        out_shape=[oshape, oshape, oshape],
        compiler_params=_cp("parallel", "parallel"),
        name="da_prep",
    )(z, cos, sin)


def _da_attn_kernel(*refs, nsrc, lam_init):
    lam_ref, q_ref = refs[0], refs[1]
    kv = refs[2:2 + 2 * nsrc]
    g_ref, o_ref = refs[2 + 2 * nsrc], refs[3 + 2 * nsrc]
    q = q_ref[0]
    lane = lax.broadcasted_iota(jnp.int32, (1, LANES), 1)
    zero = jnp.zeros_like(q)
    probs = []
    for m in range(2):
        qm = jnp.where((lane < HD) if m == 0 else (lane >= HD), q, zero)
        ss = [_dot_nt(qm, kv[2 * i][0]) for i in range(nsrc)]
        mx = functools.reduce(jnp.maximum, [jnp.max(s, axis=-1, keepdims=True) for s in ss])
        es = [jnp.exp(s - mx) for s in ss]
        den = functools.reduce(lambda a, b: a + b, [jnp.sum(e, axis=-1, keepdims=True) for e in es])
        inv = 1.0 / den
        probs.append([e * inv for e in es])
    lam = lam_ref[0:1, 0:1]
    o = None
    for i in range(nsrc):
        a = probs[0][i] - lam * probs[1][i]
        t = _dot(a, kv[2 * i + 1][0])
        o = t if o is None else o + t
    y = o * lax.rsqrt(jnp.mean(o * o, axis=-1, keepdims=True) + DA_SUBLN_EPS) * g_ref[...]
    o_ref[0] = (y * (1.0 - lam_init)).astype(o_ref.dtype)


def _da_attn(q, ks, vs, lam, subln_g, lam_init):
    bsz, lq, _ = q.shape
    tq = _tile(lq, 256, 16)
    nsrc = len(ks)
    in_specs = [pl.BlockSpec((1, LANES), lambda b, h, j: (0, 0)),
                pl.BlockSpec((1, tq, LANES), lambda b, h, j: (b, j, h))]
    args = [lam, q]
    for k, v in zip(ks, vs):
        lk = k.shape[1]
        in_specs.append(pl.BlockSpec((1, lk, LANES), lambda b, h, j: (b, 0, h)))
        in_specs.append(pl.BlockSpec((1, lk, LANES), lambda b, h, j: (b, 0, h)))
        args += [k, v]
    in_specs.append(pl.BlockSpec((1, LANES), lambda b, h, j: (0, 0)))
    args.append(subln_g)
    return pl.pallas_call(
        functools.partial(_da_attn_kernel, nsrc=nsrc, lam_init=lam_init),
        grid=(bsz, DA_HEADS, lq // tq),
        in_specs=in_specs,
        out_specs=pl.BlockSpec((1, tq, LANES), lambda b, h, j: (b, j, h)),
        out_shape=jax.ShapeDtypeStruct((bsz, lq, BR_W), BF16),
        compiler_params=_cp("parallel", "parallel", "parallel"),
        name="diff_attn",
    )(*args)


def _na_kb(i, rows):
    return jnp.clip(i * NA_QROWS - NA_WIN_R // 2, 0, rows - NA_KROWS)


def _na_attn_kernel(q_ref, k_ref, v_ref, kc_ref, vc_ref, bias_ref, o_ref, *, rows):
    i = pl.program_id(2)
    start = pl.multiple_of(_na_kb(i, rows) * GRID_W, GRID_W)
    nk = NA_KROWS * GRID_W
    q = q_ref[0] * (HD ** -0.5)
    kw = k_ref[0, pl.ds(start, nk), :]
    vw = v_ref[0, pl.ds(start, nk), :]
    kc = kc_ref[0]
    vc = vc_ref[0]
    lane = lax.broadcasted_iota(jnp.int32, (1, LANES), 1)
    outs = []
    for h in range(2):
        qm = jnp.where((lane < HD) if h == 0 else (lane >= HD), q, 0.0)
        sw = _dot_nt(qm, kw) + bias_ref[h, 0]
        sc = _dot_nt(qm, kc)
        mx = jnp.maximum(jnp.max(sw, axis=-1, keepdims=True), jnp.max(sc, axis=-1, keepdims=True))
        ew = jnp.exp(sw - mx)
        ec = jnp.exp(sc - mx)
        inv = 1.0 / (jnp.sum(ew, axis=-1, keepdims=True) + jnp.sum(ec, axis=-1, keepdims=True))
        outs.append(_dot(ew * inv, vw) + _dot(ec * inv, vc))
    o_ref[0] = jnp.where(lane < HD, outs[0], outs[1]).astype(o_ref.dtype)


def _na_attn(z_l, z_c, bias):
    bsz, length, _ = z_l.shape
    lc = z_c.shape[1]
    rows = length // GRID_W
    tq = NA_QROWS * GRID_W
    nblk = length // tq
    nkb = BR_W // LANES
    return pl.pallas_call(
        functools.partial(_na_attn_kernel, rows=rows),
        grid=(bsz, NA_HEADS // 2, nblk),
        in_specs=[pl.BlockSpec((1, tq, LANES), lambda b, p, i: (b, i, p)),
                  pl.BlockSpec((1, length, LANES), lambda b, p, i: (b, 0, nkb + p)),
                  pl.BlockSpec((1, length, LANES), lambda b, p, i: (b, 0, 2 * nkb + p)),
                  pl.BlockSpec((1, lc, LANES), lambda b, p, i: (b, 0, nkb + p)),
                  pl.BlockSpec((1, lc, LANES), lambda b, p, i: (b, 0, 2 * nkb + p)),
                  pl.BlockSpec((2, 1, tq, NA_KROWS * GRID_W), lambda b, p, i: (p, i, 0, 0))],
        out_specs=pl.BlockSpec((1, tq, LANES), lambda b, p, i: (b, i, p)),
        out_shape=jax.ShapeDtypeStruct((bsz, length, BR_W), BF16),
        compiler_params=_cp("parallel", "parallel", "parallel"),
        name="na_attn",
    )(z_l, z_l, z_l, z_c, z_c, bias)


def _ctx_attn_kernel(q_ref, k_ref, v_ref, o_ref):
    q = q_ref[0] * (HD ** -0.5)
    k = k_ref[0]
    v = v_ref[0]
    lane = lax.broadcasted_iota(jnp.int32, (1, LANES), 1)
    outs = []
    for h in range(2):
        qm = jnp.where((lane < HD) if h == 0 else (lane >= HD), q, 0.0)
        s = _dot_nt(qm, k)
        e = jnp.exp(s - jnp.max(s, axis=-1, keepdims=True))
        outs.append(_dot(e * (1.0 / jnp.sum(e, axis=-1, keepdims=True)), v))
    o_ref[0] = jnp.where(lane < HD, outs[0], outs[1]).astype(o_ref.dtype)


def _ctx_attn(z_c):
    bsz, lc, _ = z_c.shape
    nkb = BR_W // LANES
    return pl.pallas_call(
        _ctx_attn_kernel,
        grid=(bsz, NA_HEADS // 2),
        in_specs=[pl.BlockSpec((1, lc, LANES), lambda b, p: (b, 0, p)),
                  pl.BlockSpec((1, lc, LANES), lambda b, p: (b, 0, nkb + p)),
                  pl.BlockSpec((1, lc, LANES), lambda b, p: (b, 0, 2 * nkb + p))],
        out_specs=pl.BlockSpec((1, lc, LANES), lambda b, p: (b, 0, p)),
        out_shape=jax.ShapeDtypeStruct((bsz, lc, BR_W), BF16),
        compiler_params=_cp("parallel", "parallel"),
        name="ctx_attn",
    )(z_c, z_c, z_c)


def _rope_tables(length):
    n_freq = HD // 4
    t = np.arange(length)
    pos = np.stack([t // GRID_W, t % GRID_W], axis=-1).astype(np.float32)
    inv = (ROPE_BASE ** (-np.arange(n_freq, dtype=np.float32) / n_freq)).astype(np.float32)
    lane = np.arange(BR_W)
    which = (lane % HD) // (HD // 2)
    ang = pos[:, which] * inv[lane % n_freq][None, :]
    sign = np.where((lane % (HD // 2)) < n_freq, -1.0, 1.0).astype(np.float32)
    return jnp.asarray(np.cos(ang), F32), jnp.asarray(np.sin(ang) * sign[None, :], F32)


def _na_bias(rpb, length):
    rows = length // GRID_W
    wr = min(NA_WIN_R, rows)
    nblk = rows // NA_QROWS
    qr = np.arange(rows)
    rstart = np.clip(qr - wr // 2, 0, rows - wr)
    kb = np.clip(np.arange(nblk) * NA_QROWS - NA_WIN_R // 2, 0, rows - NA_KROWS)
    qrow = (np.arange(nblk)[:, None] * NA_QROWS + np.arange(NA_QROWS)[None, :])
    krow = kb[:, None] + np.arange(NA_KROWS)[None, :]
    dr = krow[:, None, :] - qrow[:, :, None] + NA_WIN_R - 1
    rvalid = (krow[:, None, :] >= rstart[qrow][:, :, None]) & (krow[:, None, :] < rstart[qrow][:, :, None] + wr)
    n_dr = 2 * NA_WIN_R - 1
    oh_dr = (np.clip(dr, 0, n_dr - 1)[..., None] == np.arange(n_dr)) & rvalid[..., None]
    cid = np.arange(GRID_W)
    cstart = np.clip(cid - NA_WIN_C // 2, 0, GRID_W - NA_WIN_C)
    in_win = (cid[None, :] >= cstart[:, None]) & (cid[None, :] < cstart[:, None] + NA_WIN_C)
    ci = np.clip(cid[None, :] - cid[:, None], -(NA_WIN_C - 1), NA_WIN_C - 1) + NA_WIN_C - 1
    n_ci = 2 * NA_WIN_C - 1
    oh_ci = (ci[..., None] == np.arange(n_ci)).astype(np.float32)
    t1 = jnp.einsum('hdc,qkc->hdqk', rpb.astype(F32), jnp.asarray(oh_ci),
                    precision=lax.Precision.HIGHEST)
    big = jnp.einsum('brkd,hdxy->hbrxky', jnp.asarray(oh_dr.astype(np.float32)), t1,
                     precision=lax.Precision.HIGHEST)
    valid = rvalid[:, :, None, :, None] & in_win[None, None, :, None, :]
    big = jnp.where(jnp.asarray(valid)[None], big, NEG)
    return big.reshape(rpb.shape[0], nblk, NA_QROWS * GRID_W, NA_KROWS * GRID_W)


def _block_diag(blocks):
    n = len(blocks)
    r, c = blocks[0].shape
    out = jnp.zeros((n * r, n * c), blocks[0].dtype)
    for i, blk in enumerate(blocks):
        out = out.at[i * r:(i + 1) * r, i * c:(i + 1) * c].set(blk)
    return out


def _layer_params(p, i):
    w = BR_W
    w_in = p['w_in'][i]
    o1, o2, o3 = RW_IN, RW_IN + SSD_IN, RW_IN + SSD_IN + DA_IN
    d = w_in.shape[0]
    w_ssd = jnp.concatenate([w_in[:, o1:o2], jnp.zeros((d, SSD_IN_PAD - SSD_IN), w_in.dtype)], axis=1)
    head_id = np.arange(w) // HD
    ones_hd = jnp.asarray((head_id[:, None] == head_id[None, :]).astype(np.float32), BF16)
    lam_p = p['da_lambda'][i].astype(F32)
    lam_init = 0.8 - 0.6 * math.exp(-0.3 * i)
    lam = jnp.exp(jnp.sum(lam_p[0] * lam_p[1])) - jnp.exp(jnp.sum(lam_p[2] * lam_p[3])) + lam_init
    dtb = p['ssd_dt_bias'][i]
    return {
        'w_rw': w_in[:, :o1].astype(BF16), 'w_ssd': w_ssd.astype(BF16),
        'w_da': w_in[:, o2:o3].astype(BF16), 'w_na': w_in[:, o3:].astype(BF16),
        'rw_mu': p['rw_mu'][i], 'rw_w0': p['rw_w0'][i],
        'rw_wup': _block_diag([p['rw_w_up'][i, 0], p['rw_w_up'][i, 1]]).astype(BF16),
        'rw_a0': p['rw_a0'][i],
        'rw_aup': _block_diag([p['rw_a_up'][i, 0], p['rw_a_up'][i, 1]]).astype(BF16),
        'rw_gup': p['rw_g_up'][i].astype(BF16),
        'rw_kk': p['rw_k_k'][i].reshape(1, w), 'rw_ka': p['rw_k_a'][i].reshape(1, w),
        'rw_rk': p['rw_r_k'][i].reshape(1, w),
        'rw_lng': p['rw_ln_g'][i].reshape(1, w), 'rw_lnb': p['rw_ln_b'][i].reshape(1, w),
        'ones_hd': ones_hd, 'mean_hd': (ones_hd.astype(F32) / HD).astype(BF16),
        'ssd_cw': p['ssd_conv_w'][i], 'ssd_cb': p['ssd_conv_b'][i].reshape(1, SSD_XBC),
        'ssd_dtb': jnp.pad(dtb, ((0, 0), (0, LANES - SSD_HEADS))),
        'ssd_a': jnp.pad(-jnp.exp(p['ssd_a_log'][i].astype(F32)), ((0, 0), (0, LANES - SSD_HEADS))).reshape(2, 1, LANES),
        'ssd_dskip': jnp.repeat(p['ssd_d'][i], HD).reshape(1, w),
        'ssd_ng': p['ssd_norm_g'][i].reshape(1, w),
        'da_lam': jnp.broadcast_to(lam.reshape(1, 1), (1, LANES)).astype(F32),
        'da_lam_init': lam_init,
        'da_g': p['da_subln_g'][i].reshape(1, 2 * HD),
        'na_rpb': p['na_rpb'][i],
        'w_gate': p['w_gate'][i].astype(BF16),
        'gate_b': p['gate_b'][i][:, None, :],
        'w_br': p['w_br'][i].astype(BF16), 'w_out': p['w_out'][i].astype(BF16),
        'ffn_up': p['ffn_up'][i].astype(BF16), 'ffn_cw': p['ffn_conv_w'][i],
        'ffn_cb': p['ffn_conv_b'][i], 'ffn_down': p['ffn_down'][i].astype(BF16),
    }


def _project(h, lp):
    bsz, length, d = h.shape
    h2 = h.reshape(bsz * length, d)
    return tuple(_matmul(h2, lp[n], F32).reshape(bsz, length, -1)
                 for n in ('w_rw', 'w_ssd', 'w_da', 'w_na'))


def _ffn(x, h2, lp, gate):
    bsz, length, d = h2.shape
    u = _matmul(h2.reshape(bsz * length, d), lp['ffn_up'], BF16).reshape(bsz, length, -1)
    act = _conv_swiglu(u, lp['ffn_cw'], lp['ffn_cb'])
    return _matmul_residual(act, lp['ffn_down'], x, gate)


def kernel(x, c, ctx, c_ctx, ada_w, ada_b, norm1_g, norm2_g, w_in, rw_mu, rw_w0, rw_w_up, rw_a0, rw_a_up, rw_g_up, rw_k_k, rw_k_a, rw_r_k, rw_ln_g, rw_ln_b, ssd_conv_w, ssd_conv_b, ssd_dt_bias, ssd_a_log, ssd_d, ssd_norm_g, da_lambda, da_subln_g, na_rpb, w_gate, gate_b, w_br, w_out, ffn_up, ffn_conv_w, ffn_conv_b, ffn_down, final_norm_g):
    p = dict(w_in=w_in, rw_mu=rw_mu, rw_w0=rw_w0, rw_w_up=rw_w_up, rw_a0=rw_a0, rw_a_up=rw_a_up,
             rw_g_up=rw_g_up, rw_k_k=rw_k_k, rw_k_a=rw_k_a, rw_r_k=rw_r_k, rw_ln_g=rw_ln_g,
             rw_ln_b=rw_ln_b, ssd_conv_w=ssd_conv_w, ssd_conv_b=ssd_conv_b, ssd_dt_bias=ssd_dt_bias,
             ssd_a_log=ssd_a_log, ssd_d=ssd_d, ssd_norm_g=ssd_norm_g, da_lambda=da_lambda,
             da_subln_g=da_subln_g, na_rpb=na_rpb, w_gate=w_gate, gate_b=gate_b, w_br=w_br,
             w_out=w_out, ffn_up=ffn_up, ffn_conv_w=ffn_conv_w, ffn_conv_b=ffn_conv_b,
             ffn_down=ffn_down)
    bsz, seq, d = x.shape
    depth = ada_w.shape[0]
    lctx = ctx.shape[1]
    mrows = -(-(bsz + 1) // 16) * 16
    cc = jnp.zeros((mrows, d), F32).at[:bsz].set(c).at[bsz].set(c_ctx)
    cos, sin = _rope_tables(seq)
    xl, xc = x, ctx
    for i in range(depth):
        last = i == depth - 1
        lp = _layer_params(p, i)
        mod = _modulation(cc, ada_w[i], ada_b[i]).reshape(mrows, 6, d)
        ml = [mod[:bsz, n][:, None, :] for n in range(6)]
        mc = [mod[bsz:bsz + 1, n][:, None, :] for n in range(6)]

        hl = _norm_mod(xl, norm1_g[i], ml[0], ml[1])
        hc = _norm_mod(xc, norm1_g[i], mc[0], mc[1])
        zl = _project(hl, lp)
        zc = _project(hc, lp)

        rp_c = _rw_prep(zc[0], lp)
        rp_l = _rw_prep(zl[0], lp)
        s_zero = jnp.zeros((bsz, 2, RW_HEADS, HD, HD), F32)
        o_c, s_c = _rw_scan(rp_c, s_zero)
        o_l, _ = _rw_scan(rp_l, s_c)
        a_l = _rw_readout(o_l, rp_l, lp)

        xbc_c, dt_c = _ssd_prep(zc[1], lp)
        xbc_l, dt_l = _ssd_prep(zl[1], lp)
        h_zero = jnp.zeros((bsz, 2, SSD_HEADS, SSD_STATE, HD), F32)
        y_c, hfin_c = _ssd_scan(xbc_c, dt_c, lp['ssd_a'], h_zero)
        y_l, _ = _ssd_scan(xbc_l, dt_l, lp['ssd_a'], hfin_c)
        b_l = _ssd_readout(y_l, xbc_l, zl[1], lp)

        q_l, k_l, v_l = _da_prep(zl[2], cos, sin, True)
        q_c, k_c, v_c = _da_prep(zc[2], cos, sin, False)
        c_l = _da_attn(q_l, [k_l, k_c], [v_l, v_c], lp['da_lam'], lp['da_g'], lp['da_lam_init'])

        d_l = _na_attn(zl[3], zc[3], _na_bias(lp['na_rpb'], seq))

        merged = _gated_merge(hl.reshape(bsz * seq, d),
                              [t.reshape(bsz * seq, BR_W) for t in (a_l, b_l, c_l, d_l)],
                              lp['w_gate'], lp['gate_b'], lp['w_br']).reshape(bsz, seq, d)
        xl = _matmul_residual(merged, lp['w_out'], xl, ml[2])
        hl2 = _norm_mod(xl, norm2_g[i], ml[3], ml[4])
        xl = _ffn(xl, hl2, lp, ml[5])

        if not last:
            a_c = _rw_readout(o_c, rp_c, lp)
            b_c = _ssd_readout(y_c, xbc_c, zc[1], lp)
            c_c = _da_attn(q_c, [k_c], [v_c], lp['da_lam'], lp['da_g'], lp['da_lam_init'])
            d_c = _ctx_attn(zc[3])
            merged_c = _gated_merge(hc.reshape(bsz * lctx, d),
                                    [t.reshape(bsz * lctx, BR_W) for t in (a_c, b_c, c_c, d_c)],
                                    lp['w_gate'], lp['gate_b'], lp['w_br']).reshape(bsz, lctx, d)
            xc = _matmul_residual(merged_c, lp['w_out'], xc, mc[2])
            hc2 = _norm_mod(xc, norm2_g[i], mc[3], mc[4])
            xc = _ffn(xc, hc2, lp, mc[5])
    return _final_norm(xl, final_norm_g)
```

```python
import functools
import math

import numpy as np
import jax
import jax.numpy as jnp
from jax import lax
from jax.experimental import pallas as pl
from jax.experimental.pallas import tpu as pltpu

F32 = jnp.float32
BF16 = jnp.bfloat16

GRID_W = 64
NORM_EPS = 1e-6
HD = 64
BR_W = 512
RW_HEADS = 8
RW_RANK = 64
RW_GATE_RANK = 128
RW_GN_EPS = 64e-5
RW_IN = 3 * BR_W + 4 * RW_RANK + RW_GATE_RANK
RW_CHUNK = 64
RW_NARR = 11
RW_GROUP = 4
RW_GW = RW_GROUP * HD
SSD_HEADS = 8
SSD_GROUPS = 2
SSD_STATE = 128
SSD_XBC = BR_W + 2 * SSD_GROUPS * SSD_STATE
SSD_IN = BR_W + SSD_XBC + 2 * SSD_HEADS
SSD_IN_PAD = BR_W + SSD_XBC + 128
SSD_CHUNK = 128
DA_HEADS = 4
DA_IN = 3 * BR_W
DA_SUBLN_EPS = 1e-5
NA_HEADS = 8
NA_IN = 3 * BR_W
NA_WIN_R = 8
NA_WIN_C = 16
NA_QROWS = 4
NA_KROWS = 12
ROPE_BASE = 10000.0
NEG = -1e30
LANES = 128
VMEM_LIMIT = 56 * 1024 * 1024


def _cp(*sem):
    return pltpu.CompilerParams(dimension_semantics=sem, vmem_limit_bytes=VMEM_LIMIT)


def _tile(n, pref, mult):
    t = min(n, pref)
    t -= t % mult
    while t >= mult:
        if n % t == 0:
            return t
        t -= mult
    return n


def _dot(a, b):
    return jnp.dot(a.astype(BF16), b.astype(BF16), preferred_element_type=F32)


def _dot_nt(a, b):
    return lax.dot_general(a.astype(BF16), b.astype(BF16), (((1,), (1,)), ((), ())),
                           preferred_element_type=F32)


def _dot_tn(a, b):
    return lax.dot_general(a.astype(BF16), b.astype(BF16), (((0,), (0,)), ((), ())),
                           preferred_element_type=F32)


def _split3(x):
    hi = x.astype(BF16)
    r1 = x - hi.astype(F32)
    mid = r1.astype(BF16)
    lo = (r1 - mid.astype(F32)).astype(BF16)
    return hi, mid, lo


def _dot01(m01, x):
    hi, mid, lo = _split3(x)
    return (jnp.dot(m01, hi, preferred_element_type=F32)
            + jnp.dot(m01, mid, preferred_element_type=F32)
            + jnp.dot(m01, lo, preferred_element_type=F32))


def _x_dot01(x, m01):
    hi, mid, lo = _split3(x)
    return (jnp.dot(hi, m01, preferred_element_type=F32)
            + jnp.dot(mid, m01, preferred_element_type=F32)
            + jnp.dot(lo, m01, preferred_element_type=F32))


def _sigmoid(x):
    return 1.0 / (1.0 + jnp.exp(-x))


def _silu(x):
    return x * _sigmoid(x)


def _softplus(x):
    return jnp.maximum(x, 0.0) + jnp.log(1.0 + jnp.exp(-jnp.abs(x)))


def _mod_kernel(c_ref, w_ref, b_ref, o_ref):
    a = _silu(c_ref[...])
    o_ref[...] = _dot(a, w_ref[...]) + b_ref[...]


def _modulation(cc, w, b):
    m, d = cc.shape
    n = w.shape[1]
    tn = _tile(n, 1024, LANES)
    return pl.pallas_call(
        _mod_kernel,
        grid=(n // tn,),
        in_specs=[pl.BlockSpec((m, d), lambda j: (0, 0)),
                  pl.BlockSpec((d, tn), lambda j: (0, j)),
                  pl.BlockSpec((1, tn), lambda j: (0, j))],
        out_specs=pl.BlockSpec((m, tn), lambda j: (0, j)),
        out_shape=jax.ShapeDtypeStruct((m, n), F32),
        compiler_params=_cp("parallel"),
        name="adaln_mod",
    )(cc, w, b.reshape(1, n))


def _norm_mod_kernel(x_ref, g_ref, sh_ref, sc_ref, o_ref):
    x = x_ref[0]
    y = x * lax.rsqrt(jnp.mean(x * x, axis=-1, keepdims=True) + NORM_EPS) * g_ref[...]
    o_ref[0] = (y * (1.0 + sc_ref[0]) + sh_ref[0]).astype(o_ref.dtype)


def _norm_mod(x, g, shift, scale):
    bsz, length, d = x.shape
    tr = _tile(length, 256, 16)
    bm = shift.shape[0]
    mod_map = (lambda b, j: (b, 0, 0)) if bm == bsz else (lambda b, j: (0, 0, 0))
    return pl.pallas_call(
        _norm_mod_kernel,
        grid=(bsz, length // tr),
        in_specs=[pl.BlockSpec((1, tr, d), lambda b, j: (b, j, 0)),
                  pl.BlockSpec((1, d), lambda b, j: (0, 0)),
                  pl.BlockSpec((1, 1, d), mod_map),
                  pl.BlockSpec((1, 1, d), mod_map)],
        out_specs=pl.BlockSpec((1, tr, d), lambda b, j: (b, j, 0)),
        out_shape=jax.ShapeDtypeStruct((bsz, length, d), BF16),
        compiler_params=_cp("parallel", "parallel"),
        name="norm_mod",
    )(x, g.reshape(1, d), shift, scale)


def _final_norm_kernel(x_ref, g_ref, o_ref):
    x = x_ref[0]
    o_ref[0] = x * lax.rsqrt(jnp.mean(x * x, axis=-1, keepdims=True) + NORM_EPS) * g_ref[...]


def _final_norm(x, g):
    bsz, length, d = x.shape
    tr = _tile(length, 256, 8)
    return pl.pallas_call(
        _final_norm_kernel,
        grid=(bsz, length // tr),
        in_specs=[pl.BlockSpec((1, tr, d), lambda b, j: (b, j, 0)),
                  pl.BlockSpec((1, d), lambda b, j: (0, 0))],
        out_specs=pl.BlockSpec((1, tr, d), lambda b, j: (b, j, 0)),
        out_shape=jax.ShapeDtypeStruct((bsz, length, d), F32),
        compiler_params=_cp("parallel", "parallel"),
        name="final_norm",
    )(x, g.reshape(1, d))


def _mm_kernel(a_ref, w_ref, o_ref):
    o_ref[...] = jnp.dot(a_ref[...], w_ref[...], preferred_element_type=F32).astype(o_ref.dtype)


def _matmul(a, w, out_dtype, tm_pref=512, tn_pref=2048):
    m, k = a.shape
    n = w.shape[1]
    tm = _tile(m, tm_pref, 16)
    tn = _tile(n, tn_pref, LANES)
    return pl.pallas_call(
        _mm_kernel,
        grid=(m // tm, n // tn),
        in_specs=[pl.BlockSpec((tm, k), lambda i, j: (i, 0)),
                  pl.BlockSpec((k, tn), lambda i, j: (0, j))],
        out_specs=pl.BlockSpec((tm, tn), lambda i, j: (i, j)),
        out_shape=jax.ShapeDtypeStruct((m, n), out_dtype),
        compiler_params=_cp("parallel", "parallel"),
        name="matmul",
    )(a, w)


def _mm_res_kernel(a_ref, w_ref, x_ref, g_ref, o_ref):
    y = jnp.dot(a_ref[0], w_ref[...], preferred_element_type=F32)
    o_ref[0] = x_ref[0] + g_ref[0] * y


def _matmul_residual(a, w, x, gate):
    bsz, length, k = a.shape
    n = w.shape[1]
    tm = _tile(length, 512, 16)
    tn = _tile(n, 512, LANES)
    bm = gate.shape[0]
    gmap = (lambda b, i, j: (b, 0, j)) if bm == bsz else (lambda b, i, j: (0, 0, j))
    return pl.pallas_call(
        _mm_res_kernel,
        grid=(bsz, length // tm, n // tn),
        in_specs=[pl.BlockSpec((1, tm, k), lambda b, i, j: (b, i, 0)),
                  pl.BlockSpec((k, tn), lambda b, i, j: (0, j)),
                  pl.BlockSpec((1, tm, tn), lambda b, i, j: (b, i, j)),
                  pl.BlockSpec((1, 1, tn), gmap)],
        out_specs=pl.BlockSpec((1, tm, tn), lambda b, i, j: (b, i, j)),
        out_shape=jax.ShapeDtypeStruct((bsz, length, n), F32),
        compiler_params=_cp("parallel", "parallel", "parallel"),
        name="matmul_residual",
    )(a, w, x, gate)


def _merge_kernel(h_ref, o0_ref, o1_ref, o2_ref, o3_ref, wg_ref, gb_ref, wbr_ref, out_ref):
    h = h_ref[...]
    acc = None
    for n, o_ref in enumerate((o0_ref, o1_ref, o2_ref, o3_ref)):
        gate = _sigmoid(jnp.dot(h, wg_ref[n], preferred_element_type=F32) + gb_ref[n])
        term = gate * jnp.dot(o_ref[...], wbr_ref[n], preferred_element_type=F32)
        acc = term if acc is None else acc + term
    out_ref[...] = acc.astype(out_ref.dtype)


def _gated_merge(h, branches, wg, gb, wbr):
    m, d = h.shape
    nb = len(branches)
    tm = _tile(m, 1024, 16)
    tn = _tile(d, 256, LANES)
    bspec = pl.BlockSpec((tm, BR_W), lambda i, j: (i, 0))
    return pl.pallas_call(
        _merge_kernel,
        grid=(m // tm, d // tn),
        in_specs=[pl.BlockSpec((tm, d), lambda i, j: (i, 0)), bspec, bspec, bspec, bspec,
                  pl.BlockSpec((nb, d, tn), lambda i, j: (0, 0, j)),
                  pl.BlockSpec((nb, 1, tn), lambda i, j: (0, 0, j)),
                  pl.BlockSpec((nb, BR_W, tn), lambda i, j: (0, 0, j))],
        out_specs=pl.BlockSpec((tm, tn), lambda i, j: (i, j)),
        out_shape=jax.ShapeDtypeStruct((m, d), BF16),
        compiler_params=_cp("parallel", "parallel"),
        name="gated_merge",
    )(h, *branches, wg, gb, wbr)


def _conv3(x, prev_row, next_row, w_ref, b_ref):
    rows = x.shape[0]
    rid = lax.broadcasted_iota(jnp.int32, x.shape, 0)
    prev = jnp.where(rid == 0, prev_row, pltpu.roll(x, 1, 0))
    nxt = jnp.where(rid == rows - 1, next_row, pltpu.roll(x, rows - 1, 0))
    return b_ref[...] + prev * w_ref[0:1, :] + x * w_ref[1:2, :] + nxt * w_ref[2:3, :]


def _ffn_up_kernel(a_ref, ap_ref, an_ref, wg_ref, wv_ref, cwg_ref, cbg_ref, cwv_ref, cbv_ref,
                   o_ref, *, nt, halo):
    i = pl.program_id(1)
    tm = a_ref.shape[1]
    a_ext = jnp.concatenate([ap_ref[0], a_ref[0], an_ref[0]], axis=0)
    n_ext = tm + 2 * halo
    rid = lax.broadcasted_iota(jnp.int32, (tm, 1), 0)
    no_prev = jnp.logical_and(i == 0, rid == 0)
    no_next = jnp.logical_and(i == nt - 1, rid == tm - 1)

    def branch(w_ref, cw_ref, cb_ref):
        u = jnp.dot(a_ext, w_ref[...], preferred_element_type=F32)
        um1 = jnp.where(no_prev, 0.0, pltpu.roll(u, 1, 0)[halo:halo + tm])
        up1 = jnp.where(no_next, 0.0, pltpu.roll(u, n_ext - 1, 0)[halo:halo + tm])
        return (cb_ref[...] + um1 * cw_ref[0:1, :] + u[halo:halo + tm] * cw_ref[1:2, :]
                + up1 * cw_ref[2:3, :])

    gate = branch(wg_ref, cwg_ref, cbg_ref)
    val = branch(wv_ref, cwv_ref, cbv_ref)
    o_ref[0] = (_silu(gate) * val).astype(o_ref.dtype)


def _ffn_up(h, w_up, conv_w, conv_b):
    bsz, length, d = h.shape
    f2 = w_up.shape[1]
    f = f2 // 2
    halo = 16
    tm = _tile(length, 1024, halo)
    tc = _tile(f, 512, LANES)
    nt = length // tm
    nc = f // tc
    nh = length // halo
    rb = tm // halo

    def wspec(rows, off):
        return pl.BlockSpec((rows, tc), lambda b, i, c: (0, c + off))

    cb = conv_b.reshape(1, f2)
    return pl.pallas_call(
        functools.partial(_ffn_up_kernel, nt=nt, halo=halo),
        grid=(bsz, nt, nc),
        in_specs=[pl.BlockSpec((1, tm, d), lambda b, i, c: (b, i, 0)),
                  pl.BlockSpec((1, halo, d), lambda b, i, c: (b, jnp.maximum(i * rb - 1, 0), 0)),
                  pl.BlockSpec((1, halo, d), lambda b, i, c: (b, jnp.minimum((i + 1) * rb, nh - 1), 0)),
                  wspec(d, 0), wspec(d, nc), wspec(3, 0), wspec(1, 0), wspec(3, nc), wspec(1, nc)],
        out_specs=pl.BlockSpec((1, tm, tc), lambda b, i, c: (b, i, c)),
        out_shape=jax.ShapeDtypeStruct((bsz, length, f), BF16),
        compiler_params=_cp("parallel", "parallel", "parallel"),
        name="ffn_up",
    )(h, h, h, w_up, w_up, conv_w, cb, conv_w, cb)


def _rw_prep_kernel(z_ref, zp_ref, zn_ref, mu_ref, w0_ref, wup_ref, a0_ref, aup_ref, gup_ref,
                    kk_ref, ka_ref, rk_ref, ones_ref, o_ref, *, nt):
    j = pl.program_id(1)
    p = z_ref[0]
    rows = p.shape[0]
    rid = lax.broadcasted_iota(jnp.int32, p.shape, 0)
    prow = jnp.where(j > 0, zp_ref[0, 7:8, :], 0.0)
    nrow = jnp.where(j < nt - 1, zn_ref[0, 0:1, :], 0.0)
    prev = jnp.where(rid == 0, prow, pltpu.roll(p, 1, 0))
    nxt = jnp.where(rid == rows - 1, nrow, pltpu.roll(p, rows - 1, 0))
    ps = p + mu_ref[0:1, :] * (prev - p) + mu_ref[1:2, :] * (nxt - p)

    w = BR_W
    r = ps[:, 0:w]
    k = ps[:, w:2 * w]
    v = ps[:, 2 * w:3 * w]
    wd = ps[:, 3 * w:3 * w + 128]
    ad = ps[:, 3 * w + 128:3 * w + 256]
    gd = ps[:, 3 * w + 256:3 * w + 384]

    wraw = _dot(jnp.tanh(wd), wup_ref[...])
    araw = _dot(ad, aup_ref[...])
    g = _dot(_sigmoid(gd), gup_ref[...])
    ones = ones_ref[...]
    kkv = k * kk_ref[...]
    ss = _x_dot01(kkv * kkv, ones)
    kkn = kkv / jnp.maximum(jnp.sqrt(ss), 1e-12)

    def put(i, x):
        o_ref[0, :, i * w:(i + 1) * w] = x

    put(0, r)
    put(1, v)
    put(2, kkn)
    ksum = None
    for d in range(2):
        lw = -math.exp(-0.5) * _sigmoid(w0_ref[d:d + 1, :] + wraw[:, d * w:(d + 1) * w])
        a = _sigmoid(a0_ref[d:d + 1, :] + araw[:, d * w:(d + 1) * w])
        kd = k * (1.0 + (a - 1.0) * ka_ref[...])
        put(3 + 3 * d, lw)
        put(4 + 3 * d, kd)
        put(5 + 3 * d, kkn * a)
        ksum = kd if ksum is None else ksum + kd
    put(9, g)
    put(10, _x_dot01(r * rk_ref[...] * ksum, ones) * v)


def _rw_prep(z, lp):
    bsz, length, fin = z.shape
    tr = _tile(length, 256, 8)
    nt = length // tr
    rb = tr // 8
    nh = length // 8
    full = lambda a: pl.BlockSpec(a.shape, lambda b, j: (0,) * a.ndim)
    params = (lp['rw_mu'], lp['rw_w0'], lp['rw_wup'], lp['rw_a0'], lp['rw_aup'], lp['rw_gup'],
              lp['rw_kk'], lp['rw_ka'], lp['rw_rk'], lp['ones_hd'])
    return pl.pallas_call(
        functools.partial(_rw_prep_kernel, nt=nt),
        grid=(bsz, nt),
        in_specs=[pl.BlockSpec((1, tr, fin), lambda b, j: (b, j, 0)),
                  pl.BlockSpec((1, 8, fin), lambda b, j: (b, jnp.maximum(j * rb - 1, 0), 0)),
                  pl.BlockSpec((1, 8, fin), lambda b, j: (b, jnp.minimum((j + 1) * rb, nh - 1), 0))]
                 + [full(a) for a in params],
        out_specs=pl.BlockSpec((1, tr, RW_NARR * BR_W), lambda b, j: (b, j, 0)),
        out_shape=jax.ShapeDtypeStruct((bsz, length, RW_NARR * BR_W), F32),
        compiler_params=_cp("parallel", "parallel"),
        name="rwkv_prep",
    )(z, z, z, *params)


def _rw_scan_kernel(r_ref, v_ref, kk_ref, lw_ref, k_ref, b_ref, s0_ref, o_ref, sfin_ref, s_scr,
                    *, nc):
    d = pl.program_id(1)
    c = pl.program_id(2)

    @pl.when(c == 0)
    def _():
        s_scr[...] = s0_ref[0, 0]

    cs = RW_CHUNK
    gw = RW_GW
    ri = lax.broadcasted_iota(jnp.int32, (gw, gw), 0)
    ci = lax.broadcasted_iota(jnp.int32, (gw, gw), 1)
    sgn = 1 - 2 * d
    rel = ((ci & (cs - 1)) - (ri & (cs - 1))) * sgn
    strict = rel < 0
    incl = rel <= 0
    head_mask = (ri >> 6) == (ci >> 6)
    eye = jnp.where(ri == ci, 1.0, 0.0)
    same = {s: (ri >> int(math.log2(s))) == (ci >> int(math.log2(s))) for s in (8, 16, 32)}

    ti = lax.broadcasted_iota(jnp.int32, (cs, cs), 0)
    ii = lax.broadcasted_iota(jnp.int32, (cs, cs), 1)
    tri = jnp.where((ii - ti) * sgn <= 0, 1.0, 0.0).astype(BF16)

    lw = lw_ref[0]
    cum = _dot01(tri, lw)
    tot = jnp.sum(lw, axis=0, keepdims=True)
    qt = kk_ref[0] * jnp.exp(cum - lw)
    rt = r_ref[0] * jnp.exp(cum)
    e_neg = jnp.exp(-cum)
    e_end = jnp.exp(tot - cum)
    kt = k_ref[0] * e_neg
    bt = b_ref[0] * e_neg
    kh = k_ref[0] * e_end
    bh = b_ref[0] * e_end
    p_tot = jnp.exp(tot)
    v = v_ref[0]

    def expand(x, sl):
        xs = x[:, sl]
        return jnp.where(head_mask, jnp.concatenate([xs] * RW_GROUP, axis=0), 0.0).astype(BF16)

    for g in range(RW_HEADS // RW_GROUP):
        sl = slice(g * gw, (g + 1) * gw)
        s0 = s_scr[g]
        s0b = s0.astype(BF16)
        qe, re, ve = expand(qt, sl), expand(rt, sl), expand(v, sl)
        kte, bte = expand(kt, sl), expand(bt, sl)
        a_qk = jnp.where(strict, _dot_nt(qe, kte), 0.0)
        lmat = jnp.where(strict, _dot_nt(qe, bte), 0.0)
        a_rk = jnp.where(incl, _dot_nt(re, kte), 0.0)
        a_rb = jnp.where(incl, _dot_nt(re, bte), 0.0)
        ld = jnp.where(same[8], lmat, 0.0)
        ld2 = _dot(ld, ld)
        ld4 = _dot(ld2, ld2)
        x = _dot(_dot(eye - ld, eye + ld2), eye + ld4)
        for s in (8, 16, 32):
            off_mask = jnp.logical_not(same[s])
            if 2 * s < cs:
                off_mask = jnp.logical_and(same[2 * s], off_mask)
            x = x - _dot(_dot(x, jnp.where(off_mask, lmat, 0.0)), x)
        sa = _dot(x, _dot_nt(qe, s0b) + _dot(a_qk, ve))
        o = _dot_nt(re, s0b) + _dot(a_rk, ve) - _dot(a_rb, sa)
        o_tm = o[0:cs]
        for h in range(1, RW_GROUP):
            o_tm = o_tm + o[h * cs:(h + 1) * cs]
        o_ref[0, 0, :, sl] = o_tm
        s_scr[g] = s0 * p_tot[:, sl] + _dot_tn(ve, expand(kh, sl)) - _dot_tn(sa, expand(bh, sl))

    @pl.when(c == nc - 1)
    def _():
        sfin_ref[0, 0] = s_scr[...]


def _rw_scan(rwp, s_init):
    bsz, length, _ = rwp.shape
    cs = RW_CHUNK
    nc = length // cs
    ng = RW_HEADS // RW_GROUP

    def cidx(d, c):
        return c + d * (nc - 1 - 2 * c)

    def arr(i):
        return pl.BlockSpec((1, cs, BR_W), lambda b, d, c: (b, cidx(d, c), i))

    def darr(i):
        return pl.BlockSpec((1, cs, BR_W), lambda b, d, c: (b, cidx(d, c), 3 + 3 * d + i))

    sspec = pl.BlockSpec((1, 1, ng, RW_GW, RW_GW), lambda b, d, c: (b, d, 0, 0, 0))
    return pl.pallas_call(
        functools.partial(_rw_scan_kernel, nc=nc),
        grid=(bsz, 2, nc),
        in_specs=[arr(0), arr(1), arr(2), darr(0), darr(1), darr(2), sspec],
        out_specs=[pl.BlockSpec((1, 1, cs, BR_W), lambda b, d, c: (b, d, cidx(d, c), 0)), sspec],
        out_shape=[jax.ShapeDtypeStruct((bsz, 2, length, BR_W), F32),
                   jax.ShapeDtypeStruct((bsz, 2, ng, RW_GW, RW_GW), F32)],
        scratch_shapes=[pltpu.VMEM((ng, RW_GW, RW_GW), F32)],
        compiler_params=_cp("parallel", "parallel", "arbitrary"),
        name="rwkv_scan",
    )(rwp, rwp, rwp, rwp, rwp, rwp, s_init)


def _rw_readout_kernel(o_ref, g_ref, bonus_ref, lng_ref, lnb_ref, mean_ref, out_ref):
    o = o_ref[0, 0] + o_ref[0, 1]
    mean_m = mean_ref[...]
    mu = _x_dot01(o, mean_m)
    dlt = o - mu
    var = _x_dot01(dlt * dlt, mean_m)
    on = dlt * lax.rsqrt(var + RW_GN_EPS) * lng_ref[...] + lnb_ref[...]
    out_ref[0] = ((on + bonus_ref[0]) * g_ref[0]).astype(out_ref.dtype)


def _rw_readout(o, rwp, lp):
    bsz, _, length, w = o.shape
    tr = _tile(length, 256, 16)
    return pl.pallas_call(
        _rw_readout_kernel,
        grid=(bsz, length // tr),
        in_specs=[pl.BlockSpec((1, 2, tr, w), lambda b, j: (b, 0, j, 0)),
                  pl.BlockSpec((1, tr, w), lambda b, j: (b, j, 9)),
                  pl.BlockSpec((1, tr, w), lambda b, j: (b, j, 10)),
                  pl.BlockSpec((1, w), lambda b, j: (0, 0)),
                  pl.BlockSpec((1, w), lambda b, j: (0, 0)),
                  pl.BlockSpec((w, w), lambda b, j: (0, 0))],
        out_specs=pl.BlockSpec((1, tr, w), lambda b, j: (b, j, 0)),
        out_shape=jax.ShapeDtypeStruct((bsz, length, w), BF16),
        compiler_params=_cp("parallel", "parallel"),
        name="rwkv_readout",
    )(o, rwp, rwp, lp['rw_lng'], lp['rw_lnb'], lp['mean_hd'])


def _ssd_prep_kernel(z_ref, zp_ref, zn_ref, cw_ref, cb_ref, dtb_ref, xbc_ref, dt_ref, *, nt):
    j = pl.program_id(1)
    x = z_ref[0, :, BR_W:BR_W + SSD_XBC]
    prow = jnp.where(j > 0, zp_ref[0, 7:8, BR_W:BR_W + SSD_XBC], 0.0)
    nrow = jnp.where(j < nt - 1, zn_ref[0, 0:1, BR_W:BR_W + SSD_XBC], 0.0)
    xbc_ref[0] = _silu(_conv3(x, prow, nrow, cw_ref, cb_ref))
    dt_raw = z_ref[0, :, BR_W + SSD_XBC:BR_W + SSD_XBC + LANES]
    dt_ref[0, 0] = _softplus(dt_raw + dtb_ref[0:1, :])
    dt_ref[0, 1] = _softplus(pltpu.roll(dt_raw, LANES - SSD_HEADS, 1) + dtb_ref[1:2, :])


def _ssd_prep(z, lp):
    bsz, length, fin = z.shape
    tr = _tile(length, 256, 8)
    nt = length // tr
    rb = tr // 8
    nh = length // 8
    return pl.pallas_call(
        functools.partial(_ssd_prep_kernel, nt=nt),
        grid=(bsz, nt),
        in_specs=[pl.BlockSpec((1, tr, fin), lambda b, j: (b, j, 0)),
                  pl.BlockSpec((1, 8, fin), lambda b, j: (b, jnp.maximum(j * rb - 1, 0), 0)),
                  pl.BlockSpec((1, 8, fin), lambda b, j: (b, jnp.minimum((j + 1) * rb, nh - 1), 0)),
                  pl.BlockSpec((3, SSD_XBC), lambda b, j: (0, 0)),
                  pl.BlockSpec((1, SSD_XBC), lambda b, j: (0, 0)),
                  pl.BlockSpec((2, LANES), lambda b, j: (0, 0))],
        out_specs=[pl.BlockSpec((1, tr, SSD_XBC), lambda b, j: (b, j, 0)),
                   pl.BlockSpec((1, 2, tr, LANES), lambda b, j: (b, 0, j, 0))],
        out_shape=[jax.ShapeDtypeStruct((bsz, length, SSD_XBC), F32),
                   jax.ShapeDtypeStruct((bsz, 2, length, LANES), F32)],
        compiler_params=_cp("parallel", "parallel"),
        name="ssd_prep",
    )(z, z, z, lp['ssd_cw'], lp['ssd_cb'], lp['ssd_dtb'])


def _ssd_scan_kernel(xbc_ref, dt_ref, a_ref, h0_ref, y_ref, hfin_ref, h_scr, *, nc):
    d = pl.program_id(1)
    c = pl.program_id(2)

    @pl.when(c == 0)
    def _():
        h_scr[...] = h0_ref[0, 0]

    q = SSD_CHUNK
    ti = lax.broadcasted_iota(jnp.int32, (q, q), 0)
    ii = lax.broadcasted_iota(jnp.int32, (q, q), 1)
    incl = (ii - ti) * (1 - 2 * d) <= 0
    tri = jnp.where(incl, 1.0, 0.0).astype(BF16)

    dt = dt_ref[0, 0]
    dta = dt * a_ref[0]
    acs = _dot01(tri, dta)
    acs_t = acs.T
    tot = jnp.sum(dta, axis=0, keepdims=True)
    e_acs = jnp.exp(acs)
    e_end = jnp.exp(tot - acs)
    e_tot = jnp.exp(tot)
    xbc = xbc_ref[0]
    n = SSD_STATE
    rep = SSD_HEADS // SSD_GROUPS
    for g in range(SSD_GROUPS):
        bg = xbc[:, BR_W + g * n:BR_W + (g + 1) * n]
        cg = xbc[:, BR_W + SSD_GROUPS * n + g * n:BR_W + SSD_GROUPS * n + (g + 1) * n]
        cb = _dot_nt(cg, bg)
        for hh in range(rep):
            h = g * rep + hh
            a_col = acs[:, h:h + 1]
            dec = jnp.exp(jnp.where(incl, a_col - acs_t[h:h + 1, :], NEG))
            xdt = xbc[:, h * HD:(h + 1) * HD] * dt[:, h:h + 1]
            ht = h_scr[h]
            y = _dot(cb * dec, xdt) + e_acs[:, h:h + 1] * _dot(cg, ht)
            y_ref[0, 0, :, h * HD:(h + 1) * HD] = y
            h_scr[h] = ht * e_tot[:, h:h + 1] + _dot_tn(bg, xdt * e_end[:, h:h + 1])

    @pl.when(c == nc - 1)
    def _():
        hfin_ref[0, 0] = h_scr[...]


def _ssd_scan(xbc, dt, a_neg, h_init):
    bsz, length, _ = xbc.shape
    q = SSD_CHUNK
    nc = length // q

    def cidx(d, c):
        return c + d * (nc - 1 - 2 * c)

    hspec = pl.BlockSpec((1, 1, SSD_HEADS, SSD_STATE, HD), lambda b, d, c: (b, d, 0, 0, 0))
    return pl.pallas_call(
        functools.partial(_ssd_scan_kernel, nc=nc),
        grid=(bsz, 2, nc),
        in_specs=[pl.BlockSpec((1, q, SSD_XBC), lambda b, d, c: (b, cidx(d, c), 0)),
                  pl.BlockSpec((1, 1, q, LANES), lambda b, d, c: (b, d, cidx(d, c), 0)),
                  pl.BlockSpec((1, 1, LANES), lambda b, d, c: (d, 0, 0)),
                  hspec],
        out_specs=[pl.BlockSpec((1, 1, q, BR_W), lambda b, d, c: (b, d, cidx(d, c), 0)), hspec],
        out_shape=[jax.ShapeDtypeStruct((bsz, 2, length, BR_W), F32),
                   jax.ShapeDtypeStruct((bsz, 2, SSD_HEADS, SSD_STATE, HD), F32)],
        scratch_shapes=[pltpu.VMEM((SSD_HEADS, SSD_STATE, HD), F32)],
        compiler_params=_cp("parallel", "parallel", "arbitrary"),
        name="ssd_scan",
    )(xbc, dt, a_neg, h_init)


def _ssd_readout_kernel(y_ref, xs_ref, z_ref, dsk_ref, ng_ref, out_ref):
    y = y_ref[0, 0] + y_ref[0, 1] + dsk_ref[...] * xs_ref[0]
    y = y * _silu(z_ref[0])
    y = y * lax.rsqrt(jnp.mean(y * y, axis=-1, keepdims=True) + NORM_EPS) * ng_ref[...]
    out_ref[0] = y.astype(out_ref.dtype)


def _ssd_readout(y, xbc, z, lp):
    bsz, _, length, w = y.shape
    tr = _tile(length, 256, 16)
    return pl.pallas_call(
        _ssd_readout_kernel,
        grid=(bsz, length // tr),
        in_specs=[pl.BlockSpec((1, 2, tr, w), lambda b, j: (b, 0, j, 0)),
                  pl.BlockSpec((1, tr, w), lambda b, j: (b, j, 0)),
                  pl.BlockSpec((1, tr, w), lambda b, j: (b, j, 0)),
                  pl.BlockSpec((1, w), lambda b, j: (0, 0)),
                  pl.BlockSpec((1, w), lambda b, j: (0, 0))],
        out_specs=pl.BlockSpec((1, tr, w), lambda b, j: (b, j, 0)),
        out_shape=jax.ShapeDtypeStruct((bsz, length, w), BF16),
        compiler_params=_cp("parallel", "parallel"),
        name="ssd_readout",
    )(y, xbc, z, lp['ssd_dskip'], lp['ssd_ng'])


def _da_prep_kernel(z_ref, cos_ref, sin_ref, q_ref, k_ref, v_ref, *, rope):
    w = BR_W

    def rot(x):
        if not rope:
            return x
        lane = lax.broadcasted_iota(jnp.int32, (1, LANES), 1)
        first = (lane & 31) < 16
        outs = []
        for cb in range(w // LANES):
            xb = x[:, cb * LANES:(cb + 1) * LANES]
            sw = jnp.where(first, pltpu.roll(xb, LANES - 16, 1), pltpu.roll(xb, 16, 1))
            outs.append(xb * cos_ref[:, cb * LANES:(cb + 1) * LANES]
                        + sw * sin_ref[:, cb * LANES:(cb + 1) * LANES])
        return jnp.concatenate(outs, axis=1)

    q_ref[0] = (rot(z_ref[0, :, 0:w]) * (HD ** -0.5)).astype(q_ref.dtype)
    k_ref[0] = rot(z_ref[0, :, w:2 * w]).astype(k_ref.dtype)
    v_ref[0] = z_ref[0, :, 2 * w:3 * w].astype(v_ref.dtype)


def _da_prep(z, cos, sin, rope):
    bsz, length, fin = z.shape
    tr = _tile(length, 256, 16)
    w = BR_W
    ospec = pl.BlockSpec((1, tr, w), lambda b, j: (b, j, 0))
    oshape = jax.ShapeDtypeStruct((bsz, length, w), BF16)
    return pl.pallas_call(
        functools.partial(_da_prep_kernel, rope=rope),
        grid=(bsz, length // tr),
        in_specs=[pl.BlockSpec((1, tr, fin), lambda b, j: (b, j, 0)),
                  pl.BlockSpec((tr, w), lambda b, j: (j, 0)),
                  pl.BlockSpec((tr, w), lambda b, j: (j, 0))],
        out_specs=[ospec, ospec, ospec],
        out_shape=[oshape, oshape, oshape],
        compiler_params=_cp("parallel", "parallel"),
        name="da_prep",
    )(z, cos, sin)


def _da_attn_kernel(*refs, nsrc, lam_init):
    lam_ref, q_ref = refs[0], refs[1]
    kv = refs[2:2 + 2 * nsrc]
    g_ref, o_ref = refs[2 + 2 * nsrc], refs[3 + 2 * nsrc]
    q = q_ref[0]
    lane = lax.broadcasted_iota(jnp.int32, (1, LANES), 1)
    zero = jnp.zeros_like(q)
    probs = []
    for m in range(2):
        qm = jnp.where((lane < HD) if m == 0 else (lane >= HD), q, zero)
        ss = [_dot_nt(qm, kv[2 * i][0]) for i in range(nsrc)]
        mx = functools.reduce(jnp.maximum, [jnp.max(s, axis=-1, keepdims=True) for s in ss])
        es = [jnp.exp(s - mx) for s in ss]
        den = functools.reduce(lambda a, b: a + b, [jnp.sum(e, axis=-1, keepdims=True) for e in es])
        inv = 1.0 / den
        probs.append([e * inv for e in es])
    lam = lam_ref[0:1, 0:1]
    o = None
    for i in range(nsrc):
        a = probs[0][i] - lam * probs[1][i]
        t = _dot(a, kv[2 * i + 1][0])
        o = t if o is None else o + t
    y = o * lax.rsqrt(jnp.mean(o * o, axis=-1, keepdims=True) + DA_SUBLN_EPS) * g_ref[...]
    o_ref[0] = (y * (1.0 - lam_init)).astype(o_ref.dtype)


def _da_attn(q, ks, vs, lam, subln_g, lam_init):
    bsz, lq, _ = q.shape
    tq = _tile(lq, 256, 16)
    nsrc = len(ks)
    in_specs = [pl.BlockSpec((1, LANES), lambda b, h, j: (0, 0)),
                pl.BlockSpec((1, tq, LANES), lambda b, h, j: (b, j, h))]
    args = [lam, q]
    for k, v in zip(ks, vs):
        lk = k.shape[1]
        in_specs.append(pl.BlockSpec((1, lk, LANES), lambda b, h, j: (b, 0, h)))
        in_specs.append(pl.BlockSpec((1, lk, LANES), lambda b, h, j: (b, 0, h)))
        args += [k, v]
    in_specs.append(pl.BlockSpec((1, LANES), lambda b, h, j: (0, 0)))
    args.append(subln_g)
    return pl.pallas_call(
        functools.partial(_da_attn_kernel, nsrc=nsrc, lam_init=lam_init),
        grid=(bsz, DA_HEADS, lq // tq),
        in_specs=in_specs,
        out_specs=pl.BlockSpec((1, tq, LANES), lambda b, h, j: (b, j, h)),
        out_shape=jax.ShapeDtypeStruct((bsz, lq, BR_W), BF16),
        compiler_params=_cp("parallel", "parallel", "parallel"),
        name="diff_attn",
    )(*args)


def _na_kb(i, rows):
    return jnp.clip(i * NA_QROWS - NA_WIN_R // 2, 0, rows - NA_KROWS)


def _na_attn_kernel(q_ref, k_ref, v_ref, kc_ref, vc_ref, bias_ref, o_ref, *, rows):
    i = pl.program_id(2)
    start = pl.multiple_of(_na_kb(i, rows) * GRID_W, GRID_W)
    nk = NA_KROWS * GRID_W
    q = q_ref[0] * (HD ** -0.5)
    kw = k_ref[0, pl.ds(start, nk), :]
    vw = v_ref[0, pl.ds(start, nk), :]
    kc = kc_ref[0]
    vc = vc_ref[0]
    lane = lax.broadcasted_iota(jnp.int32, (1, LANES), 1)
    outs = []
    for h in range(2):
        qm = jnp.where((lane < HD) if h == 0 else (lane >= HD), q, 0.0)
        sw = _dot_nt(qm, kw) + bias_ref[h, 0]
        sc = _dot_nt(qm, kc)
        mx = jnp.maximum(jnp.max(sw, axis=-1, keepdims=True), jnp.max(sc, axis=-1, keepdims=True))
        ew = jnp.exp(sw - mx)
        ec = jnp.exp(sc - mx)
        inv = 1.0 / (jnp.sum(ew, axis=-1, keepdims=True) + jnp.sum(ec, axis=-1, keepdims=True))
        outs.append(_dot(ew * inv, vw) + _dot(ec * inv, vc))
    o_ref[0] = jnp.where(lane < HD, outs[0], outs[1]).astype(o_ref.dtype)


def _na_attn(z_l, z_c, bias):
    bsz, length, _ = z_l.shape
    lc = z_c.shape[1]
    rows = length // GRID_W
    tq = NA_QROWS * GRID_W
    nblk = length // tq
    nkb = BR_W // LANES
    return pl.pallas_call(
        functools.partial(_na_attn_kernel, rows=rows),
        grid=(bsz, NA_HEADS // 2, nblk),
        in_specs=[pl.BlockSpec((1, tq, LANES), lambda b, p, i: (b, i, p)),
                  pl.BlockSpec((1, length, LANES), lambda b, p, i: (b, 0, nkb + p)),
                  pl.BlockSpec((1, length, LANES), lambda b, p, i: (b, 0, 2 * nkb + p)),
                  pl.BlockSpec((1, lc, LANES), lambda b, p, i: (b, 0, nkb + p)),
                  pl.BlockSpec((1, lc, LANES), lambda b, p, i: (b, 0, 2 * nkb + p)),
                  pl.BlockSpec((2, 1, tq, NA_KROWS * GRID_W), lambda b, p, i: (p, i, 0, 0))],
        out_specs=pl.BlockSpec((1, tq, LANES), lambda b, p, i: (b, i, p)),
        out_shape=jax.ShapeDtypeStruct((bsz, length, BR_W), BF16),
        compiler_params=_cp("parallel", "parallel", "parallel"),
        name="na_attn",
    )(z_l, z_l, z_l, z_c, z_c, bias)


def _ctx_attn_kernel(q_ref, k_ref, v_ref, o_ref):
    q = q_ref[0] * (HD ** -0.5)
    k = k_ref[0]
    v = v_ref[0]
    lane = lax.broadcasted_iota(jnp.int32, (1, LANES), 1)
    outs = []
    for h in range(2):
        qm = jnp.where((lane < HD) if h == 0 else (lane >= HD), q, 0.0)
        s = _dot_nt(qm, k)
        e = jnp.exp(s - jnp.max(s, axis=-1, keepdims=True))
        outs.append(_dot(e * (1.0 / jnp.sum(e, axis=-1, keepdims=True)), v))
    o_ref[0] = jnp.where(lane < HD, outs[0], outs[1]).astype(o_ref.dtype)


def _ctx_attn(z_c):
    bsz, lc, _ = z_c.shape
    nkb = BR_W // LANES
    return pl.pallas_call(
        _ctx_attn_kernel,
        grid=(bsz, NA_HEADS // 2),
        in_specs=[pl.BlockSpec((1, lc, LANES), lambda b, p: (b, 0, p)),
                  pl.BlockSpec((1, lc, LANES), lambda b, p: (b, 0, nkb + p)),
                  pl.BlockSpec((1, lc, LANES), lambda b, p: (b, 0, 2 * nkb + p))],
        out_specs=pl.BlockSpec((1, lc, LANES), lambda b, p: (b, 0, p)),
        out_shape=jax.ShapeDtypeStruct((bsz, lc, BR_W), BF16),
        compiler_params=_cp("parallel", "parallel"),
        name="ctx_attn",
    )(z_c, z_c, z_c)


def _rope_tables(length):
    n_freq = HD // 4
    t = np.arange(length)
    pos = np.stack([t // GRID_W, t % GRID_W], axis=-1).astype(np.float32)
    inv = (ROPE_BASE ** (-np.arange(n_freq, dtype=np.float32) / n_freq)).astype(np.float32)
    lane = np.arange(BR_W)
    which = (lane % HD) // (HD // 2)
    ang = pos[:, which] * inv[lane % n_freq][None, :]
    sign = np.where((lane % (HD // 2)) < n_freq, -1.0, 1.0).astype(np.float32)
    return jnp.asarray(np.cos(ang), F32), jnp.asarray(np.sin(ang) * sign[None, :], F32)


def _na_bias(rpb, length):
    rows = length // GRID_W
    wr = min(NA_WIN_R, rows)
    nblk = rows // NA_QROWS
    qr = np.arange(rows)
    rstart = np.clip(qr - wr // 2, 0, rows - wr)
    kb = np.clip(np.arange(nblk) * NA_QROWS - NA_WIN_R // 2, 0, rows - NA_KROWS)
    qrow = (np.arange(nblk)[:, None] * NA_QROWS + np.arange(NA_QROWS)[None, :])
    krow = kb[:, None] + np.arange(NA_KROWS)[None, :]
    dr = krow[:, None, :] - qrow[:, :, None] + NA_WIN_R - 1
    rvalid = (krow[:, None, :] >= rstart[qrow][:, :, None]) & (krow[:, None, :] < rstart[qrow][:, :, None] + wr)
    n_dr = 2 * NA_WIN_R - 1
    oh_dr = (np.clip(dr, 0, n_dr - 1)[..., None] == np.arange(n_dr)) & rvalid[..., None]
    cid = np.arange(GRID_W)
    cstart = np.clip(cid - NA_WIN_C // 2, 0, GRID_W - NA_WIN_C)
    in_win = (cid[None, :] >= cstart[:, None]) & (cid[None, :] < cstart[:, None] + NA_WIN_C)
    ci = np.clip(cid[None, :] - cid[:, None], -(NA_WIN_C - 1), NA_WIN_C - 1) + NA_WIN_C - 1
    n_ci = 2 * NA_WIN_C - 1
    oh_ci = (ci[..., None] == np.arange(n_ci)).astype(np.float32)
    t1 = jnp.einsum('hdc,qkc->hdqk', rpb.astype(F32), jnp.asarray(oh_ci),
                    precision=lax.Precision.HIGHEST)
    big = jnp.einsum('brkd,hdxy->hbrxky', jnp.asarray(oh_dr.astype(np.float32)), t1,
                     precision=lax.Precision.HIGHEST)
    valid = rvalid[:, :, None, :, None] & in_win[None, None, :, None, :]
    big = jnp.where(jnp.asarray(valid)[None], big, NEG)
    return big.reshape(rpb.shape[0], nblk, NA_QROWS * GRID_W, NA_KROWS * GRID_W)


def _block_diag(blocks):
    n = len(blocks)
    r, c = blocks[0].shape
    out = jnp.zeros((n * r, n * c), blocks[0].dtype)
    for i, blk in enumerate(blocks):
        out = out.at[i * r:(i + 1) * r, i * c:(i + 1) * c].set(blk)
    return out


def _layer_params(p, i):
    w = BR_W
    w_in = p['w_in'][i]
    o1, o2, o3 = RW_IN, RW_IN + SSD_IN, RW_IN + SSD_IN + DA_IN
    d = w_in.shape[0]
    w_ssd = jnp.concatenate([w_in[:, o1:o2], jnp.zeros((d, SSD_IN_PAD - SSD_IN), w_in.dtype)], axis=1)
    head_id = np.arange(w) // HD
    ones_hd = jnp.asarray((head_id[:, None] == head_id[None, :]).astype(np.float32), BF16)
    lam_p = p['da_lambda'][i].astype(F32)
    lam_init = 0.8 - 0.6 * math.exp(-0.3 * i)
    lam = jnp.exp(jnp.sum(lam_p[0] * lam_p[1])) - jnp.exp(jnp.sum(lam_p[2] * lam_p[3])) + lam_init
    dtb = p['ssd_dt_bias'][i]
    return {
        'w_rw': w_in[:, :o1].astype(BF16), 'w_ssd': w_ssd.astype(BF16),
        'w_da': w_in[:, o2:o3].astype(BF16), 'w_na': w_in[:, o3:].astype(BF16),
        'rw_mu': p['rw_mu'][i], 'rw_w0': p['rw_w0'][i],
        'rw_wup': _block_diag([p['rw_w_up'][i, 0], p['rw_w_up'][i, 1]]).astype(BF16),
        'rw_a0': p['rw_a0'][i],
        'rw_aup': _block_diag([p['rw_a_up'][i, 0], p['rw_a_up'][i, 1]]).astype(BF16),
        'rw_gup': p['rw_g_up'][i].astype(BF16),
        'rw_kk': p['rw_k_k'][i].reshape(1, w), 'rw_ka': p['rw_k_a'][i].reshape(1, w),
        'rw_rk': p['rw_r_k'][i].reshape(1, w),
        'rw_lng': p['rw_ln_g'][i].reshape(1, w), 'rw_lnb': p['rw_ln_b'][i].reshape(1, w),
        'ones_hd': ones_hd, 'mean_hd': (ones_hd.astype(F32) / HD).astype(BF16),
        'ssd_cw': p['ssd_conv_w'][i], 'ssd_cb': p['ssd_conv_b'][i].reshape(1, SSD_XBC),
        'ssd_dtb': jnp.pad(dtb, ((0, 0), (0, LANES - SSD_HEADS))),
        'ssd_a': jnp.pad(-jnp.exp(p['ssd_a_log'][i].astype(F32)), ((0, 0), (0, LANES - SSD_HEADS))).reshape(2, 1, LANES),
        'ssd_dskip': jnp.repeat(p['ssd_d'][i], HD).reshape(1, w),
        'ssd_ng': p['ssd_norm_g'][i].reshape(1, w),
        'da_lam': jnp.broadcast_to(lam.reshape(1, 1), (1, LANES)).astype(F32),
        'da_lam_init': lam_init,
        'da_g': p['da_subln_g'][i].reshape(1, 2 * HD),
        'na_rpb': p['na_rpb'][i],
        'w_gate': p['w_gate'][i].astype(BF16),
        'gate_b': p['gate_b'][i][:, None, :],
        'w_br': p['w_br'][i].astype(BF16), 'w_out': p['w_out'][i].astype(BF16),
        'ffn_up': p['ffn_up'][i].astype(BF16), 'ffn_cw': p['ffn_conv_w'][i],
        'ffn_cb': p['ffn_conv_b'][i], 'ffn_down': p['ffn_down'][i].astype(BF16),
    }


def _project(h, lp):
    bsz, length, d = h.shape
    h2 = h.reshape(bsz * length, d)
    return tuple(_matmul(h2, lp[n], F32).reshape(bsz, length, -1)
                 for n in ('w_rw', 'w_ssd', 'w_da', 'w_na'))


def _ffn(x, h2, lp, gate):
    act = _ffn_up(h2, lp['ffn_up'], lp['ffn_cw'], lp['ffn_cb'])
    return _matmul_residual(act, lp['ffn_down'], x, gate)


def kernel(x, c, ctx, c_ctx, ada_w, ada_b, norm1_g, norm2_g, w_in, rw_mu, rw_w0, rw_w_up, rw_a0, rw_a_up, rw_g_up, rw_k_k, rw_k_a, rw_r_k, rw_ln_g, rw_ln_b, ssd_conv_w, ssd_conv_b, ssd_dt_bias, ssd_a_log, ssd_d, ssd_norm_g, da_lambda, da_subln_g, na_rpb, w_gate, gate_b, w_br, w_out, ffn_up, ffn_conv_w, ffn_conv_b, ffn_down, final_norm_g):
    p = dict(w_in=w_in, rw_mu=rw_mu, rw_w0=rw_w0, rw_w_up=rw_w_up, rw_a0=rw_a0, rw_a_up=rw_a_up,
             rw_g_up=rw_g_up, rw_k_k=rw_k_k, rw_k_a=rw_k_a, rw_r_k=rw_r_k, rw_ln_g=rw_ln_g,
             rw_ln_b=rw_ln_b, ssd_conv_w=ssd_conv_w, ssd_conv_b=ssd_conv_b, ssd_dt_bias=ssd_dt_bias,
             ssd_a_log=ssd_a_log, ssd_d=ssd_d, ssd_norm_g=ssd_norm_g, da_lambda=da_lambda,
             da_subln_g=da_subln_g, na_rpb=na_rpb, w_gate=w_gate, gate_b=gate_b, w_br=w_br,
             w_out=w_out, ffn_up=ffn_up, ffn_conv_w=ffn_conv_w, ffn_conv_b=ffn_conv_b,
             ffn_down=ffn_down)
    bsz, seq, d = x.shape
    depth = ada_w.shape[0]
    lctx = ctx.shape[1]
    mrows = -(-(bsz + 1) // 16) * 16
    cc = jnp.zeros((mrows, d), F32).at[:bsz].set(c).at[bsz].set(c_ctx)
    cos, sin = _rope_tables(seq)
    xl, xc = x, ctx
    for i in range(depth):
        last = i == depth - 1
        lp = _layer_params(p, i)
        mod = _modulation(cc, ada_w[i], ada_b[i]).reshape(mrows, 6, d)
        ml = [mod[:bsz, n][:, None, :] for n in range(6)]
        mc = [mod[bsz:bsz + 1, n][:, None, :] for n in range(6)]

        hl = _norm_mod(xl, norm1_g[i], ml[0], ml[1])
        hc = _norm_mod(xc, norm1_g[i], mc[0], mc[1])
        zl = _project(hl, lp)
        zc = _project(hc, lp)

        rp_c = _rw_prep(zc[0], lp)
        rp_l = _rw_prep(zl[0], lp)
        s_zero = jnp.zeros((bsz, 2, RW_HEADS // RW_GROUP, RW_GW, RW_GW), F32)
        o_c, s_c = _rw_scan(rp_c, s_zero)
        o_l, _ = _rw_scan(rp_l, s_c)
        a_l = _rw_readout(o_l, rp_l, lp)

        xbc_c, dt_c = _ssd_prep(zc[1], lp)
        xbc_l, dt_l = _ssd_prep(zl[1], lp)
        h_zero = jnp.zeros((bsz, 2, SSD_HEADS, SSD_STATE, HD), F32)
        y_c, hfin_c = _ssd_scan(xbc_c, dt_c, lp['ssd_a'], h_zero)
        y_l, _ = _ssd_scan(xbc_l, dt_l, lp['ssd_a'], hfin_c)
        b_l = _ssd_readout(y_l, xbc_l, zl[1], lp)

        q_l, k_l, v_l = _da_prep(zl[2], cos, sin, True)
        q_c, k_c, v_c = _da_prep(zc[2], cos, sin, False)
        c_l = _da_attn(q_l, [k_l, k_c], [v_l, v_c], lp['da_lam'], lp['da_g'], lp['da_lam_init'])

        d_l = _na_attn(zl[3], zc[3], _na_bias(lp['na_rpb'], seq))

        merged = _gated_merge(hl.reshape(bsz * seq, d),
                              [t.reshape(bsz * seq, BR_W) for t in (a_l, b_l, c_l, d_l)],
                              lp['w_gate'], lp['gate_b'], lp['w_br']).reshape(bsz, seq, d)
        xl = _matmul_residual(merged, lp['w_out'], xl, ml[2])
        hl2 = _norm_mod(xl, norm2_g[i], ml[3], ml[4])
        xl = _ffn(xl, hl2, lp, ml[5])

        if not last:
            a_c = _rw_readout(o_c, rp_c, lp)
            b_c = _ssd_readout(y_c, xbc_c, zc[1], lp)
            c_c = _da_attn(q_c, [k_c], [v_c], lp['da_lam'], lp['da_g'], lp['da_lam_init'])
            d_c = _ctx_attn(zc[3])
            merged_c = _gated_merge(hc.reshape(bsz * lctx, d),
                                    [t.reshape(bsz * lctx, BR_W) for t in (a_c, b_c, c_c, d_c)],
                                    lp['w_gate'], lp['gate_b'], lp['w_br']).reshape(bsz, lctx, d)
            xc = _matmul_residual(merged_c, lp['w_out'], xc, mc[2])
            hc2 = _norm_mod(xc, norm2_g[i], mc[3], mc[4])
            xc = _ffn(xc, hc2, lp, mc[5])
    return _final_norm(xl, final_norm_g)
```

```python
import functools
import math

import numpy as np
import jax
import jax.numpy as jnp
from jax import lax
from jax.experimental import pallas as pl
from jax.experimental.pallas import tpu as pltpu

F32 = jnp.float32
BF16 = jnp.bfloat16

GRID_W = 64
NORM_EPS = 1e-6
HD = 64
BR_W = 512
RW_HEADS = 8
RW_RANK = 64
RW_GATE_RANK = 128
RW_GN_EPS = 64e-5
RW_IN = 3 * BR_W + 4 * RW_RANK + RW_GATE_RANK
RW_CHUNK = 64
RW_NARR = 11
RW_GROUP = 4
RW_GW = RW_GROUP * HD
SSD_HEADS = 8
SSD_GROUPS = 2
SSD_STATE = 128
SSD_XBC = BR_W + 2 * SSD_GROUPS * SSD_STATE
SSD_IN = BR_W + SSD_XBC + 2 * SSD_HEADS
SSD_IN_PAD = BR_W + SSD_XBC + 128
SSD_CHUNK = 128
SSD_GW = SSD_HEADS // SSD_GROUPS * HD
DA_HEADS = 4
DA_IN = 3 * BR_W
DA_SUBLN_EPS = 1e-5
NA_HEADS = 8
NA_IN = 3 * BR_W
NA_WIN_R = 8
NA_WIN_C = 16
NA_QROWS = 4
NA_KROWS = 12
ROPE_BASE = 10000.0
NEG = -1e30
LANES = 128
VMEM_LIMIT = 56 * 1024 * 1024


def _cp(*sem):
    return pltpu.CompilerParams(dimension_semantics=sem, vmem_limit_bytes=VMEM_LIMIT)


def _tile(n, pref, mult):
    t = min(n, pref)
    t -= t % mult
    while t >= mult:
        if n % t == 0:
            return t
        t -= mult
    return n


def _dot(a, b):
    return jnp.dot(a.astype(BF16), b.astype(BF16), preferred_element_type=F32)


def _dot_nt(a, b):
    return lax.dot_general(a.astype(BF16), b.astype(BF16), (((1,), (1,)), ((), ())),
                           preferred_element_type=F32)


def _dot_tn(a, b):
    return lax.dot_general(a.astype(BF16), b.astype(BF16), (((0,), (0,)), ((), ())),
                           preferred_element_type=F32)


def _split3(x):
    hi = x.astype(BF16)
    r1 = x - hi.astype(F32)
    mid = r1.astype(BF16)
    lo = (r1 - mid.astype(F32)).astype(BF16)
    return hi, mid, lo


def _dot01(m01, x):
    hi, mid, lo = _split3(x)
    return (jnp.dot(m01, hi, preferred_element_type=F32)
            + jnp.dot(m01, mid, preferred_element_type=F32)
            + jnp.dot(m01, lo, preferred_element_type=F32))


def _x_dot01(x, m01):
    hi, mid, lo = _split3(x)
    return (jnp.dot(hi, m01, preferred_element_type=F32)
            + jnp.dot(mid, m01, preferred_element_type=F32)
            + jnp.dot(lo, m01, preferred_element_type=F32))


def _sigmoid(x):
    return 1.0 / (1.0 + jnp.exp(-x))


def _silu(x):
    return x * _sigmoid(x)


def _softplus(x):
    return jnp.maximum(x, 0.0) + jnp.log(1.0 + jnp.exp(-jnp.abs(x)))


def _mod_kernel(c_ref, w_ref, b_ref, o_ref):
    a = _silu(c_ref[...])
    o_ref[...] = _dot(a, w_ref[...]) + b_ref[...]


def _modulation(cc, w, b):
    m, d = cc.shape
    n = w.shape[1]
    tn = _tile(n, 1024, LANES)
    return pl.pallas_call(
        _mod_kernel,
        grid=(n // tn,),
        in_specs=[pl.BlockSpec((m, d), lambda j: (0, 0)),
                  pl.BlockSpec((d, tn), lambda j: (0, j)),
                  pl.BlockSpec((1, tn), lambda j: (0, j))],
        out_specs=pl.BlockSpec((m, tn), lambda j: (0, j)),
        out_shape=jax.ShapeDtypeStruct((m, n), F32),
        compiler_params=_cp("parallel"),
        name="adaln_mod",
    )(cc, w, b.reshape(1, n))


def _norm_mod_kernel(x_ref, g_ref, sh_ref, sc_ref, o_ref):
    x = x_ref[0]
    y = x * lax.rsqrt(jnp.mean(x * x, axis=-1, keepdims=True) + NORM_EPS) * g_ref[...]
    o_ref[0] = (y * (1.0 + sc_ref[0]) + sh_ref[0]).astype(o_ref.dtype)


def _norm_mod(x, g, shift, scale):
    bsz, length, d = x.shape
    tr = _tile(length, 256, 16)
    bm = shift.shape[0]
    mod_map = (lambda b, j: (b, 0, 0)) if bm == bsz else (lambda b, j: (0, 0, 0))
    return pl.pallas_call(
        _norm_mod_kernel,
        grid=(bsz, length // tr),
        in_specs=[pl.BlockSpec((1, tr, d), lambda b, j: (b, j, 0)),
                  pl.BlockSpec((1, d), lambda b, j: (0, 0)),
                  pl.BlockSpec((1, 1, d), mod_map),
                  pl.BlockSpec((1, 1, d), mod_map)],
        out_specs=pl.BlockSpec((1, tr, d), lambda b, j: (b, j, 0)),
        out_shape=jax.ShapeDtypeStruct((bsz, length, d), BF16),
        compiler_params=_cp("parallel", "parallel"),
        name="norm_mod",
    )(x, g.reshape(1, d), shift, scale)


def _final_norm_kernel(x_ref, g_ref, o_ref):
    x = x_ref[0]
    o_ref[0] = x * lax.rsqrt(jnp.mean(x * x, axis=-1, keepdims=True) + NORM_EPS) * g_ref[...]


def _final_norm(x, g):
    bsz, length, d = x.shape
    tr = _tile(length, 256, 8)
    return pl.pallas_call(
        _final_norm_kernel,
        grid=(bsz, length // tr),
        in_specs=[pl.BlockSpec((1, tr, d), lambda b, j: (b, j, 0)),
                  pl.BlockSpec((1, d), lambda b, j: (0, 0))],
        out_specs=pl.BlockSpec((1, tr, d), lambda b, j: (b, j, 0)),
        out_shape=jax.ShapeDtypeStruct((bsz, length, d), F32),
        compiler_params=_cp("parallel", "parallel"),
        name="final_norm",
    )(x, g.reshape(1, d))


def _mm_kernel(a_ref, w_ref, o_ref):
    o_ref[...] = jnp.dot(a_ref[...], w_ref[...], preferred_element_type=F32).astype(o_ref.dtype)


def _matmul(a, w, out_dtype, tm_pref=512, tn_pref=2048):
    m, k = a.shape
    n = w.shape[1]
    tm = _tile(m, tm_pref, 16)
    tn = _tile(n, tn_pref, LANES)
    return pl.pallas_call(
        _mm_kernel,
        grid=(m // tm, n // tn),
        in_specs=[pl.BlockSpec((tm, k), lambda i, j: (i, 0)),
                  pl.BlockSpec((k, tn), lambda i, j: (0, j))],
        out_specs=pl.BlockSpec((tm, tn), lambda i, j: (i, j)),
        out_shape=jax.ShapeDtypeStruct((m, n), out_dtype),
        compiler_params=_cp("parallel", "parallel"),
        name="matmul",
    )(a, w)


def _mm_res_kernel(a_ref, w_ref, x_ref, g_ref, o_ref):
    y = jnp.dot(a_ref[0], w_ref[...], preferred_element_type=F32)
    o_ref[0] = x_ref[0] + g_ref[0] * y


def _matmul_residual(a, w, x, gate):
    bsz, length, k = a.shape
    n = w.shape[1]
    tm = _tile(length, 1024, 16)
    tn = _tile(n, 512, LANES)
    bm = gate.shape[0]
    gmap = (lambda b, i, j: (b, 0, j)) if bm == bsz else (lambda b, i, j: (0, 0, j))
    return pl.pallas_call(
        _mm_res_kernel,
        grid=(bsz, length // tm, n // tn),
        in_specs=[pl.BlockSpec((1, tm, k), lambda b, i, j: (b, i, 0)),
                  pl.BlockSpec((k, tn), lambda b, i, j: (0, j)),
                  pl.BlockSpec((1, tm, tn), lambda b, i, j: (b, i, j)),
                  pl.BlockSpec((1, 1, tn), gmap)],
        out_specs=pl.BlockSpec((1, tm, tn), lambda b, i, j: (b, i, j)),
        out_shape=jax.ShapeDtypeStruct((bsz, length, n), F32),
        compiler_params=_cp("parallel", "parallel", "parallel"),
        name="matmul_residual",
    )(a, w, x, gate)


def _merge_kernel(h_ref, o0_ref, o1_ref, o2_ref, o3_ref, wg_ref, gb_ref, wbr_ref, out_ref):
    h = h_ref[...]
    acc = None
    for n, o_ref in enumerate((o0_ref, o1_ref, o2_ref, o3_ref)):
        gate = _sigmoid(jnp.dot(h, wg_ref[n], preferred_element_type=F32) + gb_ref[n])
        term = gate * jnp.dot(o_ref[...], wbr_ref[n], preferred_element_type=F32)
        acc = term if acc is None else acc + term
    out_ref[...] = acc.astype(out_ref.dtype)


def _gated_merge(h, branches, wg, gb, wbr):
    m, d = h.shape
    nb = len(branches)
    tm = _tile(m, 1024, 16)
    tn = _tile(d, 256, LANES)
    bspec = pl.BlockSpec((tm, BR_W), lambda i, j: (i, 0))
    return pl.pallas_call(
        _merge_kernel,
        grid=(m // tm, d // tn),
        in_specs=[pl.BlockSpec((tm, d), lambda i, j: (i, 0)), bspec, bspec, bspec, bspec,
                  pl.BlockSpec((nb, d, tn), lambda i, j: (0, 0, j)),
                  pl.BlockSpec((nb, 1, tn), lambda i, j: (0, 0, j)),
                  pl.BlockSpec((nb, BR_W, tn), lambda i, j: (0, 0, j))],
        out_specs=pl.BlockSpec((tm, tn), lambda i, j: (i, j)),
        out_shape=jax.ShapeDtypeStruct((m, d), BF16),
        compiler_params=_cp("parallel", "parallel"),
        name="gated_merge",
    )(h, *branches, wg, gb, wbr)


def _conv3(x, prev_row, next_row, w_ref, b_ref):
    rows = x.shape[0]
    rid = lax.broadcasted_iota(jnp.int32, x.shape, 0)
    prev = jnp.where(rid == 0, prev_row, pltpu.roll(x, 1, 0))
    nxt = jnp.where(rid == rows - 1, next_row, pltpu.roll(x, rows - 1, 0))
    return b_ref[...] + prev * w_ref[0:1, :] + x * w_ref[1:2, :] + nxt * w_ref[2:3, :]


def _ffn_up_kernel(a_ref, ap_ref, an_ref, wg_ref, wv_ref, cwg_ref, cbg_ref, cwv_ref, cbv_ref,
                   o_ref, *, nt, halo):
    i = pl.program_id(1)
    tm = a_ref.shape[1]
    a_ext = jnp.concatenate([ap_ref[0], a_ref[0], an_ref[0]], axis=0)
    n_ext = tm + 2 * halo
    rid = lax.broadcasted_iota(jnp.int32, (tm, 1), 0)
    no_prev = jnp.logical_and(i == 0, rid == 0)
    no_next = jnp.logical_and(i == nt - 1, rid == tm - 1)

    def branch(w_ref, cw_ref, cb_ref):
        u = jnp.dot(a_ext, w_ref[...], preferred_element_type=F32)
        um1 = jnp.where(no_prev, 0.0, pltpu.roll(u, 1, 0)[halo:halo + tm])
        up1 = jnp.where(no_next, 0.0, pltpu.roll(u, n_ext - 1, 0)[halo:halo + tm])
        return (cb_ref[...] + um1 * cw_ref[0:1, :] + u[halo:halo + tm] * cw_ref[1:2, :]
                + up1 * cw_ref[2:3, :])

    gate = branch(wg_ref, cwg_ref, cbg_ref)
    val = branch(wv_ref, cwv_ref, cbv_ref)
    o_ref[0] = (_silu(gate) * val).astype(o_ref.dtype)


def _ffn_up(h, w_up, conv_w, conv_b):
    bsz, length, d = h.shape
    f2 = w_up.shape[1]
    f = f2 // 2
    halo = 16
    tm = _tile(length, 1024, halo)
    tc = _tile(f, 512, LANES)
    nt = length // tm
    nc = f // tc
    nh = length // halo
    rb = tm // halo

    def wspec(rows, off):
        return pl.BlockSpec((rows, tc), lambda b, i, c: (0, c + off))

    cb = conv_b.reshape(1, f2)
    return pl.pallas_call(
        functools.partial(_ffn_up_kernel, nt=nt, halo=halo),
        grid=(bsz, nt, nc),
        in_specs=[pl.BlockSpec((1, tm, d), lambda b, i, c: (b, i, 0)),
                  pl.BlockSpec((1, halo, d), lambda b, i, c: (b, jnp.maximum(i * rb - 1, 0), 0)),
                  pl.BlockSpec((1, halo, d), lambda b, i, c: (b, jnp.minimum((i + 1) * rb, nh - 1), 0)),
                  wspec(d, 0), wspec(d, nc), wspec(3, 0), wspec(1, 0), wspec(3, nc), wspec(1, nc)],
        out_specs=pl.BlockSpec((1, tm, tc), lambda b, i, c: (b, i, c)),
        out_shape=jax.ShapeDtypeStruct((bsz, length, f), BF16),
        compiler_params=_cp("parallel", "parallel", "parallel"),
        name="ffn_up",
    )(h, h, h, w_up, w_up, conv_w, cb, conv_w, cb)


def _rw_prep_kernel(z_ref, zp_ref, zn_ref, mu_ref, w0_ref, wup_ref, a0_ref, aup_ref, gup_ref,
                    kk_ref, ka_ref, rk_ref, ones_ref, o_ref, *, nt):
    j = pl.program_id(1)
    p = z_ref[0]
    rows = p.shape[0]
    rid = lax.broadcasted_iota(jnp.int32, p.shape, 0)
    prow = jnp.where(j > 0, zp_ref[0, 7:8, :], 0.0)
    nrow = jnp.where(j < nt - 1, zn_ref[0, 0:1, :], 0.0)
    prev = jnp.where(rid == 0, prow, pltpu.roll(p, 1, 0))
    nxt = jnp.where(rid == rows - 1, nrow, pltpu.roll(p, rows - 1, 0))
    ps = p + mu_ref[0:1, :] * (prev - p) + mu_ref[1:2, :] * (nxt - p)

    w = BR_W
    r = ps[:, 0:w]
    k = ps[:, w:2 * w]
    v = ps[:, 2 * w:3 * w]
    wd = ps[:, 3 * w:3 * w + 128]
    ad = ps[:, 3 * w + 128:3 * w + 256]
    gd = ps[:, 3 * w + 256:3 * w + 384]

    wraw = _dot(jnp.tanh(wd), wup_ref[...])
    araw = _dot(ad, aup_ref[...])
    g = _dot(_sigmoid(gd), gup_ref[...])
    ones = ones_ref[...]
    kkv = k * kk_ref[...]
    ss = _x_dot01(kkv * kkv, ones)
    kkn = kkv / jnp.maximum(jnp.sqrt(ss), 1e-12)

    def put(i, x):
        o_ref[0, :, i * w:(i + 1) * w] = x

    put(0, r)
    put(1, v)
    put(2, kkn)
    ksum = None
    for d in range(2):
        lw = -math.exp(-0.5) * _sigmoid(w0_ref[d:d + 1, :] + wraw[:, d * w:(d + 1) * w])
        a = _sigmoid(a0_ref[d:d + 1, :] + araw[:, d * w:(d + 1) * w])
        kd = k * (1.0 + (a - 1.0) * ka_ref[...])
        put(3 + 3 * d, lw)
        put(4 + 3 * d, kd)
        put(5 + 3 * d, kkn * a)
        ksum = kd if ksum is None else ksum + kd
    put(9, g)
    put(10, _x_dot01(r * rk_ref[...] * ksum, ones) * v)


def _rw_prep(z, lp):
    bsz, length, fin = z.shape
    tr = _tile(length, 256, 8)
    nt = length // tr
    rb = tr // 8
    nh = length // 8
    full = lambda a: pl.BlockSpec(a.shape, lambda b, j: (0,) * a.ndim)
    params = (lp['rw_mu'], lp['rw_w0'], lp['rw_wup'], lp['rw_a0'], lp['rw_aup'], lp['rw_gup'],
              lp['rw_kk'], lp['rw_ka'], lp['rw_rk'], lp['ones_hd'])
    return pl.pallas_call(
        functools.partial(_rw_prep_kernel, nt=nt),
        grid=(bsz, nt),
        in_specs=[pl.BlockSpec((1, tr, fin), lambda b, j: (b, j, 0)),
                  pl.BlockSpec((1, 8, fin), lambda b, j: (b, jnp.maximum(j * rb - 1, 0), 0)),
                  pl.BlockSpec((1, 8, fin), lambda b, j: (b, jnp.minimum((j + 1) * rb, nh - 1), 0))]
                 + [full(a) for a in params],
        out_specs=pl.BlockSpec((1, tr, RW_NARR * BR_W), lambda b, j: (b, j, 0)),
        out_shape=jax.ShapeDtypeStruct((bsz, length, RW_NARR * BR_W), F32),
        compiler_params=_cp("parallel", "parallel"),
        name="rwkv_prep",
    )(z, z, z, *params)


def _rw_chunks(dir_refs, o_refs, s_scr):
    cs = RW_CHUNK
    gw = RW_GW
    ng = RW_HEADS // RW_GROUP
    ri = lax.broadcasted_iota(jnp.int32, (gw, gw), 0)
    ci = lax.broadcasted_iota(jnp.int32, (gw, gw), 1)
    head_mask = (ri >> 6) == (ci >> 6)
    tw = lax.broadcasted_iota(jnp.int32, (cs, gw), 0)
    iw = lax.broadcasted_iota(jnp.int32, (cs, gw), 1) & (cs - 1)
    eye = jnp.where(iw == tw, 1.0, 0.0)
    same = {s: (tw >> int(math.log2(s))) == (iw >> int(math.log2(s))) for s in (8, 16, 32)}
    ti = lax.broadcasted_iota(jnp.int32, (cs, cs), 0)
    ii = lax.broadcasted_iota(jnp.int32, (cs, cs), 1)

    def expand(x):
        return jnp.where(head_mask, jnp.concatenate([x] * RW_GROUP, axis=0), 0.0).astype(BF16)

    def mm(a, b):
        return _dot(a, expand(b))

    chains = []
    for d, (r_ref, v_ref, kk_ref, lw_ref, k_ref, b_ref) in enumerate(dir_refs):
        sgn = 1 - 2 * d
        rel = (iw - tw) * sgn
        tri = jnp.where((ii - ti) * sgn <= 0, 1.0, 0.0).astype(BF16)
        lw = lw_ref[0]
        cum = _dot01(tri, lw)
        tot = jnp.sum(lw, axis=0, keepdims=True)
        qt = kk_ref[0] * jnp.exp(cum - lw)
        rt = r_ref[0] * jnp.exp(cum)
        e_neg = jnp.exp(-cum)
        e_end = jnp.exp(tot - cum)
        kt = k_ref[0] * e_neg
        bt = b_ref[0] * e_neg
        kh = k_ref[0] * e_end
        bh = b_ref[0] * e_end
        p_tot = jnp.exp(tot)
        v = v_ref[0]
        for g in range(ng):
            sl = slice(g * gw, (g + 1) * gw)
            chains.append(dict(d=d, g=g, sl=sl, strict=rel < 0, incl=rel <= 0, q=qt[:, sl],
                               r=rt[:, sl], v=v[:, sl], kt=kt[:, sl], bt=bt[:, sl], kh=kh[:, sl],
                               bh=bh[:, sl], p_tot=p_tot[:, sl], s0=s_scr[d, g]))

    for c in chains:
        c['kte'], c['bte'], c['ve'] = expand(c['kt']), expand(c['bt']), expand(c['v'])
        c['s0b'] = c['s0'].astype(BF16)
    for c in chains:
        c['lmat'] = jnp.where(c['strict'], _dot_nt(c['q'], c['bte']), 0.0)
        c['ld'] = jnp.where(same[8], c['lmat'], 0.0)
    for c in chains:
        c['ld2'] = mm(c['ld'], c['ld'])
    for c in chains:
        c['a_qk'] = jnp.where(c['strict'], _dot_nt(c['q'], c['kte']), 0.0)
    for c in chains:
        c['ld4'] = mm(c['ld2'], c['ld2'])
    for c in chains:
        c['x'] = mm(eye - c['ld'], eye + c['ld2'])
    for c in chains:
        c['a_rk'] = jnp.where(c['incl'], _dot_nt(c['r'], c['kte']), 0.0)
    for c in chains:
        c['x'] = mm(c['x'], eye + c['ld4'])
    for c in chains:
        c['a_rb'] = jnp.where(c['incl'], _dot_nt(c['r'], c['bte']), 0.0)
    for c in chains:
        c['rhs'] = _dot_nt(c['q'], c['s0b']) + _dot(c['a_qk'], c['ve'])
    for s in (8, 16, 32):
        off_mask = jnp.logical_not(same[s])
        if 2 * s < cs:
            off_mask = jnp.logical_and(same[2 * s], off_mask)
        for c in chains:
            c['t'] = mm(c['x'], jnp.where(off_mask, c['lmat'], 0.0))
        if s == 8:
            for c in chains:
                c['o'] = _dot_nt(c['r'], c['s0b']) + _dot(c['a_rk'], c['ve'])
        for c in chains:
            c['x'] = c['x'] - mm(c['t'], c['x'])
    for c in chains:
        c['sa'] = mm(c['x'], c['rhs'])
    for c in chains:
        o_refs[c['d']][0, :, c['sl']] = c['o'] - mm(c['a_rb'], c['sa'])
    for c in chains:
        upd = _dot_tn(jnp.concatenate([c['v'], c['sa']], axis=0),
                      jnp.concatenate([c['kh'], -c['bh']], axis=0))
        s_scr[c['d'], c['g']] = c['s0'] * c['p_tot'] + jnp.where(head_mask, upd, 0.0)


def _rw_scan_kernel(*refs, nc):
    fwd, bwd = refs[0:6], refs[6:12]
    s0_ref, of_ref, ob_ref, sfin_ref, s_scr = refs[12:]
    c = pl.program_id(1)

    @pl.when(c == 0)
    def _():
        s_scr[...] = s0_ref[0]

    _rw_chunks((fwd, bwd), (of_ref, ob_ref), s_scr)

    @pl.when(c == nc - 1)
    def _():
        sfin_ref[0] = s_scr[...]


def _rw_scan(rwp, s_init):
    bsz, length, _ = rwp.shape
    cs = RW_CHUNK
    nc = length // cs
    ng = RW_HEADS // RW_GROUP

    def arr(d, i):
        if d == 0:
            return pl.BlockSpec((1, cs, BR_W), lambda b, c: (b, c, i))
        return pl.BlockSpec((1, cs, BR_W), lambda b, c: (b, nc - 1 - c, i))

    def specs(d):
        return [arr(d, 0), arr(d, 1), arr(d, 2)] + [arr(d, 3 + 3 * d + i) for i in range(3)]

    sspec = pl.BlockSpec((1, 2, ng, RW_GW, RW_GW), lambda b, c: (b, 0, 0, 0, 0))
    oshape = jax.ShapeDtypeStruct((bsz, length, BR_W), F32)
    return pl.pallas_call(
        functools.partial(_rw_scan_kernel, nc=nc),
        grid=(bsz, nc),
        in_specs=specs(0) + specs(1) + [sspec],
        out_specs=[pl.BlockSpec((1, cs, BR_W), lambda b, c: (b, c, 0)),
                   pl.BlockSpec((1, cs, BR_W), lambda b, c: (b, nc - 1 - c, 0)), sspec],
        out_shape=[oshape, oshape, jax.ShapeDtypeStruct((bsz, 2, ng, RW_GW, RW_GW), F32)],
        scratch_shapes=[pltpu.VMEM((2, ng, RW_GW, RW_GW), F32)],
        compiler_params=_cp("parallel", "arbitrary"),
        name="rwkv_scan",
    )(*([rwp] * 12), s_init)


def _rw_readout_kernel(of_ref, ob_ref, g_ref, bonus_ref, lng_ref, lnb_ref, mean_ref, out_ref):
    o = of_ref[0] + ob_ref[0]
    mean_m = mean_ref[...]
    mu = _x_dot01(o, mean_m)
    dlt = o - mu
    var = _x_dot01(dlt * dlt, mean_m)
    on = dlt * lax.rsqrt(var + RW_GN_EPS) * lng_ref[...] + lnb_ref[...]
    out_ref[0] = ((on + bonus_ref[0]) * g_ref[0]).astype(out_ref.dtype)


def _rw_readout(o_f, o_b, rwp, lp):
    bsz, length, w = o_f.shape
    tr = _tile(length, 256, 16)
    ospec = pl.BlockSpec((1, tr, w), lambda b, j: (b, j, 0))
    return pl.pallas_call(
        _rw_readout_kernel,
        grid=(bsz, length // tr),
        in_specs=[ospec, ospec,
                  pl.BlockSpec((1, tr, w), lambda b, j: (b, j, 9)),
                  pl.BlockSpec((1, tr, w), lambda b, j: (b, j, 10)),
                  pl.BlockSpec((1, w), lambda b, j: (0, 0)),
                  pl.BlockSpec((1, w), lambda b, j: (0, 0)),
                  pl.BlockSpec((w, w), lambda b, j: (0, 0))],
        out_specs=ospec,
        out_shape=jax.ShapeDtypeStruct((bsz, length, w), BF16),
        compiler_params=_cp("parallel", "parallel"),
        name="rwkv_readout",
    )(o_f, o_b, rwp, rwp, lp['rw_lng'], lp['rw_lnb'], lp['mean_hd'])


def _ssd_prep_kernel(z_ref, zp_ref, zn_ref, cw_ref, cb_ref, dtb_ref, xbc_ref, dt_ref, *, nt):
    j = pl.program_id(1)
    x = z_ref[0, :, BR_W:BR_W + SSD_XBC]
    prow = jnp.where(j > 0, zp_ref[0, 7:8, BR_W:BR_W + SSD_XBC], 0.0)
    nrow = jnp.where(j < nt - 1, zn_ref[0, 0:1, BR_W:BR_W + SSD_XBC], 0.0)
    xbc_ref[0] = _silu(_conv3(x, prow, nrow, cw_ref, cb_ref))
    dt_raw = z_ref[0, :, BR_W + SSD_XBC:BR_W + SSD_XBC + LANES]
    dt_ref[0, 0] = _softplus(dt_raw + dtb_ref[0:1, :])
    dt_ref[0, 1] = _softplus(pltpu.roll(dt_raw, LANES - SSD_HEADS, 1) + dtb_ref[1:2, :])


def _ssd_prep(z, lp):
    bsz, length, fin = z.shape
    tr = _tile(length, 256, 8)
    nt = length // tr
    rb = tr // 8
    nh = length // 8
    return pl.pallas_call(
        functools.partial(_ssd_prep_kernel, nt=nt),
        grid=(bsz, nt),
        in_specs=[pl.BlockSpec((1, tr, fin), lambda b, j: (b, j, 0)),
                  pl.BlockSpec((1, 8, fin), lambda b, j: (b, jnp.maximum(j * rb - 1, 0), 0)),
                  pl.BlockSpec((1, 8, fin), lambda b, j: (b, jnp.minimum((j + 1) * rb, nh - 1), 0)),
                  pl.BlockSpec((3, SSD_XBC), lambda b, j: (0, 0)),
                  pl.BlockSpec((1, SSD_XBC), lambda b, j: (0, 0)),
                  pl.BlockSpec((2, LANES), lambda b, j: (0, 0))],
        out_specs=[pl.BlockSpec((1, tr, SSD_XBC), lambda b, j: (b, j, 0)),
                   pl.BlockSpec((1, 2, tr, LANES), lambda b, j: (b, 0, j, 0))],
        out_shape=[jax.ShapeDtypeStruct((bsz, length, SSD_XBC), F32),
                   jax.ShapeDtypeStruct((bsz, 2, length, LANES), F32)],
        compiler_params=_cp("parallel", "parallel"),
        name="ssd_prep",
    )(z, z, z, lp['ssd_cw'], lp['ssd_cb'], lp['ssd_dtb'])


def _ssd_chunks(dir_refs, a_ref, y_refs, h_scr):
    q = SSD_CHUNK
    n = SSD_STATE
    rep = SSD_HEADS // SSD_GROUPS
    gw = SSD_GW
    ti = lax.broadcasted_iota(jnp.int32, (q, q), 0)
    ii = lax.broadcasted_iota(jnp.int32, (q, q), 1)
    ri = lax.broadcasted_iota(jnp.int32, (rep * q, gw), 0)
    ci = lax.broadcasted_iota(jnp.int32, (rep * q, gw), 1)
    head_mask = (ri >> int(math.log2(q))) == (ci >> int(math.log2(HD)))

    def per_head(x, h0):
        return jnp.concatenate([jnp.broadcast_to(x[:, h0 + h:h0 + h + 1], (x.shape[0], HD))
                                for h in range(rep)], axis=1)

    work = []
    for d, (xbc_ref, dt_ref) in enumerate(dir_refs):
        incl = (ii - ti) * (1 - 2 * d) <= 0
        tri = jnp.where(incl, 1.0, 0.0).astype(BF16)
        dt = dt_ref[0, 0]
        dta = dt * a_ref[d]
        acs = _dot01(tri, dta)
        acs_t = acs.T
        tot = jnp.sum(dta, axis=0, keepdims=True)
        e_acs = jnp.exp(acs)
        e_end = jnp.exp(tot - acs)
        e_tot = jnp.exp(tot)
        xbc = xbc_ref[0]
        for g in range(SSD_GROUPS):
            h0 = g * rep
            bg = xbc[:, BR_W + g * n:BR_W + (g + 1) * n]
            cg = xbc[:, BR_W + SSD_GROUPS * n + g * n:BR_W + SSD_GROUPS * n + (g + 1) * n]
            xdt = xbc[:, g * gw:(g + 1) * gw] * per_head(dt, h0)
            dec = jnp.concatenate(
                [jnp.exp(jnp.where(incl, acs[:, h0 + h:h0 + h + 1] - acs_t[h0 + h:h0 + h + 1, :], NEG))
                 for h in range(rep)], axis=1)
            work.append(dict(d=d, g=g, bg=bg, cg=cg, xdt=xdt, dec=dec, ht=h_scr[d, g],
                             e_acs=per_head(e_acs, h0), e_end=per_head(e_end, h0),
                             e_tot=per_head(e_tot, h0)))
    for w in work:
        w['cb'] = _dot_nt(w['cg'], w['bg'])
    for w in work:
        w['yoff'] = _dot(w['cg'], w['ht'])
    for w in work:
        w['st'] = _dot_tn(w['bg'], w['xdt'] * w['e_end'])
    for w in work:
        xe = jnp.where(head_mask, jnp.concatenate([w['xdt']] * rep, axis=0), 0.0)
        scores = jnp.concatenate([w['cb']] * rep, axis=1) * w['dec']
        y = _dot(scores, xe) + w['e_acs'] * w['yoff']
        y_refs[w['d']][0, :, w['g'] * gw:(w['g'] + 1) * gw] = y
        h_scr[w['d'], w['g']] = w['ht'] * w['e_tot'] + w['st']


def _ssd_scan_kernel(xf_ref, dtf_ref, xb_ref, dtb_ref, a_ref, h0_ref, yf_ref, yb_ref, hfin_ref,
                     h_scr, *, nc):
    c = pl.program_id(1)

    @pl.when(c == 0)
    def _():
        h_scr[...] = h0_ref[0]

    _ssd_chunks(((xf_ref, dtf_ref), (xb_ref, dtb_ref)), a_ref, (yf_ref, yb_ref), h_scr)

    @pl.when(c == nc - 1)
    def _():
        hfin_ref[0] = h_scr[...]


def _ssd_scan(xbc, dt, a_neg, h_init):
    bsz, length, _ = xbc.shape
    q = SSD_CHUNK
    nc = length // q
    hspec = pl.BlockSpec((1, 2, SSD_GROUPS, SSD_STATE, SSD_GW), lambda b, c: (b, 0, 0, 0, 0))
    oshape = jax.ShapeDtypeStruct((bsz, length, BR_W), F32)
    return pl.pallas_call(
        functools.partial(_ssd_scan_kernel, nc=nc),
        grid=(bsz, nc),
        in_specs=[pl.BlockSpec((1, q, SSD_XBC), lambda b, c: (b, c, 0)),
                  pl.BlockSpec((1, 1, q, LANES), lambda b, c: (b, 0, c, 0)),
                  pl.BlockSpec((1, q, SSD_XBC), lambda b, c: (b, nc - 1 - c, 0)),
                  pl.BlockSpec((1, 1, q, LANES), lambda b, c: (b, 1, nc - 1 - c, 0)),
                  pl.BlockSpec((2, 1, LANES), lambda b, c: (0, 0, 0)),
                  hspec],
        out_specs=[pl.BlockSpec((1, q, BR_W), lambda b, c: (b, c, 0)),
                   pl.BlockSpec((1, q, BR_W), lambda b, c: (b, nc - 1 - c, 0)), hspec],
        out_shape=[oshape, oshape,
                   jax.ShapeDtypeStruct((bsz, 2, SSD_GROUPS, SSD_STATE, SSD_GW), F32)],
        scratch_shapes=[pltpu.VMEM((2, SSD_GROUPS, SSD_STATE, SSD_GW), F32)],
        compiler_params=_cp("parallel", "arbitrary"),
        name="ssd_scan",
    )(xbc, dt, xbc, dt, a_neg, h_init)


def _ssd_readout_kernel(yf_ref, yb_ref, xs_ref, z_ref, dsk_ref, ng_ref, out_ref):
    y = yf_ref[0] + yb_ref[0] + dsk_ref[...] * xs_ref[0]
    y = y * _silu(z_ref[0])
    y = y * lax.rsqrt(jnp.mean(y * y, axis=-1, keepdims=True) + NORM_EPS) * ng_ref[...]
    out_ref[0] = y.astype(out_ref.dtype)


def _ssd_readout(y_f, y_b, xbc, z, lp):
    bsz, length, w = y_f.shape
    tr = _tile(length, 256, 16)
    ospec = pl.BlockSpec((1, tr, w), lambda b, j: (b, j, 0))
    return pl.pallas_call(
        _ssd_readout_kernel,
        grid=(bsz, length // tr),
        in_specs=[ospec, ospec, ospec, ospec,
                  pl.BlockSpec((1, w), lambda b, j: (0, 0)),
                  pl.BlockSpec((1, w), lambda b, j: (0, 0))],
        out_specs=ospec,
        out_shape=jax.ShapeDtypeStruct((bsz, length, w), BF16),
        compiler_params=_cp("parallel", "parallel"),
        name="ssd_readout",
    )(y_f, y_b, xbc, z, lp['ssd_dskip'], lp['ssd_ng'])


def _da_prep_kernel(z_ref, cos_ref, sin_ref, q_ref, k_ref, v_ref, *, rope):
    w = BR_W

    def rot(x):
        if not rope:
            return x
        lane = lax.broadcasted_iota(jnp.int32, (1, LANES), 1)
        first = (lane & 31) < 16
        outs = []
        for cb in range(w // LANES):
            xb = x[:, cb * LANES:(cb + 1) * LANES]
            sw = jnp.where(first, pltpu.roll(xb, LANES - 16, 1), pltpu.roll(xb, 16, 1))
            outs.append(xb * cos_ref[:, cb * LANES:(cb + 1) * LANES]
                        + sw * sin_ref[:, cb * LANES:(cb + 1) * LANES])
        return jnp.concatenate(outs, axis=1)

    q_ref[0] = (rot(z_ref[0, :, 0:w]) * (HD ** -0.5)).astype(q_ref.dtype)
    k_ref[0] = rot(z_ref[0, :, w:2 * w]).astype(k_ref.dtype)
    v_ref[0] = z_ref[0, :, 2 * w:3 * w].astype(v_ref.dtype)


def _da_prep(z, cos, sin, rope):
    bsz, length, fin = z.shape
    tr = _tile(length, 256, 16)
    w = BR_W
    ospec = pl.BlockSpec((1, tr, w), lambda b, j: (b, j, 0))
    oshape = jax.ShapeDtypeStruct((bsz, length, w), BF16)
    return pl.pallas_call(
        functools.partial(_da_prep_kernel, rope=rope),
        grid=(bsz, length // tr),
        in_specs=[pl.BlockSpec((1, tr, fin), lambda b, j: (b, j, 0)),
                  pl.BlockSpec((tr, w), lambda b, j: (j, 0)),
                  pl.BlockSpec((tr, w), lambda b, j: (j, 0))],
        out_specs=[ospec, ospec, ospec],
        out_shape=[oshape, oshape, oshape],
        compiler_params=_cp("parallel", "parallel"),
        name="da_prep",
    )(z, cos, sin)


def _da_attn_kernel(*refs, nsrc, lam_init):
    lam_ref, q_ref = refs[0], refs[1]
    kv = refs[2:2 + 2 * nsrc]
    g_ref, o_ref = refs[2 + 2 * nsrc], refs[3 + 2 * nsrc]
    q = q_ref[0]
    lane = lax.broadcasted_iota(jnp.int32, (1, LANES), 1)
    zero = jnp.zeros_like(q)
    lam = lam_ref[0:1, 0:1]
    ess, scales = [], []
    for m in range(2):
        qm = jnp.where((lane < HD) if m == 0 else (lane >= HD), q, zero)
        ss = [_dot_nt(qm, kv[2 * i][0]) for i in range(nsrc)]
        mx = functools.reduce(jnp.maximum, [jnp.max(s, axis=-1, keepdims=True) for s in ss])
        es = [jnp.exp(s - mx) for s in ss]
        den = functools.reduce(lambda a, b: a + b, [jnp.sum(e, axis=-1, keepdims=True) for e in es])
        ess.append(es)
        scales.append(1.0 / den if m == 0 else lam / den)
    o = None
    for i in range(nsrc):
        a = ess[0][i] * scales[0] - ess[1][i] * scales[1]
        t = _dot(a, kv[2 * i + 1][0])
        o = t if o is None else o + t
    y = o * lax.rsqrt(jnp.mean(o * o, axis=-1, keepdims=True) + DA_SUBLN_EPS) * g_ref[...]
    o_ref[0] = (y * (1.0 - lam_init)).astype(o_ref.dtype)


def _da_attn(q, ks, vs, lam, subln_g, lam_init):
    bsz, lq, _ = q.shape
    tq = _tile(lq, 256, 16)
    nsrc = len(ks)
    in_specs = [pl.BlockSpec((1, LANES), lambda b, h, j: (0, 0)),
                pl.BlockSpec((1, tq, LANES), lambda b, h, j: (b, j, h))]
    args = [lam, q]
    for k, v in zip(ks, vs):
        lk = k.shape[1]
        in_specs.append(pl.BlockSpec((1, lk, LANES), lambda b, h, j: (b, 0, h)))
        in_specs.append(pl.BlockSpec((1, lk, LANES), lambda b, h, j: (b, 0, h)))
        args += [k, v]
    in_specs.append(pl.BlockSpec((1, LANES), lambda b, h, j: (0, 0)))
    args.append(subln_g)
    return pl.pallas_call(
        functools.partial(_da_attn_kernel, nsrc=nsrc, lam_init=lam_init),
        grid=(bsz, DA_HEADS, lq // tq),
        in_specs=in_specs,
        out_specs=pl.BlockSpec((1, tq, LANES), lambda b, h, j: (b, j, h)),
        out_shape=jax.ShapeDtypeStruct((bsz, lq, BR_W), BF16),
        compiler_params=_cp("parallel", "parallel", "parallel"),
        name="diff_attn",
    )(*args)


def _na_kb(i, rows):
    return jnp.clip(i * NA_QROWS - NA_WIN_R // 2, 0, rows - NA_KROWS)


def _na_attn_kernel(q_ref, k_ref, v_ref, kc_ref, vc_ref, bias_ref, o_ref, *, rows):
    i = pl.program_id(2)
    start = pl.multiple_of(_na_kb(i, rows) * GRID_W, GRID_W)
    nk = NA_KROWS * GRID_W
    q = q_ref[0] * (HD ** -0.5)
    kw = k_ref[0, pl.ds(start, nk), :]
    vw = v_ref[0, pl.ds(start, nk), :]
    kc = kc_ref[0]
    vc = vc_ref[0]
    lane = lax.broadcasted_iota(jnp.int32, (1, LANES), 1)
    outs = []
    for h in range(2):
        qm = jnp.where((lane < HD) if h == 0 else (lane >= HD), q, 0.0)
        sw = _dot_nt(qm, kw) + bias_ref[h, 0]
        sc = _dot_nt(qm, kc)
        mx = jnp.maximum(jnp.max(sw, axis=-1, keepdims=True), jnp.max(sc, axis=-1, keepdims=True))
        ew = jnp.exp(sw - mx)
        ec = jnp.exp(sc - mx)
        inv = 1.0 / (jnp.sum(ew, axis=-1, keepdims=True) + jnp.sum(ec, axis=-1, keepdims=True))
        outs.append(_dot(ew * inv, vw) + _dot(ec * inv, vc))
    o_ref[0] = jnp.where(lane < HD, outs[0], outs[1]).astype(o_ref.dtype)


def _na_attn(z_l, z_c, bias, pat_id):
    bsz, length, _ = z_l.shape
    lc = z_c.shape[1]
    rows = length // GRID_W
    tq = NA_QROWS * GRID_W
    nblk = length // tq
    nkb = BR_W // LANES

    def pat(i):
        out = pat_id[0]
        for j in range(1, nblk):
            if pat_id[j] != pat_id[j - 1]:
                out = out + jnp.where(i >= j, pat_id[j] - pat_id[j - 1], 0)
        return out

    return pl.pallas_call(
        functools.partial(_na_attn_kernel, rows=rows),
        grid=(bsz, NA_HEADS // 2, nblk),
        in_specs=[pl.BlockSpec((1, tq, LANES), lambda b, p, i: (b, i, p)),
                  pl.BlockSpec((1, length, LANES), lambda b, p, i: (b, 0, nkb + p)),
                  pl.BlockSpec((1, length, LANES), lambda b, p, i: (b, 0, 2 * nkb + p)),
                  pl.BlockSpec((1, lc, LANES), lambda b, p, i: (b, 0, nkb + p)),
                  pl.BlockSpec((1, lc, LANES), lambda b, p, i: (b, 0, 2 * nkb + p)),
                  pl.BlockSpec((2, 1, tq, NA_KROWS * GRID_W), lambda b, p, i: (p, pat(i), 0, 0))],
        out_specs=pl.BlockSpec((1, tq, LANES), lambda b, p, i: (b, i, p)),
        out_shape=jax.ShapeDtypeStruct((bsz, length, BR_W), BF16),
        compiler_params=_cp("parallel", "parallel", "parallel"),
        name="na_attn",
    )(z_l, z_l, z_l, z_c, z_c, bias)


def _ctx_attn_kernel(q_ref, k_ref, v_ref, o_ref):
    q = q_ref[0] * (HD ** -0.5)
    k = k_ref[0]
    v = v_ref[0]
    lane = lax.broadcasted_iota(jnp.int32, (1, LANES), 1)
    outs = []
    for h in range(2):
        qm = jnp.where((lane < HD) if h == 0 else (lane >= HD), q, 0.0)
        s = _dot_nt(qm, k)
        e = jnp.exp(s - jnp.max(s, axis=-1, keepdims=True))
        outs.append(_dot(e * (1.0 / jnp.sum(e, axis=-1, keepdims=True)), v))
    o_ref[0] = jnp.where(lane < HD, outs[0], outs[1]).astype(o_ref.dtype)


def _ctx_attn(z_c):
    bsz, lc, _ = z_c.shape
    nkb = BR_W // LANES
    return pl.pallas_call(
        _ctx_attn_kernel,
        grid=(bsz, NA_HEADS // 2),
        in_specs=[pl.BlockSpec((1, lc, LANES), lambda b, p: (b, 0, p)),
                  pl.BlockSpec((1, lc, LANES), lambda b, p: (b, 0, nkb + p)),
                  pl.BlockSpec((1, lc, LANES), lambda b, p: (b, 0, 2 * nkb + p))],
        out_specs=pl.BlockSpec((1, lc, LANES), lambda b, p: (b, 0, p)),
        out_shape=jax.ShapeDtypeStruct((bsz, lc, BR_W), BF16),
        compiler_params=_cp("parallel", "parallel"),
        name="ctx_attn",
    )(z_c, z_c, z_c)


def _rope_tables(length):
    n_freq = HD // 4
    t = np.arange(length)
    pos = np.stack([t // GRID_W, t % GRID_W], axis=-1).astype(np.float32)
    inv = (ROPE_BASE ** (-np.arange(n_freq, dtype=np.float32) / n_freq)).astype(np.float32)
    lane = np.arange(BR_W)
    which = (lane % HD) // (HD // 2)
    ang = pos[:, which] * inv[lane % n_freq][None, :]
    sign = np.where((lane % (HD // 2)) < n_freq, -1.0, 1.0).astype(np.float32)
    return jnp.asarray(np.cos(ang), F32), jnp.asarray(np.sin(ang) * sign[None, :], F32)


def _na_bias(rpb, length):
    rows = length // GRID_W
    wr = min(NA_WIN_R, rows)
    nblk = rows // NA_QROWS
    qr = np.arange(rows)
    rstart = np.clip(qr - wr // 2, 0, rows - wr)
    kb = np.clip(np.arange(nblk) * NA_QROWS - NA_WIN_R // 2, 0, rows - NA_KROWS)
    qrow = (np.arange(nblk)[:, None] * NA_QROWS + np.arange(NA_QROWS)[None, :])
    krow = kb[:, None] + np.arange(NA_KROWS)[None, :]
    dr = krow[:, None, :] - qrow[:, :, None] + NA_WIN_R - 1
    rvalid = (krow[:, None, :] >= rstart[qrow][:, :, None]) & (krow[:, None, :] < rstart[qrow][:, :, None] + wr)
    dr = np.where(rvalid, dr, -1)
    pats, pat_id = [], []
    for i in range(nblk):
        for j, pdr in enumerate(pats):
            if np.array_equal(pdr, dr[i]):
                pat_id.append(j)
                break
        else:
            pat_id.append(len(pats))
            pats.append(dr[i])
    pdr = np.stack(pats)
    n_dr = 2 * NA_WIN_R - 1
    oh_dr = (pdr[..., None] == np.arange(n_dr)).astype(np.float32)
    cid = np.arange(GRID_W)
    cstart = np.clip(cid - NA_WIN_C // 2, 0, GRID_W - NA_WIN_C)
    in_win = (cid[None, :] >= cstart[:, None]) & (cid[None, :] < cstart[:, None] + NA_WIN_C)
    ci = np.clip(cid[None, :] - cid[:, None], -(NA_WIN_C - 1), NA_WIN_C - 1) + NA_WIN_C - 1
    n_ci = 2 * NA_WIN_C - 1
    oh_ci = (ci[..., None] == np.arange(n_ci)).astype(np.float32)
    t1 = jnp.einsum('hdc,qkc->hdqk', rpb.astype(F32), jnp.asarray(oh_ci),
                    precision=lax.Precision.HIGHEST)
    big = jnp.einsum('brkd,hdxy->hbrxky', jnp.asarray(oh_dr), t1,
                     precision=lax.Precision.HIGHEST)
    valid = (pdr >= 0)[:, :, None, :, None] & in_win[None, None, :, None, :]
    big = jnp.where(jnp.asarray(valid)[None], big, NEG)
    big = big.reshape(rpb.shape[0], len(pats), NA_QROWS * GRID_W, NA_KROWS * GRID_W)
    return big, tuple(pat_id)


def _block_diag(blocks):
    n = len(blocks)
    r, c = blocks[0].shape
    out = jnp.zeros((n * r, n * c), blocks[0].dtype)
    for i, blk in enumerate(blocks):
        out = out.at[i * r:(i + 1) * r, i * c:(i + 1) * c].set(blk)
    return out


def _layer_params(p, i):
    w = BR_W
    w_in = p['w_in'][i]
    o1, o2, o3 = RW_IN, RW_IN + SSD_IN, RW_IN + SSD_IN + DA_IN
    d = w_in.shape[0]
    w_ssd = jnp.concatenate([w_in[:, o1:o2], jnp.zeros((d, SSD_IN_PAD - SSD_IN), w_in.dtype)], axis=1)
    head_id = np.arange(w) // HD
    ones_hd = jnp.asarray((head_id[:, None] == head_id[None, :]).astype(np.float32), BF16)
    lam_p = p['da_lambda'][i].astype(F32)
    lam_init = 0.8 - 0.6 * math.exp(-0.3 * i)
    lam = jnp.exp(jnp.sum(lam_p[0] * lam_p[1])) - jnp.exp(jnp.sum(lam_p[2] * lam_p[3])) + lam_init
    dtb = p['ssd_dt_bias'][i]
    return {
        'w_rw': w_in[:, :o1].astype(BF16), 'w_ssd': w_ssd.astype(BF16),
        'w_da': w_in[:, o2:o3].astype(BF16), 'w_na': w_in[:, o3:].astype(BF16),
        'rw_mu': p['rw_mu'][i], 'rw_w0': p['rw_w0'][i],
        'rw_wup': _block_diag([p['rw_w_up'][i, 0], p['rw_w_up'][i, 1]]).astype(BF16),
        'rw_a0': p['rw_a0'][i],
        'rw_aup': _block_diag([p['rw_a_up'][i, 0], p['rw_a_up'][i, 1]]).astype(BF16),
        'rw_gup': p['rw_g_up'][i].astype(BF16),
        'rw_kk': p['rw_k_k'][i].reshape(1, w), 'rw_ka': p['rw_k_a'][i].reshape(1, w),
        'rw_rk': p['rw_r_k'][i].reshape(1, w),
        'rw_lng': p['rw_ln_g'][i].reshape(1, w), 'rw_lnb': p['rw_ln_b'][i].reshape(1, w),
        'ones_hd': ones_hd, 'mean_hd': (ones_hd.astype(F32) / HD).astype(BF16),
        'ssd_cw': p['ssd_conv_w'][i], 'ssd_cb': p['ssd_conv_b'][i].reshape(1, SSD_XBC),
        'ssd_dtb': jnp.pad(dtb, ((0, 0), (0, LANES - SSD_HEADS))),
        'ssd_a': jnp.pad(-jnp.exp(p['ssd_a_log'][i].astype(F32)), ((0, 0), (0, LANES - SSD_HEADS))).reshape(2, 1, LANES),
        'ssd_dskip': jnp.repeat(p['ssd_d'][i], HD).reshape(1, w),
        'ssd_ng': p['ssd_norm_g'][i].reshape(1, w),
        'da_lam': jnp.broadcast_to(lam.reshape(1, 1), (1, LANES)).astype(F32),
        'da_lam_init': lam_init,
        'da_g': p['da_subln_g'][i].reshape(1, 2 * HD),
        'na_rpb': p['na_rpb'][i],
        'w_gate': p['w_gate'][i].astype(BF16),
        'gate_b': p['gate_b'][i][:, None, :],
        'w_br': p['w_br'][i].astype(BF16), 'w_out': p['w_out'][i].astype(BF16),
        'ffn_up': p['ffn_up'][i].astype(BF16), 'ffn_cw': p['ffn_conv_w'][i],
        'ffn_cb': p['ffn_conv_b'][i], 'ffn_down': p['ffn_down'][i].astype(BF16),
    }


def _project(h, lp):
    bsz, length, d = h.shape
    h2 = h.reshape(bsz * length, d)
    return tuple(_matmul(h2, lp[n], F32).reshape(bsz, length, -1)
                 for n in ('w_rw', 'w_ssd', 'w_da', 'w_na'))


def _ffn(x, h2, lp, gate):
    act = _ffn_up(h2, lp['ffn_up'], lp['ffn_cw'], lp['ffn_cb'])
    return _matmul_residual(act, lp['ffn_down'], x, gate)


def kernel(x, c, ctx, c_ctx, ada_w, ada_b, norm1_g, norm2_g, w_in, rw_mu, rw_w0, rw_w_up, rw_a0, rw_a_up, rw_g_up, rw_k_k, rw_k_a, rw_r_k, rw_ln_g, rw_ln_b, ssd_conv_w, ssd_conv_b, ssd_dt_bias, ssd_a_log, ssd_d, ssd_norm_g, da_lambda, da_subln_g, na_rpb, w_gate, gate_b, w_br, w_out, ffn_up, ffn_conv_w, ffn_conv_b, ffn_down, final_norm_g):
    p = dict(w_in=w_in, rw_mu=rw_mu, rw_w0=rw_w0, rw_w_up=rw_w_up, rw_a0=rw_a0, rw_a_up=rw_a_up,
             rw_g_up=rw_g_up, rw_k_k=rw_k_k, rw_k_a=rw_k_a, rw_r_k=rw_r_k, rw_ln_g=rw_ln_g,
             rw_ln_b=rw_ln_b, ssd_conv_w=ssd_conv_w, ssd_conv_b=ssd_conv_b, ssd_dt_bias=ssd_dt_bias,
             ssd_a_log=ssd_a_log, ssd_d=ssd_d, ssd_norm_g=ssd_norm_g, da_lambda=da_lambda,
             da_subln_g=da_subln_g, na_rpb=na_rpb, w_gate=w_gate, gate_b=gate_b, w_br=w_br,
             w_out=w_out, ffn_up=ffn_up, ffn_conv_w=ffn_conv_w, ffn_conv_b=ffn_conv_b,
             ffn_down=ffn_down)
    bsz, seq, d = x.shape
    depth = ada_w.shape[0]
    lctx = ctx.shape[1]
    mrows = -(-(bsz + 1) // 16) * 16
    cc = jnp.zeros((mrows, d), F32).at[:bsz].set(c).at[bsz].set(c_ctx)
    cos, sin = _rope_tables(seq)
    xl, xc = x, ctx
    for i in range(depth):
        last = i == depth - 1
        lp = _layer_params(p, i)
        mod = _modulation(cc, ada_w[i], ada_b[i]).reshape(mrows, 6, d)
        ml = [mod[:bsz, n][:, None, :] for n in range(6)]
        mc = [mod[bsz:bsz + 1, n][:, None, :] for n in range(6)]

        hl = _norm_mod(xl, norm1_g[i], ml[0], ml[1])
        hc = _norm_mod(xc, norm1_g[i], mc[0], mc[1])
        zl = _project(hl, lp)
        zc = _project(hc, lp)

        rp_c = _rw_prep(zc[0], lp)
        rp_l = _rw_prep(zl[0], lp)
        s_zero = jnp.zeros((bsz, 2, RW_HEADS // RW_GROUP, RW_GW, RW_GW), F32)
        of_c, ob_c, s_c = _rw_scan(rp_c, s_zero)
        of_l, ob_l, _ = _rw_scan(rp_l, s_c)
        a_l = _rw_readout(of_l, ob_l, rp_l, lp)

        xbc_c, dt_c = _ssd_prep(zc[1], lp)
        xbc_l, dt_l = _ssd_prep(zl[1], lp)
        h_zero = jnp.zeros((bsz, 2, SSD_GROUPS, SSD_STATE, SSD_GW), F32)
        yf_c, yb_c, hfin_c = _ssd_scan(xbc_c, dt_c, lp['ssd_a'], h_zero)
        yf_l, yb_l, _ = _ssd_scan(xbc_l, dt_l, lp['ssd_a'], hfin_c)
        b_l = _ssd_readout(yf_l, yb_l, xbc_l, zl[1], lp)

        q_l, k_l, v_l = _da_prep(zl[2], cos, sin, True)
        q_c, k_c, v_c = _da_prep(zc[2], cos, sin, False)
        c_l = _da_attn(q_l, [k_l, k_c], [v_l, v_c], lp['da_lam'], lp['da_g'], lp['da_lam_init'])

        d_l = _na_attn(zl[3], zc[3], *_na_bias(lp['na_rpb'], seq))

        merged = _gated_merge(hl.reshape(bsz * seq, d),
                              [t.reshape(bsz * seq, BR_W) for t in (a_l, b_l, c_l, d_l)],
                              lp['w_gate'], lp['gate_b'], lp['w_br']).reshape(bsz, seq, d)
        xl = _matmul_residual(merged, lp['w_out'], xl, ml[2])
        hl2 = _norm_mod(xl, norm2_g[i], ml[3], ml[4])
        xl = _ffn(xl, hl2, lp, ml[5])

        if not last:
            a_c = _rw_readout(of_c, ob_c, rp_c, lp)
            b_c = _ssd_readout(yf_c, yb_c, xbc_c, zc[1], lp)
            c_c = _da_attn(q_c, [k_c], [v_c], lp['da_lam'], lp['da_g'], lp['da_lam_init'])
            d_c = _ctx_attn(zc[3])
            merged_c = _gated_merge(hc.reshape(bsz * lctx, d),
                                    [t.reshape(bsz * lctx, BR_W) for t in (a_c, b_c, c_c, d_c)],
                                    lp['w_gate'], lp['gate_b'], lp['w_br']).reshape(bsz, lctx, d)
            xc = _matmul_residual(merged_c, lp['w_out'], xc, mc[2])
            hc2 = _norm_mod(xc, norm2_g[i], mc[3], mc[4])
            xc = _ffn(xc, hc2, lp, mc[5])
    return _final_norm(xl, final_norm_g)
```

```python
import functools
import math

import numpy as np
import jax
import jax.numpy as jnp
from jax import lax
from jax.experimental import pallas as pl
from jax.experimental.pallas import tpu as pltpu

F32 = jnp.float32
BF16 = jnp.bfloat16

GRID_W = 64
NORM_EPS = 1e-6
HD = 64
BR_W = 512
RW_HEADS = 8
RW_RANK = 64
RW_GATE_RANK = 128
RW_GN_EPS = 64e-5
RW_IN = 3 * BR_W + 4 * RW_RANK + RW_GATE_RANK
RW_CHUNK = 64
RW_NARR = 11
RW_GROUP = 4
RW_GW = RW_GROUP * HD
RW_BATCH = 2
SSD_HEADS = 8
SSD_GROUPS = 2
SSD_STATE = 128
SSD_XBC = BR_W + 2 * SSD_GROUPS * SSD_STATE
SSD_IN = BR_W + SSD_XBC + 2 * SSD_HEADS
SSD_IN_PAD = BR_W + SSD_XBC + 128
SSD_CHUNK = 128
SSD_GW = SSD_HEADS // SSD_GROUPS * HD
DA_HEADS = 4
DA_IN = 3 * BR_W
DA_SUBLN_EPS = 1e-5
NA_HEADS = 8
NA_IN = 3 * BR_W
NA_WIN_R = 8
NA_WIN_C = 16
NA_QROWS = 4
NA_KROWS = 12
ROPE_BASE = 10000.0
NEG = -1e30
LANES = 128
VMEM_LIMIT = 56 * 1024 * 1024


def _cp(*sem):
    return pltpu.CompilerParams(dimension_semantics=sem, vmem_limit_bytes=VMEM_LIMIT)


def _tile(n, pref, mult):
    t = min(n, pref)
    t -= t % mult
    while t >= mult:
        if n % t == 0:
            return t
        t -= mult
    return n


def _dot(a, b):
    return jnp.dot(a.astype(BF16), b.astype(BF16), preferred_element_type=F32)


def _dot_nt(a, b):
    return lax.dot_general(a.astype(BF16), b.astype(BF16), (((1,), (1,)), ((), ())),
                           preferred_element_type=F32)


def _dot_tn(a, b):
    return lax.dot_general(a.astype(BF16), b.astype(BF16), (((0,), (0,)), ((), ())),
                           preferred_element_type=F32)


def _split3(x):
    hi = x.astype(BF16)
    r1 = x - hi.astype(F32)
    mid = r1.astype(BF16)
    lo = (r1 - mid.astype(F32)).astype(BF16)
    return hi, mid, lo


def _dot01(m01, x):
    hi, mid, lo = _split3(x)
    return (jnp.dot(m01, hi, preferred_element_type=F32)
            + jnp.dot(m01, mid, preferred_element_type=F32)
            + jnp.dot(m01, lo, preferred_element_type=F32))


def _x_dot01(x, m01):
    hi, mid, lo = _split3(x)
    return (jnp.dot(hi, m01, preferred_element_type=F32)
            + jnp.dot(mid, m01, preferred_element_type=F32)
            + jnp.dot(lo, m01, preferred_element_type=F32))


def _sigmoid(x):
    return 1.0 / (1.0 + jnp.exp(-x))


def _silu(x):
    return x * _sigmoid(x)


def _softplus(x):
    return jnp.maximum(x, 0.0) + jnp.log(1.0 + jnp.exp(-jnp.abs(x)))


def _mod_kernel(c_ref, w_ref, b_ref, o_ref):
    a = _silu(c_ref[...])
    o_ref[...] = _dot(a, w_ref[...]) + b_ref[...]


def _modulation(cc, w, b):
    m, d = cc.shape
    n = w.shape[1]
    tn = _tile(n, 1024, LANES)
    return pl.pallas_call(
        _mod_kernel,
        grid=(n // tn,),
        in_specs=[pl.BlockSpec((m, d), lambda j: (0, 0)),
                  pl.BlockSpec((d, tn), lambda j: (0, j)),
                  pl.BlockSpec((1, tn), lambda j: (0, j))],
        out_specs=pl.BlockSpec((m, tn), lambda j: (0, j)),
        out_shape=jax.ShapeDtypeStruct((m, n), F32),
        compiler_params=_cp("parallel"),
        name="adaln_mod",
    )(cc, w, b.reshape(1, n))


def _norm_mod_kernel(x_ref, g_ref, sh_ref, sc_ref, o_ref):
    x = x_ref[0]
    y = x * lax.rsqrt(jnp.mean(x * x, axis=-1, keepdims=True) + NORM_EPS) * g_ref[...]
    o_ref[0] = (y * (1.0 + sc_ref[0]) + sh_ref[0]).astype(o_ref.dtype)


def _norm_mod(x, g, shift, scale):
    bsz, length, d = x.shape
    tr = _tile(length, 256, 16)
    bm = shift.shape[0]
    mod_map = (lambda b, j: (b, 0, 0)) if bm == bsz else (lambda b, j: (0, 0, 0))
    return pl.pallas_call(
        _norm_mod_kernel,
        grid=(bsz, length // tr),
        in_specs=[pl.BlockSpec((1, tr, d), lambda b, j: (b, j, 0)),
                  pl.BlockSpec((1, d), lambda b, j: (0, 0)),
                  pl.BlockSpec((1, 1, d), mod_map),
                  pl.BlockSpec((1, 1, d), mod_map)],
        out_specs=pl.BlockSpec((1, tr, d), lambda b, j: (b, j, 0)),
        out_shape=jax.ShapeDtypeStruct((bsz, length, d), BF16),
        compiler_params=_cp("parallel", "parallel"),
        name="norm_mod",
    )(x, g.reshape(1, d), shift, scale)


def _final_norm_kernel(x_ref, g_ref, o_ref):
    x = x_ref[0]
    o_ref[0] = x * lax.rsqrt(jnp.mean(x * x, axis=-1, keepdims=True) + NORM_EPS) * g_ref[...]


def _final_norm(x, g):
    bsz, length, d = x.shape
    tr = _tile(length, 256, 8)
    return pl.pallas_call(
        _final_norm_kernel,
        grid=(bsz, length // tr),
        in_specs=[pl.BlockSpec((1, tr, d), lambda b, j: (b, j, 0)),
                  pl.BlockSpec((1, d), lambda b, j: (0, 0))],
        out_specs=pl.BlockSpec((1, tr, d), lambda b, j: (b, j, 0)),
        out_shape=jax.ShapeDtypeStruct((bsz, length, d), F32),
        compiler_params=_cp("parallel", "parallel"),
        name="final_norm",
    )(x, g.reshape(1, d))


def _mm_kernel(a_ref, w_ref, o_ref):
    o_ref[...] = jnp.dot(a_ref[...], w_ref[...], preferred_element_type=F32).astype(o_ref.dtype)


def _matmul(a, w, out_dtype, tm_pref=512, tn_pref=2048):
    m, k = a.shape
    n = w.shape[1]
    tm = _tile(m, tm_pref, 16)
    tn = _tile(n, tn_pref, LANES)
    return pl.pallas_call(
        _mm_kernel,
        grid=(m // tm, n // tn),
        in_specs=[pl.BlockSpec((tm, k), lambda i, j: (i, 0)),
                  pl.BlockSpec((k, tn), lambda i, j: (0, j))],
        out_specs=pl.BlockSpec((tm, tn), lambda i, j: (i, j)),
        out_shape=jax.ShapeDtypeStruct((m, n), out_dtype),
        compiler_params=_cp("parallel", "parallel"),
        name="matmul",
    )(a, w)


def _mm_res_kernel(a_ref, w_ref, x_ref, g_ref, o_ref):
    y = jnp.dot(a_ref[0], w_ref[...], preferred_element_type=F32)
    o_ref[0] = x_ref[0] + g_ref[0] * y


def _matmul_residual(a, w, x, gate):
    bsz, length, k = a.shape
    n = w.shape[1]
    tm = _tile(length, 1024, 16)
    tn = _tile(n, 512, LANES)
    bm = gate.shape[0]
    gmap = (lambda b, i, j: (b, 0, j)) if bm == bsz else (lambda b, i, j: (0, 0, j))
    return pl.pallas_call(
        _mm_res_kernel,
        grid=(bsz, length // tm, n // tn),
        in_specs=[pl.BlockSpec((1, tm, k), lambda b, i, j: (b, i, 0)),
                  pl.BlockSpec((k, tn), lambda b, i, j: (0, j)),
                  pl.BlockSpec((1, tm, tn), lambda b, i, j: (b, i, j)),
                  pl.BlockSpec((1, 1, tn), gmap)],
        out_specs=pl.BlockSpec((1, tm, tn), lambda b, i, j: (b, i, j)),
        out_shape=jax.ShapeDtypeStruct((bsz, length, n), F32),
        compiler_params=_cp("parallel", "parallel", "parallel"),
        name="matmul_residual",
    )(a, w, x, gate)


def _mm_res_norm_kernel(a_ref, w_ref, x_ref, g_ref, ng_ref, sh_ref, sc_ref, xo_ref, h_ref):
    xn = x_ref[0] + g_ref[0] * jnp.dot(a_ref[0], w_ref[...], preferred_element_type=F32)
    xo_ref[0] = xn
    y = xn * lax.rsqrt(jnp.mean(xn * xn, axis=-1, keepdims=True) + NORM_EPS) * ng_ref[...]
    h_ref[0] = (y * (1.0 + sc_ref[0]) + sh_ref[0]).astype(h_ref.dtype)


def _matmul_residual_norm(a, w, x, gate, norm_g, shift, scale):
    bsz, length, k = a.shape
    n = w.shape[1]
    tm = _tile(length, 512, 16)
    bm = gate.shape[0]
    mmap = (lambda b, i: (b, 0, 0)) if bm == bsz else (lambda b, i: (0, 0, 0))
    row = pl.BlockSpec((1, tm, n), lambda b, i: (b, i, 0))
    mod = pl.BlockSpec((1, 1, n), mmap)
    return pl.pallas_call(
        _mm_res_norm_kernel,
        grid=(bsz, length // tm),
        in_specs=[pl.BlockSpec((1, tm, k), lambda b, i: (b, i, 0)),
                  pl.BlockSpec((k, n), lambda b, i: (0, 0)),
                  row, mod, pl.BlockSpec((1, n), lambda b, i: (0, 0)), mod, mod],
        out_specs=[row, row],
        out_shape=[jax.ShapeDtypeStruct((bsz, length, n), F32),
                   jax.ShapeDtypeStruct((bsz, length, n), BF16)],
        compiler_params=_cp("parallel", "parallel"),
        name="outproj_norm",
    )(a, w, x, gate, norm_g.reshape(1, n), shift, scale)


def _merge_kernel(h_ref, o0_ref, o1_ref, o2_ref, o3_ref, wg_ref, gb_ref, wbr_ref, out_ref):
    h = h_ref[...]
    acc = None
    for n, o_ref in enumerate((o0_ref, o1_ref, o2_ref, o3_ref)):
        gate = _sigmoid(jnp.dot(h, wg_ref[n], preferred_element_type=F32) + gb_ref[n])
        term = gate * jnp.dot(o_ref[...], wbr_ref[n], preferred_element_type=F32)
        acc = term if acc is None else acc + term
    out_ref[...] = acc.astype(out_ref.dtype)


def _gated_merge(h, branches, wg, gb, wbr):
    m, d = h.shape
    nb = len(branches)
    tm = _tile(m, 1024, 16)
    tn = _tile(d, 256, LANES)
    bspec = pl.BlockSpec((tm, BR_W), lambda i, j: (i, 0))
    return pl.pallas_call(
        _merge_kernel,
        grid=(m // tm, d // tn),
        in_specs=[pl.BlockSpec((tm, d), lambda i, j: (i, 0)), bspec, bspec, bspec, bspec,
                  pl.BlockSpec((nb, d, tn), lambda i, j: (0, 0, j)),
                  pl.BlockSpec((nb, 1, tn), lambda i, j: (0, 0, j)),
                  pl.BlockSpec((nb, BR_W, tn), lambda i, j: (0, 0, j))],
        out_specs=pl.BlockSpec((tm, tn), lambda i, j: (i, j)),
        out_shape=jax.ShapeDtypeStruct((m, d), BF16),
        compiler_params=_cp("parallel", "parallel"),
        name="gated_merge",
    )(h, *branches, wg, gb, wbr)


def _conv3(x, prev_row, next_row, w_ref, b_ref):
    rows = x.shape[0]
    rid = lax.broadcasted_iota(jnp.int32, x.shape, 0)
    prev = jnp.where(rid == 0, prev_row, pltpu.roll(x, 1, 0))
    nxt = jnp.where(rid == rows - 1, next_row, pltpu.roll(x, rows - 1, 0))
    return b_ref[...] + prev * w_ref[0:1, :] + x * w_ref[1:2, :] + nxt * w_ref[2:3, :]


def _ffn_up_kernel(a_ref, ap_ref, an_ref, wg_ref, wv_ref, cwg_ref, cbg_ref, cwv_ref, cbv_ref,
                   o_ref, *, nt, halo):
    i = pl.program_id(1)
    tm = a_ref.shape[1]
    a_ext = jnp.concatenate([ap_ref[0], a_ref[0], an_ref[0]], axis=0)
    n_ext = tm + 2 * halo
    rid = lax.broadcasted_iota(jnp.int32, (tm, 1), 0)
    no_prev = jnp.logical_and(i == 0, rid == 0)
    no_next = jnp.logical_and(i == nt - 1, rid == tm - 1)

    def branch(w_ref, cw_ref, cb_ref):
        u = jnp.dot(a_ext, w_ref[...], preferred_element_type=F32)
        um1 = jnp.where(no_prev, 0.0, pltpu.roll(u, 1, 0)[halo:halo + tm])
        up1 = jnp.where(no_next, 0.0, pltpu.roll(u, n_ext - 1, 0)[halo:halo + tm])
        return (cb_ref[...] + um1 * cw_ref[0:1, :] + u[halo:halo + tm] * cw_ref[1:2, :]
                + up1 * cw_ref[2:3, :])

    gate = branch(wg_ref, cwg_ref, cbg_ref)
    val = branch(wv_ref, cwv_ref, cbv_ref)
    o_ref[0] = (_silu(gate) * val).astype(o_ref.dtype)


def _ffn_up(h, w_up, conv_w, conv_b):
    bsz, length, d = h.shape
    f2 = w_up.shape[1]
    f = f2 // 2
    halo = 16
    tm = _tile(length, 1024, halo)
    tc = _tile(f, 512, LANES)
    nt = length // tm
    nc = f // tc
    nh = length // halo
    rb = tm // halo

    def wspec(rows, off):
        return pl.BlockSpec((rows, tc), lambda b, i, c: (0, c + off))

    cb = conv_b.reshape(1, f2)
    return pl.pallas_call(
        functools.partial(_ffn_up_kernel, nt=nt, halo=halo),
        grid=(bsz, nt, nc),
        in_specs=[pl.BlockSpec((1, tm, d), lambda b, i, c: (b, i, 0)),
                  pl.BlockSpec((1, halo, d), lambda b, i, c: (b, jnp.maximum(i * rb - 1, 0), 0)),
                  pl.BlockSpec((1, halo, d), lambda b, i, c: (b, jnp.minimum((i + 1) * rb, nh - 1), 0)),
                  wspec(d, 0), wspec(d, nc), wspec(3, 0), wspec(1, 0), wspec(3, nc), wspec(1, nc)],
        out_specs=pl.BlockSpec((1, tm, tc), lambda b, i, c: (b, i, c)),
        out_shape=jax.ShapeDtypeStruct((bsz, length, f), BF16),
        compiler_params=_cp("parallel", "parallel", "parallel"),
        name="ffn_up",
    )(h, h, h, w_up, w_up, conv_w, cb, conv_w, cb)


def _rw_prep_kernel(z_ref, zp_ref, zn_ref, mu_ref, w0_ref, wup_ref, a0_ref, aup_ref, gup_ref,
                    kk_ref, ka_ref, rk_ref, ones_ref, o_ref, *, nt):
    j = pl.program_id(1)
    p = z_ref[0]
    rows = p.shape[0]
    rid = lax.broadcasted_iota(jnp.int32, p.shape, 0)
    prow = jnp.where(j > 0, zp_ref[0, 7:8, :], 0.0)
    nrow = jnp.where(j < nt - 1, zn_ref[0, 0:1, :], 0.0)
    prev = jnp.where(rid == 0, prow, pltpu.roll(p, 1, 0))
    nxt = jnp.where(rid == rows - 1, nrow, pltpu.roll(p, rows - 1, 0))
    ps = p + mu_ref[0:1, :] * (prev - p) + mu_ref[1:2, :] * (nxt - p)

    w = BR_W
    r = ps[:, 0:w]
    k = ps[:, w:2 * w]
    v = ps[:, 2 * w:3 * w]
    wd = ps[:, 3 * w:3 * w + 128]
    ad = ps[:, 3 * w + 128:3 * w + 256]
    gd = ps[:, 3 * w + 256:3 * w + 384]

    wraw = _dot(jnp.tanh(wd), wup_ref[...])
    araw = _dot(ad, aup_ref[...])
    g = _dot(_sigmoid(gd), gup_ref[...])
    ones = ones_ref[...]
    kkv = k * kk_ref[...]
    ss = _x_dot01(kkv * kkv, ones)
    kkn = kkv / jnp.maximum(jnp.sqrt(ss), 1e-12)

    def put(i, x):
        o_ref[0, :, i * w:(i + 1) * w] = x

    put(0, r)
    put(1, v)
    put(2, kkn)
    ksum = None
    for d in range(2):
        lw = -math.exp(-0.5) * _sigmoid(w0_ref[d:d + 1, :] + wraw[:, d * w:(d + 1) * w])
        a = _sigmoid(a0_ref[d:d + 1, :] + araw[:, d * w:(d + 1) * w])
        kd = k * (1.0 + (a - 1.0) * ka_ref[...])
        put(3 + 3 * d, lw)
        put(4 + 3 * d, kd)
        put(5 + 3 * d, kkn * a)
        ksum = kd if ksum is None else ksum + kd
    put(9, g)
    put(10, _x_dot01(r * rk_ref[...] * ksum, ones) * v)


def _rw_prep(z, lp):
    bsz, length, fin = z.shape
    tr = _tile(length, 256, 8)
    nt = length // tr
    rb = tr // 8
    nh = length // 8
    full = lambda a: pl.BlockSpec(a.shape, lambda b, j: (0,) * a.ndim)
    params = (lp['rw_mu'], lp['rw_w0'], lp['rw_wup'], lp['rw_a0'], lp['rw_aup'], lp['rw_gup'],
              lp['rw_kk'], lp['rw_ka'], lp['rw_rk'], lp['ones_hd'])
    return pl.pallas_call(
        functools.partial(_rw_prep_kernel, nt=nt),
        grid=(bsz, nt),
        in_specs=[pl.BlockSpec((1, tr, fin), lambda b, j: (b, j, 0)),
                  pl.BlockSpec((1, 8, fin), lambda b, j: (b, jnp.maximum(j * rb - 1, 0), 0)),
                  pl.BlockSpec((1, 8, fin), lambda b, j: (b, jnp.minimum((j + 1) * rb, nh - 1), 0))]
                 + [full(a) for a in params],
        out_specs=pl.BlockSpec((1, tr, RW_NARR * BR_W), lambda b, j: (b, j, 0)),
        out_shape=jax.ShapeDtypeStruct((bsz, length, RW_NARR * BR_W), F32),
        compiler_params=_cp("parallel", "parallel"),
        name="rwkv_prep",
    )(z, z, z, *params)


def _rw_chunks(dir_refs, o_refs, s_scr, nb):
    cs = RW_CHUNK
    gw = RW_GW
    ng = RW_HEADS // RW_GROUP
    ri = lax.broadcasted_iota(jnp.int32, (gw, gw), 0)
    ci = lax.broadcasted_iota(jnp.int32, (gw, gw), 1)
    head_mask = (ri >> 6) == (ci >> 6)
    tw = lax.broadcasted_iota(jnp.int32, (cs, gw), 0)
    iw = lax.broadcasted_iota(jnp.int32, (cs, gw), 1) & (cs - 1)
    eye = jnp.where(iw == tw, 1.0, 0.0)
    same = {s: (tw >> int(math.log2(s))) == (iw >> int(math.log2(s))) for s in (8, 16, 32)}
    ti = lax.broadcasted_iota(jnp.int32, (cs, cs), 0)
    ii = lax.broadcasted_iota(jnp.int32, (cs, cs), 1)

    def expand(x):
        xb = x.astype(BF16)
        return jnp.where(head_mask, jnp.concatenate([xb] * RW_GROUP, axis=0), jnp.zeros_like(xb[0:1, 0:1]))

    def mm(a, b):
        return _dot(a, expand(b))

    chains = []
    for bb in range(nb):
        for d, (r_ref, v_ref, kk_ref, lw_ref, k_ref, b_ref) in enumerate(dir_refs):
            sgn = 1 - 2 * d
            rel = (iw - tw) * sgn
            tri = jnp.where((ii - ti) * sgn <= 0, 1.0, 0.0).astype(BF16)
            lw = lw_ref[bb]
            cum = _dot01(tri, lw)
            tot = jnp.sum(lw, axis=0, keepdims=True)
            qt = kk_ref[bb] * jnp.exp(cum - lw)
            rt = r_ref[bb] * jnp.exp(cum)
            e_neg = jnp.exp(-cum)
            e_end = jnp.exp(tot - cum)
            kt = k_ref[bb] * e_neg
            bt = b_ref[bb] * e_neg
            kh = k_ref[bb] * e_end
            bh = b_ref[bb] * e_end
            p_tot = jnp.exp(tot)
            v = v_ref[bb]
            for g in range(ng):
                sl = slice(g * gw, (g + 1) * gw)
                chains.append(dict(bb=bb, d=d, g=g, sl=sl, strict=rel < 0, incl=rel <= 0,
                                   q=qt[:, sl], r=rt[:, sl], v=v[:, sl], kt=kt[:, sl],
                                   bt=bt[:, sl], kh=kh[:, sl], bh=bh[:, sl], p_tot=p_tot[:, sl],
                                   s0=s_scr[bb, d, g]))

    for c in chains:
        c['kte'], c['bte'], c['ve'] = expand(c['kt']), expand(c['bt']), expand(c['v'])
        c['s0b'] = c['s0'].astype(BF16)
    for c in chains:
        c['lmat'] = jnp.where(c['strict'], _dot_nt(c['q'], c['bte']), 0.0)
        c['ld'] = jnp.where(same[8], c['lmat'], 0.0)
    for c in chains:
        c['ld2'] = mm(c['ld'], c['ld'])
    for c in chains:
        c['a_qk'] = jnp.where(c['strict'], _dot_nt(c['q'], c['kte']), 0.0)
    for c in chains:
        c['ld4'] = mm(c['ld2'], c['ld2'])
    for c in chains:
        c['x'] = mm(eye - c['ld'], eye + c['ld2'])
    for c in chains:
        c['a_rk'] = jnp.where(c['incl'], _dot_nt(c['r'], c['kte']), 0.0)
    for c in chains:
        c['x'] = mm(c['x'], eye + c['ld4'])
    for c in chains:
        c['a_rb'] = jnp.where(c['incl'], _dot_nt(c['r'], c['bte']), 0.0)
    for c in chains:
        c['rhs'] = _dot_nt(c['q'], c['s0b']) + _dot(c['a_qk'], c['ve'])
    for s in (8, 16, 32):
        off_mask = jnp.logical_not(same[s])
        if 2 * s < cs:
            off_mask = jnp.logical_and(same[2 * s], off_mask)
        for c in chains:
            c['t'] = mm(c['x'], jnp.where(off_mask, c['lmat'], 0.0))
        if s == 8:
            for c in chains:
                c['o'] = _dot_nt(c['r'], c['s0b']) + _dot(c['a_rk'], c['ve'])
        for c in chains:
            c['x'] = c['x'] - mm(c['t'], c['x'])
    for c in chains:
        c['sa'] = mm(c['x'], c['rhs'])
    for c in chains:
        o_refs[c['d']][c['bb'], :, c['sl']] = c['o'] - mm(c['a_rb'], c['sa'])
    for c in chains:
        upd = _dot_tn(jnp.concatenate([c['v'], c['sa']], axis=0),
                      jnp.concatenate([c['kh'], -c['bh']], axis=0))
        s_scr[c['bb'], c['d'], c['g']] = c['s0'] * c['p_tot'] + jnp.where(head_mask, upd, 0.0)


def _rw_scan_kernel(*refs, nc, nb):
    fwd, bwd = refs[0:6], refs[6:12]
    s0_ref, of_ref, ob_ref, sfin_ref, s_scr = refs[12:]
    c = pl.program_id(1)

    @pl.when(c == 0)
    def _():
        s_scr[...] = s0_ref[...]

    _rw_chunks((fwd, bwd), (of_ref, ob_ref), s_scr, nb)

    @pl.when(c == nc - 1)
    def _():
        sfin_ref[...] = s_scr[...]


def _rw_scan(rwp, s_init):
    bsz, length, _ = rwp.shape
    cs = RW_CHUNK
    nc = length // cs
    ng = RW_HEADS // RW_GROUP
    nb = RW_BATCH if bsz % RW_BATCH == 0 else 1

    def arr(d, i):
        if d == 0:
            return pl.BlockSpec((nb, cs, BR_W), lambda b, c: (b, c, i))
        return pl.BlockSpec((nb, cs, BR_W), lambda b, c: (b, nc - 1 - c, i))

    def specs(d):
        return [arr(d, 0), arr(d, 1), arr(d, 2)] + [arr(d, 3 + 3 * d + i) for i in range(3)]

    sspec = pl.BlockSpec((nb, 2, ng, RW_GW, RW_GW), lambda b, c: (b, 0, 0, 0, 0))
    oshape = jax.ShapeDtypeStruct((bsz, length, BR_W), F32)
    return pl.pallas_call(
        functools.partial(_rw_scan_kernel, nc=nc, nb=nb),
        grid=(bsz // nb, nc),
        in_specs=specs(0) + specs(1) + [sspec],
        out_specs=[pl.BlockSpec((nb, cs, BR_W), lambda b, c: (b, c, 0)),
                   pl.BlockSpec((nb, cs, BR_W), lambda b, c: (b, nc - 1 - c, 0)), sspec],
        out_shape=[oshape, oshape, jax.ShapeDtypeStruct((bsz, 2, ng, RW_GW, RW_GW), F32)],
        scratch_shapes=[pltpu.VMEM((nb, 2, ng, RW_GW, RW_GW), F32)],
        compiler_params=_cp("parallel", "arbitrary"),
        name="rwkv_scan",
    )(*([rwp] * 12), s_init)


def _rw_readout_kernel(of_ref, ob_ref, g_ref, bonus_ref, lng_ref, lnb_ref, mean_ref, out_ref):
    o = of_ref[0] + ob_ref[0]
    mean_m = mean_ref[...]
    mu = _x_dot01(o, mean_m)
    dlt = o - mu
    var = _x_dot01(dlt * dlt, mean_m)
    on = dlt * lax.rsqrt(var + RW_GN_EPS) * lng_ref[...] + lnb_ref[...]
    out_ref[0] = ((on + bonus_ref[0]) * g_ref[0]).astype(out_ref.dtype)


def _rw_readout(o_f, o_b, rwp, lp):
    bsz, length, w = o_f.shape
    tr = _tile(length, 256, 16)
    ospec = pl.BlockSpec((1, tr, w), lambda b, j: (b, j, 0))
    return pl.pallas_call(
        _rw_readout_kernel,
        grid=(bsz, length // tr),
        in_specs=[ospec, ospec,
                  pl.BlockSpec((1, tr, w), lambda b, j: (b, j, 9)),
                  pl.BlockSpec((1, tr, w), lambda b, j: (b, j, 10)),
                  pl.BlockSpec((1, w), lambda b, j: (0, 0)),
                  pl.BlockSpec((1, w), lambda b, j: (0, 0)),
                  pl.BlockSpec((w, w), lambda b, j: (0, 0))],
        out_specs=ospec,
        out_shape=jax.ShapeDtypeStruct((bsz, length, w), BF16),
        compiler_params=_cp("parallel", "parallel"),
        name="rwkv_readout",
    )(o_f, o_b, rwp, rwp, lp['rw_lng'], lp['rw_lnb'], lp['mean_hd'])


def _ssd_prep_kernel(z_ref, zp_ref, zn_ref, cw_ref, cb_ref, dtb_ref, xbc_ref, dt_ref, *, nt):
    j = pl.program_id(1)
    x = z_ref[0, :, BR_W:BR_W + SSD_XBC]
    prow = jnp.where(j > 0, zp_ref[0, 7:8, BR_W:BR_W + SSD_XBC], 0.0)
    nrow = jnp.where(j < nt - 1, zn_ref[0, 0:1, BR_W:BR_W + SSD_XBC], 0.0)
    xbc_ref[0] = _silu(_conv3(x, prow, nrow, cw_ref, cb_ref))
    dt_raw = z_ref[0, :, BR_W + SSD_XBC:BR_W + SSD_XBC + LANES]
    dt_ref[0, 0] = _softplus(dt_raw + dtb_ref[0:1, :])
    dt_ref[0, 1] = _softplus(pltpu.roll(dt_raw, LANES - SSD_HEADS, 1) + dtb_ref[1:2, :])


def _ssd_prep(z, lp):
    bsz, length, fin = z.shape
    tr = _tile(length, 256, 8)
    nt = length // tr
    rb = tr // 8
    nh = length // 8
    return pl.pallas_call(
        functools.partial(_ssd_prep_kernel, nt=nt),
        grid=(bsz, nt),
        in_specs=[pl.BlockSpec((1, tr, fin), lambda b, j: (b, j, 0)),
                  pl.BlockSpec((1, 8, fin), lambda b, j: (b, jnp.maximum(j * rb - 1, 0), 0)),
                  pl.BlockSpec((1, 8, fin), lambda b, j: (b, jnp.minimum((j + 1) * rb, nh - 1), 0)),
                  pl.BlockSpec((3, SSD_XBC), lambda b, j: (0, 0)),
                  pl.BlockSpec((1, SSD_XBC), lambda b, j: (0, 0)),
                  pl.BlockSpec((2, LANES), lambda b, j: (0, 0))],
        out_specs=[pl.BlockSpec((1, tr, SSD_XBC), lambda b, j: (b, j, 0)),
                   pl.BlockSpec((1, 2, tr, LANES), lambda b, j: (b, 0, j, 0))],
        out_shape=[jax.ShapeDtypeStruct((bsz, length, SSD_XBC), F32),
                   jax.ShapeDtypeStruct((bsz, 2, length, LANES), F32)],
        compiler_params=_cp("parallel", "parallel"),
        name="ssd_prep",
    )(z, z, z, lp['ssd_cw'], lp['ssd_cb'], lp['ssd_dtb'])


def _ssd_chunks(dir_refs, a_ref, y_refs, h_scr, nb):
    q = SSD_CHUNK
    n = SSD_STATE
    rep = SSD_HEADS // SSD_GROUPS
    gw = SSD_GW
    ti = lax.broadcasted_iota(jnp.int32, (q, q), 0)
    ii = lax.broadcasted_iota(jnp.int32, (q, q), 1)
    ri = lax.broadcasted_iota(jnp.int32, (rep * q, gw), 0)
    ci = lax.broadcasted_iota(jnp.int32, (rep * q, gw), 1)
    head_mask = (ri >> int(math.log2(q))) == (ci >> int(math.log2(HD)))

    def per_head(x, h0):
        return jnp.concatenate([jnp.broadcast_to(x[:, h0 + h:h0 + h + 1], (x.shape[0], HD))
                                for h in range(rep)], axis=1)

    work = []
    for bb in range(nb):
        for d, (xbc_ref, dt_ref) in enumerate(dir_refs):
            incl = (ii - ti) * (1 - 2 * d) <= 0
            tri = jnp.where(incl, 1.0, 0.0).astype(BF16)
            dt = dt_ref[bb, 0]
            dta = dt * a_ref[d]
            acs = _dot01(tri, dta)
            acs_t = acs.T
            tot = jnp.sum(dta, axis=0, keepdims=True)
            e_acs = jnp.exp(acs)
            e_end = jnp.exp(tot - acs)
            e_tot = jnp.exp(tot)
            xbc = xbc_ref[bb]
            for g in range(SSD_GROUPS):
                h0 = g * rep
                bg = xbc[:, BR_W + g * n:BR_W + (g + 1) * n]
                cg = xbc[:, BR_W + SSD_GROUPS * n + g * n:BR_W + SSD_GROUPS * n + (g + 1) * n]
                xdt = xbc[:, g * gw:(g + 1) * gw] * per_head(dt, h0)
                dec = jnp.concatenate(
                    [jnp.exp(jnp.where(incl, acs[:, h0 + h:h0 + h + 1] - acs_t[h0 + h:h0 + h + 1, :], NEG))
                     for h in range(rep)], axis=1)
                work.append(dict(bb=bb, d=d, g=g, bg=bg, cg=cg, xdt=xdt, dec=dec,
                                 ht=h_scr[bb, d, g], e_acs=per_head(e_acs, h0),
                                 e_end=per_head(e_end, h0), e_tot=per_head(e_tot, h0)))
    for w in work:
        w['cb'] = _dot_nt(w['cg'], w['bg'])
    for w in work:
        w['yoff'] = _dot(w['cg'], w['ht'])
    for w in work:
        w['st'] = _dot_tn(w['bg'], w['xdt'] * w['e_end'])
    for w in work:
        xe = jnp.where(head_mask, jnp.concatenate([w['xdt']] * rep, axis=0), 0.0)
        scores = jnp.concatenate([w['cb']] * rep, axis=1) * w['dec']
        y = _dot(scores, xe) + w['e_acs'] * w['yoff']
        y_refs[w['d']][w['bb'], :, w['g'] * gw:(w['g'] + 1) * gw] = y
        h_scr[w['bb'], w['d'], w['g']] = w['ht'] * w['e_tot'] + w['st']


def _ssd_scan_kernel(xf_ref, dtf_ref, xb_ref, dtb_ref, a_ref, h0_ref, yf_ref, yb_ref, hfin_ref,
                     h_scr, *, nc, nb):
    c = pl.program_id(1)

    @pl.when(c == 0)
    def _():
        h_scr[...] = h0_ref[...]

    _ssd_chunks(((xf_ref, dtf_ref), (xb_ref, dtb_ref)), a_ref, (yf_ref, yb_ref), h_scr, nb)

    @pl.when(c == nc - 1)
    def _():
        hfin_ref[...] = h_scr[...]


def _ssd_scan(xbc, dt, a_neg, h_init):
    bsz, length, _ = xbc.shape
    q = SSD_CHUNK
    nc = length // q
    nb = RW_BATCH if bsz % RW_BATCH == 0 else 1
    hspec = pl.BlockSpec((nb, 2, SSD_GROUPS, SSD_STATE, SSD_GW), lambda b, c: (b, 0, 0, 0, 0))
    oshape = jax.ShapeDtypeStruct((bsz, length, BR_W), F32)
    return pl.pallas_call(
        functools.partial(_ssd_scan_kernel, nc=nc, nb=nb),
        grid=(bsz // nb, nc),
        in_specs=[pl.BlockSpec((nb, q, SSD_XBC), lambda b, c: (b, c, 0)),
                  pl.BlockSpec((nb, 1, q, LANES), lambda b, c: (b, 0, c, 0)),
                  pl.BlockSpec((nb, q, SSD_XBC), lambda b, c: (b, nc - 1 - c, 0)),
                  pl.BlockSpec((nb, 1, q, LANES), lambda b, c: (b, 1, nc - 1 - c, 0)),
                  pl.BlockSpec((2, 1, LANES), lambda b, c: (0, 0, 0)),
                  hspec],
        out_specs=[pl.BlockSpec((nb, q, BR_W), lambda b, c: (b, c, 0)),
                   pl.BlockSpec((nb, q, BR_W), lambda b, c: (b, nc - 1 - c, 0)), hspec],
        out_shape=[oshape, oshape,
                   jax.ShapeDtypeStruct((bsz, 2, SSD_GROUPS, SSD_STATE, SSD_GW), F32)],
        scratch_shapes=[pltpu.VMEM((nb, 2, SSD_GROUPS, SSD_STATE, SSD_GW), F32)],
        compiler_params=_cp("parallel", "arbitrary"),
        name="ssd_scan",
    )(xbc, dt, xbc, dt, a_neg, h_init)


def _ssd_readout_kernel(yf_ref, yb_ref, xs_ref, z_ref, dsk_ref, ng_ref, out_ref):
    y = yf_ref[0] + yb_ref[0] + dsk_ref[...] * xs_ref[0]
    y = y * _silu(z_ref[0])
    y = y * lax.rsqrt(jnp.mean(y * y, axis=-1, keepdims=True) + NORM_EPS) * ng_ref[...]
    out_ref[0] = y.astype(out_ref.dtype)


def _ssd_readout(y_f, y_b, xbc, z, lp):
    bsz, length, w = y_f.shape
    tr = _tile(length, 256, 16)
    ospec = pl.BlockSpec((1, tr, w), lambda b, j: (b, j, 0))
    return pl.pallas_call(
        _ssd_readout_kernel,
        grid=(bsz, length // tr),
        in_specs=[ospec, ospec, ospec, ospec,
                  pl.BlockSpec((1, w), lambda b, j: (0, 0)),
                  pl.BlockSpec((1, w), lambda b, j: (0, 0))],
        out_specs=ospec,
        out_shape=jax.ShapeDtypeStruct((bsz, length, w), BF16),
        compiler_params=_cp("parallel", "parallel"),
        name="ssd_readout",
    )(y_f, y_b, xbc, z, lp['ssd_dskip'], lp['ssd_ng'])


def _da_prep_kernel(z_ref, cos_ref, sin_ref, q_ref, k_ref, v_ref, *, rope):
    w = BR_W

    def rot(x):
        if not rope:
            return x
        lane = lax.broadcasted_iota(jnp.int32, (1, LANES), 1)
        first = (lane & 31) < 16
        outs = []
        for cb in range(w // LANES):
            xb = x[:, cb * LANES:(cb + 1) * LANES]
            sw = jnp.where(first, pltpu.roll(xb, LANES - 16, 1), pltpu.roll(xb, 16, 1))
            outs.append(xb * cos_ref[:, cb * LANES:(cb + 1) * LANES]
                        + sw * sin_ref[:, cb * LANES:(cb + 1) * LANES])
        return jnp.concatenate(outs, axis=1)

    q_ref[0] = (rot(z_ref[0, :, 0:w]) * (HD ** -0.5)).astype(q_ref.dtype)
    k_ref[0] = rot(z_ref[0, :, w:2 * w]).astype(k_ref.dtype)
    v_ref[0] = z_ref[0, :, 2 * w:3 * w].astype(v_ref.dtype)


def _da_prep(z, cos, sin, rope):
    bsz, length, fin = z.shape
    tr = _tile(length, 256, 16)
    w = BR_W
    ospec = pl.BlockSpec((1, tr, w), lambda b, j: (b, j, 0))
    oshape = jax.ShapeDtypeStruct((bsz, length, w), BF16)
    return pl.pallas_call(
        functools.partial(_da_prep_kernel, rope=rope),
        grid=(bsz, length // tr),
        in_specs=[pl.BlockSpec((1, tr, fin), lambda b, j: (b, j, 0)),
                  pl.BlockSpec((tr, w), lambda b, j: (j, 0)),
                  pl.BlockSpec((tr, w), lambda b, j: (j, 0))],
        out_specs=[ospec, ospec, ospec],
        out_shape=[oshape, oshape, oshape],
        compiler_params=_cp("parallel", "parallel"),
        name="da_prep",
    )(z, cos, sin)


def _da_attn_kernel(*refs, nsrc, lam_init):
    lam_ref, q_ref = refs[0], refs[1]
    kv = refs[2:2 + 2 * nsrc]
    g_ref, o_ref = refs[2 + 2 * nsrc], refs[3 + 2 * nsrc]
    q = q_ref[0]
    lane = lax.broadcasted_iota(jnp.int32, (1, LANES), 1)
    zero = jnp.zeros_like(q)
    lam = lam_ref[0:1, 0:1]
    probs = []
    for m in range(2):
        qm = jnp.where((lane < HD) if m == 0 else (lane >= HD), q, zero)
        ss = [_dot_nt(qm, kv[2 * i][0]) for i in range(nsrc)]
        mx = functools.reduce(jnp.maximum, [jnp.max(s, axis=-1, keepdims=True) for s in ss])
        es = [jnp.exp(s - mx) for s in ss]
        den = functools.reduce(lambda a, b: a + b, [jnp.sum(e, axis=-1, keepdims=True) for e in es])
        inv = 1.0 / den
        probs.append([e * inv for e in es])
    o = None
    for i in range(nsrc):
        a = probs[0][i] - lam * probs[1][i]
        t = _dot(a, kv[2 * i + 1][0])
        o = t if o is None else o + t
    y = o * lax.rsqrt(jnp.mean(o * o, axis=-1, keepdims=True) + DA_SUBLN_EPS) * g_ref[...]
    o_ref[0] = (y * (1.0 - lam_init)).astype(o_ref.dtype)


def _da_attn(q, ks, vs, lam, subln_g, lam_init):
    bsz, lq, _ = q.shape
    tq = _tile(lq, 256, 16)
    nsrc = len(ks)
    in_specs = [pl.BlockSpec((1, LANES), lambda b, h, j: (0, 0)),
                pl.BlockSpec((1, tq, LANES), lambda b, h, j: (b, j, h))]
    args = [lam, q]
    for k, v in zip(ks, vs):
        lk = k.shape[1]
        in_specs.append(pl.BlockSpec((1, lk, LANES), lambda b, h, j: (b, 0, h)))
        in_specs.append(pl.BlockSpec((1, lk, LANES), lambda b, h, j: (b, 0, h)))
        args += [k, v]
    in_specs.append(pl.BlockSpec((1, LANES), lambda b, h, j: (0, 0)))
    args.append(subln_g)
    return pl.pallas_call(
        functools.partial(_da_attn_kernel, nsrc=nsrc, lam_init=lam_init),
        grid=(bsz, DA_HEADS, lq // tq),
        in_specs=in_specs,
        out_specs=pl.BlockSpec((1, tq, LANES), lambda b, h, j: (b, j, h)),
        out_shape=jax.ShapeDtypeStruct((bsz, lq, BR_W), BF16),
        compiler_params=_cp("parallel", "parallel", "parallel"),
        name="diff_attn",
    )(*args)


def _na_kb(i, rows):
    return jnp.clip(i * NA_QROWS - NA_WIN_R // 2, 0, rows - NA_KROWS)


def _na_attn_kernel(q_ref, k_ref, v_ref, kc_ref, vc_ref, bias_ref, o_ref, *, rows):
    i = pl.program_id(2)
    start = pl.multiple_of(_na_kb(i, rows) * GRID_W, GRID_W)
    nk = NA_KROWS * GRID_W
    q = q_ref[0] * (HD ** -0.5)
    kw = k_ref[0, pl.ds(start, nk), :]
    vw = v_ref[0, pl.ds(start, nk), :]
    kc = kc_ref[0]
    vc = vc_ref[0]
    lane = lax.broadcasted_iota(jnp.int32, (1, LANES), 1)
    outs = []
    for h in range(2):
        qm = jnp.where((lane < HD) if h == 0 else (lane >= HD), q, 0.0)
        sw = _dot_nt(qm, kw) + bias_ref[h, 0]
        sc = _dot_nt(qm, kc)
        mx = jnp.maximum(jnp.max(sw, axis=-1, keepdims=True), jnp.max(sc, axis=-1, keepdims=True))
        ew = jnp.exp(sw - mx)
        ec = jnp.exp(sc - mx)
        inv = 1.0 / (jnp.sum(ew, axis=-1, keepdims=True) + jnp.sum(ec, axis=-1, keepdims=True))
        outs.append((_dot(ew, vw) + _dot(ec, vc)) * inv)
    o_ref[0] = jnp.where(lane < HD, outs[0], outs[1]).astype(o_ref.dtype)


def _na_attn(z_l, z_c, bias, pat_id):
    bsz, length, _ = z_l.shape
    lc = z_c.shape[1]
    rows = length // GRID_W
    tq = NA_QROWS * GRID_W
    nblk = length // tq
    nkb = BR_W // LANES

    def pat(i):
        out = pat_id[0]
        for j in range(1, nblk):
            if pat_id[j] != pat_id[j - 1]:
                out = out + jnp.where(i >= j, pat_id[j] - pat_id[j - 1], 0)
        return out

    return pl.pallas_call(
        functools.partial(_na_attn_kernel, rows=rows),
        grid=(bsz, NA_HEADS // 2, nblk),
        in_specs=[pl.BlockSpec((1, tq, LANES), lambda b, p, i: (b, i, p)),
                  pl.BlockSpec((1, length, LANES), lambda b, p, i: (b, 0, nkb + p)),
                  pl.BlockSpec((1, length, LANES), lambda b, p, i: (b, 0, 2 * nkb + p)),
                  pl.BlockSpec((1, lc, LANES), lambda b, p, i: (b, 0, nkb + p)),
                  pl.BlockSpec((1, lc, LANES), lambda b, p, i: (b, 0, 2 * nkb + p)),
                  pl.BlockSpec((2, 1, tq, NA_KROWS * GRID_W), lambda b, p, i: (p, pat(i), 0, 0))],
        out_specs=pl.BlockSpec((1, tq, LANES), lambda b, p, i: (b, i, p)),
        out_shape=jax.ShapeDtypeStruct((bsz, length, BR_W), BF16),
        compiler_params=_cp("parallel", "parallel", "parallel"),
        name="na_attn",
    )(z_l, z_l, z_l, z_c, z_c, bias)


def _ctx_attn_kernel(q_ref, k_ref, v_ref, o_ref):
    q = q_ref[0] * (HD ** -0.5)
    k = k_ref[0]
    v = v_ref[0]
    lane = lax.broadcasted_iota(jnp.int32, (1, LANES), 1)
    outs = []
    for h in range(2):
        qm = jnp.where((lane < HD) if h == 0 else (lane >= HD), q, 0.0)
        s = _dot_nt(qm, k)
        e = jnp.exp(s - jnp.max(s, axis=-1, keepdims=True))
        outs.append(_dot(e, v) * (1.0 / jnp.sum(e, axis=-1, keepdims=True)))
    o_ref[0] = jnp.where(lane < HD, outs[0], outs[1]).astype(o_ref.dtype)


def _ctx_attn(z_c):
    bsz, lc, _ = z_c.shape
    nkb = BR_W // LANES
    return pl.pallas_call(
        _ctx_attn_kernel,
        grid=(bsz, NA_HEADS // 2),
        in_specs=[pl.BlockSpec((1, lc, LANES), lambda b, p: (b, 0, p)),
                  pl.BlockSpec((1, lc, LANES), lambda b, p: (b, 0, nkb + p)),
                  pl.BlockSpec((1, lc, LANES), lambda b, p: (b, 0, 2 * nkb + p))],
        out_specs=pl.BlockSpec((1, lc, LANES), lambda b, p: (b, 0, p)),
        out_shape=jax.ShapeDtypeStruct((bsz, lc, BR_W), BF16),
        compiler_params=_cp("parallel", "parallel"),
        name="ctx_attn",
    )(z_c, z_c, z_c)


def _rope_tables(length):
    n_freq = HD // 4
    t = np.arange(length)
    pos = np.stack([t // GRID_W, t % GRID_W], axis=-1).astype(np.float32)
    inv = (ROPE_BASE ** (-np.arange(n_freq, dtype=np.float32) / n_freq)).astype(np.float32)
    lane = np.arange(BR_W)
    which = (lane % HD) // (HD // 2)
    ang = pos[:, which] * inv[lane % n_freq][None, :]
    sign = np.where((lane % (HD // 2)) < n_freq, -1.0, 1.0).astype(np.float32)
    return jnp.asarray(np.cos(ang), F32), jnp.asarray(np.sin(ang) * sign[None, :], F32)


def _na_bias(rpb, length):
    rows = length // GRID_W
    wr = min(NA_WIN_R, rows)
    nblk = rows // NA_QROWS
    qr = np.arange(rows)
    rstart = np.clip(qr - wr // 2, 0, rows - wr)
    kb = np.clip(np.arange(nblk) * NA_QROWS - NA_WIN_R // 2, 0, rows - NA_KROWS)
    qrow = (np.arange(nblk)[:, None] * NA_QROWS + np.arange(NA_QROWS)[None, :])
    krow = kb[:, None] + np.arange(NA_KROWS)[None, :]
    dr = krow[:, None, :] - qrow[:, :, None] + NA_WIN_R - 1
    rvalid = (krow[:, None, :] >= rstart[qrow][:, :, None]) & (krow[:, None, :] < rstart[qrow][:, :, None] + wr)
    dr = np.where(rvalid, dr, -1)
    pats, pat_id = [], []
    for i in range(nblk):
        for j, pdr in enumerate(pats):
            if np.array_equal(pdr, dr[i]):
                pat_id.append(j)
                break
        else:
            pat_id.append(len(pats))
            pats.append(dr[i])
    pdr = np.stack(pats)
    n_dr = 2 * NA_WIN_R - 1
    oh_dr = (pdr[..., None] == np.arange(n_dr)).astype(np.float32)
    cid = np.arange(GRID_W)
    cstart = np.clip(cid - NA_WIN_C // 2, 0, GRID_W - NA_WIN_C)
    in_win = (cid[None, :] >= cstart[:, None]) & (cid[None, :] < cstart[:, None] + NA_WIN_C)
    ci = np.clip(cid[None, :] - cid[:, None], -(NA_WIN_C - 1), NA_WIN_C - 1) + NA_WIN_C - 1
    n_ci = 2 * NA_WIN_C - 1
    oh_ci = (ci[..., None] == np.arange(n_ci)).astype(np.float32)
    t1 = jnp.einsum('hdc,qkc->hdqk', rpb.astype(F32), jnp.asarray(oh_ci),
                    precision=lax.Precision.HIGHEST)
    big = jnp.einsum('brkd,hdxy->hbrxky', jnp.asarray(oh_dr), t1,
                     precision=lax.Precision.HIGHEST)
    valid = (pdr >= 0)[:, :, None, :, None] & in_win[None, None, :, None, :]
    big = jnp.where(jnp.asarray(valid)[None], big, NEG)
    big = big.reshape(rpb.shape[0], len(pats), NA_QROWS * GRID_W, NA_KROWS * GRID_W)
    return big, tuple(pat_id)


def _block_diag(blocks):
    n = len(blocks)
    r, c = blocks[0].shape
    out = jnp.zeros((n * r, n * c), blocks[0].dtype)
    for i, blk in enumerate(blocks):
        out = out.at[i * r:(i + 1) * r, i * c:(i + 1) * c].set(blk)
    return out


def _layer_params(p, i):
    w = BR_W
    w_in = p['w_in'][i]
    o1, o2, o3 = RW_IN, RW_IN + SSD_IN, RW_IN + SSD_IN + DA_IN
    d = w_in.shape[0]
    w_ssd = jnp.concatenate([w_in[:, o1:o2], jnp.zeros((d, SSD_IN_PAD - SSD_IN), w_in.dtype)], axis=1)
    head_id = np.arange(w) // HD
    ones_hd = jnp.asarray((head_id[:, None] == head_id[None, :]).astype(np.float32), BF16)
    lam_p = p['da_lambda'][i].astype(F32)
    lam_init = 0.8 - 0.6 * math.exp(-0.3 * i)
    lam = jnp.exp(jnp.sum(lam_p[0] * lam_p[1])) - jnp.exp(jnp.sum(lam_p[2] * lam_p[3])) + lam_init
    dtb = p['ssd_dt_bias'][i]
    return {
        'w_rw': w_in[:, :o1].astype(BF16), 'w_ssd': w_ssd.astype(BF16),
        'w_da': w_in[:, o2:o3].astype(BF16), 'w_na': w_in[:, o3:].astype(BF16),
        'rw_mu': p['rw_mu'][i], 'rw_w0': p['rw_w0'][i],
        'rw_wup': _block_diag([p['rw_w_up'][i, 0], p['rw_w_up'][i, 1]]).astype(BF16),
        'rw_a0': p['rw_a0'][i],
        'rw_aup': _block_diag([p['rw_a_up'][i, 0], p['rw_a_up'][i, 1]]).astype(BF16),
        'rw_gup': p['rw_g_up'][i].astype(BF16),
        'rw_kk': p['rw_k_k'][i].reshape(1, w), 'rw_ka': p['rw_k_a'][i].reshape(1, w),
        'rw_rk': p['rw_r_k'][i].reshape(1, w),
        'rw_lng': p['rw_ln_g'][i].reshape(1, w), 'rw_lnb': p['rw_ln_b'][i].reshape(1, w),
        'ones_hd': ones_hd, 'mean_hd': (ones_hd.astype(F32) / HD).astype(BF16),
        'ssd_cw': p['ssd_conv_w'][i], 'ssd_cb': p['ssd_conv_b'][i].reshape(1, SSD_XBC),
        'ssd_dtb': jnp.pad(dtb, ((0, 0), (0, LANES - SSD_HEADS))),
        'ssd_a': jnp.pad(-jnp.exp(p['ssd_a_log'][i].astype(F32)), ((0, 0), (0, LANES - SSD_HEADS))).reshape(2, 1, LANES),
        'ssd_dskip': jnp.repeat(p['ssd_d'][i], HD).reshape(1, w),
        'ssd_ng': p['ssd_norm_g'][i].reshape(1, w),
        'da_lam': jnp.broadcast_to(lam.reshape(1, 1), (1, LANES)).astype(F32),
        'da_lam_init': lam_init,
        'da_g': p['da_subln_g'][i].reshape(1, 2 * HD),
        'na_rpb': p['na_rpb'][i],
        'w_gate': p['w_gate'][i].astype(BF16),
        'gate_b': p['gate_b'][i][:, None, :],
        'w_br': p['w_br'][i].astype(BF16), 'w_out': p['w_out'][i].astype(BF16),
        'ffn_up': p['ffn_up'][i].astype(BF16), 'ffn_cw': p['ffn_conv_w'][i],
        'ffn_cb': p['ffn_conv_b'][i], 'ffn_down': p['ffn_down'][i].astype(BF16),
    }


def _project(h, lp):
    bsz, length, d = h.shape
    h2 = h.reshape(bsz * length, d)
    return tuple(_matmul(h2, lp[n], F32).reshape(bsz, length, -1)
                 for n in ('w_rw', 'w_ssd', 'w_da', 'w_na'))


def _ffn(x, h2, lp, gate):
    act = _ffn_up(h2, lp['ffn_up'], lp['ffn_cw'], lp['ffn_cb'])
    return _matmul_residual(act, lp['ffn_down'], x, gate)


def kernel(x, c, ctx, c_ctx, ada_w, ada_b, norm1_g, norm2_g, w_in, rw_mu, rw_w0, rw_w_up, rw_a0, rw_a_up, rw_g_up, rw_k_k, rw_k_a, rw_r_k, rw_ln_g, rw_ln_b, ssd_conv_w, ssd_conv_b, ssd_dt_bias, ssd_a_log, ssd_d, ssd_norm_g, da_lambda, da_subln_g, na_rpb, w_gate, gate_b, w_br, w_out, ffn_up, ffn_conv_w, ffn_conv_b, ffn_down, final_norm_g):
    p = dict(w_in=w_in, rw_mu=rw_mu, rw_w0=rw_w0, rw_w_up=rw_w_up, rw_a0=rw_a0, rw_a_up=rw_a_up,
             rw_g_up=rw_g_up, rw_k_k=rw_k_k, rw_k_a=rw_k_a, rw_r_k=rw_r_k, rw_ln_g=rw_ln_g,
             rw_ln_b=rw_ln_b, ssd_conv_w=ssd_conv_w, ssd_conv_b=ssd_conv_b, ssd_dt_bias=ssd_dt_bias,
             ssd_a_log=ssd_a_log, ssd_d=ssd_d, ssd_norm_g=ssd_norm_g, da_lambda=da_lambda,
             da_subln_g=da_subln_g, na_rpb=na_rpb, w_gate=w_gate, gate_b=gate_b, w_br=w_br,
             w_out=w_out, ffn_up=ffn_up, ffn_conv_w=ffn_conv_w, ffn_conv_b=ffn_conv_b,
             ffn_down=ffn_down)
    bsz, seq, d = x.shape
    depth = ada_w.shape[0]
    lctx = ctx.shape[1]
    mrows = -(-(bsz + 1) // 16) * 16
    cc = jnp.zeros((mrows, d), F32).at[:bsz].set(c).at[bsz].set(c_ctx)
    cos, sin = _rope_tables(seq)
    xl, xc = x, ctx
    for i in range(depth):
        last = i == depth - 1
        lp = _layer_params(p, i)
        mod = _modulation(cc, ada_w[i], ada_b[i]).reshape(mrows, 6, d)
        ml = [mod[:bsz, n][:, None, :] for n in range(6)]
        mc = [mod[bsz:bsz + 1, n][:, None, :] for n in range(6)]

        hl = _norm_mod(xl, norm1_g[i], ml[0], ml[1])
        hc = _norm_mod(xc, norm1_g[i], mc[0], mc[1])
        zl = _project(hl, lp)
        zc = _project(hc, lp)

        rp_c = _rw_prep(zc[0], lp)
        rp_l = _rw_prep(zl[0], lp)
        s_zero = jnp.zeros((bsz, 2, RW_HEADS // RW_GROUP, RW_GW, RW_GW), F32)
        of_c, ob_c, s_c = _rw_scan(rp_c, s_zero)
        of_l, ob_l, _ = _rw_scan(rp_l, s_c)
        a_l = _rw_readout(of_l, ob_l, rp_l, lp)

        xbc_c, dt_c = _ssd_prep(zc[1], lp)
        xbc_l, dt_l = _ssd_prep(zl[1], lp)
        h_zero = jnp.zeros((bsz, 2, SSD_GROUPS, SSD_STATE, SSD_GW), F32)
        yf_c, yb_c, hfin_c = _ssd_scan(xbc_c, dt_c, lp['ssd_a'], h_zero)
        yf_l, yb_l, _ = _ssd_scan(xbc_l, dt_l, lp['ssd_a'], hfin_c)
        b_l = _ssd_readout(yf_l, yb_l, xbc_l, zl[1], lp)

        q_l, k_l, v_l = _da_prep(zl[2], cos, sin, True)
        q_c, k_c, v_c = _da_prep(zc[2], cos, sin, False)
        c_l = _da_attn(q_l, [k_l, k_c], [v_l, v_c], lp['da_lam'], lp['da_g'], lp['da_lam_init'])

        d_l = _na_attn(zl[3], zc[3], *_na_bias(lp['na_rpb'], seq))

        merged = _gated_merge(hl.reshape(bsz * seq, d),
                              [t.reshape(bsz * seq, BR_W) for t in (a_l, b_l, c_l, d_l)],
                              lp['w_gate'], lp['gate_b'], lp['w_br']).reshape(bsz, seq, d)
        xl, hl2 = _matmul_residual_norm(merged, lp['w_out'], xl, ml[2], norm2_g[i], ml[3], ml[4])
        xl = _ffn(xl, hl2, lp, ml[5])

        if not last:
            a_c = _rw_readout(of_c, ob_c, rp_c, lp)
            b_c = _ssd_readout(yf_c, yb_c, xbc_c, zc[1], lp)
            c_c = _da_attn(q_c, [k_c], [v_c], lp['da_lam'], lp['da_g'], lp['da_lam_init'])
            d_c = _ctx_attn(zc[3])
            merged_c = _gated_merge(hc.reshape(bsz * lctx, d),
                                    [t.reshape(bsz * lctx, BR_W) for t in (a_c, b_c, c_c, d_c)],
                                    lp['w_gate'], lp['gate_b'], lp['w_br']).reshape(bsz, lctx, d)
            xc, hc2 = _matmul_residual_norm(merged_c, lp['w_out'], xc, mc[2], norm2_g[i], mc[3], mc[4])
            xc = _ffn(xc, hc2, lp, mc[5])
    return _final_norm(xl, final_norm_g)
```

```python
import functools
import math

import numpy as np
import jax
import jax.numpy as jnp
from jax import lax
from jax.experimental import pallas as pl
from jax.experimental.pallas import tpu as pltpu

F32 = jnp.float32
BF16 = jnp.bfloat16

GRID_W = 64
NORM_EPS = 1e-6
HD = 64
BR_W = 512
RW_HEADS = 8
RW_RANK = 64
RW_GATE_RANK = 128
RW_GN_EPS = 64e-5
RW_IN = 3 * BR_W + 4 * RW_RANK + RW_GATE_RANK
RW_CHUNK = 64
RW_NARR = 9
RW_R, RW_V, RW_KK, RW_KD, RW_BD, RW_G, RW_BONUS = 0, 1, 2, 3, 4, 7, 8
RW_GROUP = 4
RW_GW = RW_GROUP * HD
RW_BATCH = 2
SSD_HEADS = 8
SSD_GROUPS = 2
SSD_STATE = 128
SSD_XBC = BR_W + 2 * SSD_GROUPS * SSD_STATE
SSD_IN = BR_W + SSD_XBC + 2 * SSD_HEADS
SSD_IN_PAD = BR_W + SSD_XBC + 128
SSD_CHUNK = 128
SSD_GW = SSD_HEADS // SSD_GROUPS * HD
DA_HEADS = 4
DA_IN = 3 * BR_W
DA_SUBLN_EPS = 1e-5
NA_HEADS = 8
NA_IN = 3 * BR_W
NA_WIN_R = 8
NA_WIN_C = 16
NA_QROWS = 4
NA_KROWS = 12
ROPE_BASE = 10000.0
NEG = -1e30
LANES = 128
VMEM_LIMIT = 56 * 1024 * 1024


def _cp(*sem):
    return pltpu.CompilerParams(dimension_semantics=sem, vmem_limit_bytes=VMEM_LIMIT)


def _tile(n, pref, mult):
    t = min(n, pref)
    t -= t % mult
    while t >= mult:
        if n % t == 0:
            return t
        t -= mult
    return n


def _dot(a, b):
    return jnp.dot(a.astype(BF16), b.astype(BF16), preferred_element_type=F32)


def _dot_nt(a, b):
    return lax.dot_general(a.astype(BF16), b.astype(BF16), (((1,), (1,)), ((), ())),
                           preferred_element_type=F32)


def _dot_tn(a, b):
    return lax.dot_general(a.astype(BF16), b.astype(BF16), (((0,), (0,)), ((), ())),
                           preferred_element_type=F32)


def _split3(x):
    hi = x.astype(BF16)
    r1 = x - hi.astype(F32)
    mid = r1.astype(BF16)
    lo = (r1 - mid.astype(F32)).astype(BF16)
    return hi, mid, lo


def _dot01(m01, x):
    hi, mid, lo = _split3(x)
    return (jnp.dot(m01, hi, preferred_element_type=F32)
            + jnp.dot(m01, mid, preferred_element_type=F32)
            + jnp.dot(m01, lo, preferred_element_type=F32))


def _x_dot01(x, m01):
    hi, mid, lo = _split3(x)
    return (jnp.dot(hi, m01, preferred_element_type=F32)
            + jnp.dot(mid, m01, preferred_element_type=F32)
            + jnp.dot(lo, m01, preferred_element_type=F32))


def _sigmoid(x):
    return 1.0 / (1.0 + jnp.exp(-x))


def _silu(x):
    return x * _sigmoid(x)


def _softplus(x):
    return jnp.maximum(x, 0.0) + jnp.log(1.0 + jnp.exp(-jnp.abs(x)))


def _mod_kernel(c_ref, w_ref, b_ref, o_ref):
    a = _silu(c_ref[...])
    o_ref[...] = _dot(a, w_ref[...]) + b_ref[...]


def _modulation(cc, w, b):
    m, d = cc.shape
    n = w.shape[1]
    tn = _tile(n, 1024, LANES)
    return pl.pallas_call(
        _mod_kernel,
        grid=(n // tn,),
        in_specs=[pl.BlockSpec((m, d), lambda j: (0, 0)),
                  pl.BlockSpec((d, tn), lambda j: (0, j)),
                  pl.BlockSpec((1, tn), lambda j: (0, j))],
        out_specs=pl.BlockSpec((m, tn), lambda j: (0, j)),
        out_shape=jax.ShapeDtypeStruct((m, n), F32),
        compiler_params=_cp("parallel"),
        name="adaln_mod",
    )(cc, w, b.reshape(1, n))


def _norm_mod_kernel(x_ref, g_ref, sh_ref, sc_ref, o_ref):
    x = x_ref[0]
    y = x * lax.rsqrt(jnp.mean(x * x, axis=-1, keepdims=True) + NORM_EPS) * g_ref[...]
    o_ref[0] = (y * (1.0 + sc_ref[0]) + sh_ref[0]).astype(o_ref.dtype)


def _norm_mod(x, g, shift, scale):
    bsz, length, d = x.shape
    tr = _tile(length, 256, 16)
    bm = shift.shape[0]
    mod_map = (lambda b, j: (b, 0, 0)) if bm == bsz else (lambda b, j: (0, 0, 0))
    return pl.pallas_call(
        _norm_mod_kernel,
        grid=(bsz, length // tr),
        in_specs=[pl.BlockSpec((1, tr, d), lambda b, j: (b, j, 0)),
                  pl.BlockSpec((1, d), lambda b, j: (0, 0)),
                  pl.BlockSpec((1, 1, d), mod_map),
                  pl.BlockSpec((1, 1, d), mod_map)],
        out_specs=pl.BlockSpec((1, tr, d), lambda b, j: (b, j, 0)),
        out_shape=jax.ShapeDtypeStruct((bsz, length, d), BF16),
        compiler_params=_cp("parallel", "parallel"),
        name="norm_mod",
    )(x, g.reshape(1, d), shift, scale)


def _final_norm_kernel(x_ref, g_ref, o_ref):
    x = x_ref[0]
    o_ref[0] = x * lax.rsqrt(jnp.mean(x * x, axis=-1, keepdims=True) + NORM_EPS) * g_ref[...]


def _final_norm(x, g):
    bsz, length, d = x.shape
    tr = _tile(length, 256, 8)
    return pl.pallas_call(
        _final_norm_kernel,
        grid=(bsz, length // tr),
        in_specs=[pl.BlockSpec((1, tr, d), lambda b, j: (b, j, 0)),
                  pl.BlockSpec((1, d), lambda b, j: (0, 0))],
        out_specs=pl.BlockSpec((1, tr, d), lambda b, j: (b, j, 0)),
        out_shape=jax.ShapeDtypeStruct((bsz, length, d), F32),
        compiler_params=_cp("parallel", "parallel"),
        name="final_norm",
    )(x, g.reshape(1, d))


def _mm_kernel(a_ref, w_ref, o_ref):
    o_ref[...] = jnp.dot(a_ref[...], w_ref[...], preferred_element_type=F32).astype(o_ref.dtype)


def _matmul(a, w, out_dtype, tm_pref=512, tn_pref=2048):
    m, k = a.shape
    n = w.shape[1]
    tm = _tile(m, tm_pref, 16)
    tn = _tile(n, tn_pref, LANES)
    return pl.pallas_call(
        _mm_kernel,
        grid=(m // tm, n // tn),
        in_specs=[pl.BlockSpec((tm, k), lambda i, j: (i, 0)),
                  pl.BlockSpec((k, tn), lambda i, j: (0, j))],
        out_specs=pl.BlockSpec((tm, tn), lambda i, j: (i, j)),
        out_shape=jax.ShapeDtypeStruct((m, n), out_dtype),
        compiler_params=_cp("parallel", "parallel"),
        name="matmul",
    )(a, w)


def _mm_res_kernel(a_ref, w_ref, x_ref, g_ref, o_ref):
    y = jnp.dot(a_ref[0], w_ref[...], preferred_element_type=F32)
    o_ref[0] = x_ref[0] + g_ref[0] * y


def _matmul_residual(a, w, x, gate):
    bsz, length, k = a.shape
    n = w.shape[1]
    tm = _tile(length, 1024, 16)
    tn = _tile(n, 512, LANES)
    bm = gate.shape[0]
    gmap = (lambda b, i, j: (b, 0, j)) if bm == bsz else (lambda b, i, j: (0, 0, j))
    return pl.pallas_call(
        _mm_res_kernel,
        grid=(bsz, length // tm, n // tn),
        in_specs=[pl.BlockSpec((1, tm, k), lambda b, i, j: (b, i, 0)),
                  pl.BlockSpec((k, tn), lambda b, i, j: (0, j)),
                  pl.BlockSpec((1, tm, tn), lambda b, i, j: (b, i, j)),
                  pl.BlockSpec((1, 1, tn), gmap)],
        out_specs=pl.BlockSpec((1, tm, tn), lambda b, i, j: (b, i, j)),
        out_shape=jax.ShapeDtypeStruct((bsz, length, n), F32),
        compiler_params=_cp("parallel", "parallel", "parallel"),
        name="matmul_residual",
    )(a, w, x, gate)


def _mm_res_norm_kernel(a_ref, w_ref, x_ref, g_ref, ng_ref, sh_ref, sc_ref, xo_ref, h_ref):
    xn = x_ref[0] + g_ref[0] * jnp.dot(a_ref[0], w_ref[...], preferred_element_type=F32)
    xo_ref[0] = xn
    y = xn * lax.rsqrt(jnp.mean(xn * xn, axis=-1, keepdims=True) + NORM_EPS) * ng_ref[...]
    h_ref[0] = (y * (1.0 + sc_ref[0]) + sh_ref[0]).astype(h_ref.dtype)


def _matmul_residual_norm(a, w, x, gate, norm_g, shift, scale):
    bsz, length, k = a.shape
    n = w.shape[1]
    tm = _tile(length, 512, 16)
    bm = gate.shape[0]
    mmap = (lambda b, i: (b, 0, 0)) if bm == bsz else (lambda b, i: (0, 0, 0))
    row = pl.BlockSpec((1, tm, n), lambda b, i: (b, i, 0))
    mod = pl.BlockSpec((1, 1, n), mmap)
    return pl.pallas_call(
        _mm_res_norm_kernel,
        grid=(bsz, length // tm),
        in_specs=[pl.BlockSpec((1, tm, k), lambda b, i: (b, i, 0)),
                  pl.BlockSpec((k, n), lambda b, i: (0, 0)),
                  row, mod, pl.BlockSpec((1, n), lambda b, i: (0, 0)), mod, mod],
        out_specs=[row, row],
        out_shape=[jax.ShapeDtypeStruct((bsz, length, n), F32),
                   jax.ShapeDtypeStruct((bsz, length, n), BF16)],
        compiler_params=_cp("parallel", "parallel"),
        name="outproj_norm",
    )(a, w, x, gate, norm_g.reshape(1, n), shift, scale)


def _merge_kernel(h_ref, o0_ref, o1_ref, o2_ref, o3_ref, wg_ref, gb_ref, wbr_ref, out_ref):
    h = h_ref[...]
    acc = None
    for n, o_ref in enumerate((o0_ref, o1_ref, o2_ref, o3_ref)):
        gate = _sigmoid(jnp.dot(h, wg_ref[n], preferred_element_type=F32) + gb_ref[n])
        term = gate * jnp.dot(o_ref[...], wbr_ref[n], preferred_element_type=F32)
        acc = term if acc is None else acc + term
    out_ref[...] = acc.astype(out_ref.dtype)


def _gated_merge(h, branches, wg, gb, wbr):
    m, d = h.shape
    nb = len(branches)
    tm = _tile(m, 1024, 16)
    tn = _tile(d, 512, LANES)
    bspec = pl.BlockSpec((tm, BR_W), lambda i, j: (i, 0))
    return pl.pallas_call(
        _merge_kernel,
        grid=(m // tm, d // tn),
        in_specs=[pl.BlockSpec((tm, d), lambda i, j: (i, 0)), bspec, bspec, bspec, bspec,
                  pl.BlockSpec((nb, d, tn), lambda i, j: (0, 0, j)),
                  pl.BlockSpec((nb, 1, tn), lambda i, j: (0, 0, j)),
                  pl.BlockSpec((nb, BR_W, tn), lambda i, j: (0, 0, j))],
        out_specs=pl.BlockSpec((tm, tn), lambda i, j: (i, j)),
        out_shape=jax.ShapeDtypeStruct((m, d), BF16),
        compiler_params=_cp("parallel", "parallel"),
        name="gated_merge",
    )(h, *branches, wg, gb, wbr)


def _conv3(x, prev_row, next_row, w_ref, b_ref):
    rows = x.shape[0]
    rid = lax.broadcasted_iota(jnp.int32, x.shape, 0)
    prev = jnp.where(rid == 0, prev_row, pltpu.roll(x, 1, 0))
    nxt = jnp.where(rid == rows - 1, next_row, pltpu.roll(x, rows - 1, 0))
    return b_ref[...] + prev * w_ref[0:1, :] + x * w_ref[1:2, :] + nxt * w_ref[2:3, :]


def _ffn_up_kernel(*refs, nt, halo):
    if nt == 1:
        a_ref = refs[0]
        refs = refs[1:]
        halo = 0
        a_ext = a_ref[0]
    else:
        a_ref, ap_ref, an_ref = refs[0:3]
        refs = refs[3:]
        a_ext = jnp.concatenate([ap_ref[0], a_ref[0], an_ref[0]], axis=0)
    wg_ref, wv_ref, cwg_ref, cbg_ref, cwv_ref, cbv_ref, o_ref = refs
    i = pl.program_id(1)
    tm = a_ref.shape[1]
    n_ext = tm + 2 * halo
    rid = lax.broadcasted_iota(jnp.int32, (tm, 1), 0)
    no_prev = jnp.logical_and(i == 0, rid == 0)
    no_next = jnp.logical_and(i == nt - 1, rid == tm - 1)

    def branch(w_ref, cw_ref, cb_ref):
        u = jnp.dot(a_ext, w_ref[...], preferred_element_type=F32)
        um1 = jnp.where(no_prev, 0.0, pltpu.roll(u, 1, 0)[halo:halo + tm])
        up1 = jnp.where(no_next, 0.0, pltpu.roll(u, n_ext - 1, 0)[halo:halo + tm])
        return (cb_ref[...] + um1 * cw_ref[0:1, :] + u[halo:halo + tm] * cw_ref[1:2, :]
                + up1 * cw_ref[2:3, :])

    gate = branch(wg_ref, cwg_ref, cbg_ref)
    val = branch(wv_ref, cwv_ref, cbv_ref)
    o_ref[0] = (_silu(gate) * val).astype(o_ref.dtype)


def _ffn_up(h, w_up, conv_w, conv_b):
    bsz, length, d = h.shape
    f2 = w_up.shape[1]
    f = f2 // 2
    halo = 16
    tm = _tile(length, 1024, halo)
    tc = _tile(f, 512, LANES)
    nt = length // tm
    nc = f // tc
    nh = length // halo
    rb = tm // halo

    def wspec(rows, off):
        return pl.BlockSpec((rows, tc), lambda b, i, c: (0, c + off))

    a_specs = [pl.BlockSpec((1, tm, d), lambda b, i, c: (b, i, 0))]
    if nt > 1:
        a_specs += [
            pl.BlockSpec((1, halo, d), lambda b, i, c: (b, jnp.maximum(i * rb - 1, 0), 0)),
            pl.BlockSpec((1, halo, d), lambda b, i, c: (b, jnp.minimum((i + 1) * rb, nh - 1), 0))]
    cb = conv_b.reshape(1, f2)
    return pl.pallas_call(
        functools.partial(_ffn_up_kernel, nt=nt, halo=halo),
        grid=(bsz, nt, nc),
        in_specs=a_specs + [wspec(d, 0), wspec(d, nc), wspec(3, 0), wspec(1, 0), wspec(3, nc),
                            wspec(1, nc)],
        out_specs=pl.BlockSpec((1, tm, tc), lambda b, i, c: (b, i, c)),
        out_shape=jax.ShapeDtypeStruct((bsz, length, f), BF16),
        compiler_params=_cp("parallel", "parallel", "parallel"),
        name="ffn_up",
    )(*([h] * len(a_specs)), w_up, w_up, conv_w, cb, conv_w, cb)


def _rw_prep_kernel(z_ref, zp_ref, zn_ref, mu_ref, w0_ref, wup_ref, a0_ref, aup_ref, gup_ref,
                    kk_ref, ka_ref, rk_ref, ones_ref, o_ref, lw_out_ref, *, nt):
    j = pl.program_id(1)
    p = z_ref[0]
    rows = p.shape[0]
    rid = lax.broadcasted_iota(jnp.int32, p.shape, 0)
    prow = jnp.where(j > 0, zp_ref[0, 7:8, :], 0.0)
    nrow = jnp.where(j < nt - 1, zn_ref[0, 0:1, :], 0.0)
    prev = jnp.where(rid == 0, prow, pltpu.roll(p, 1, 0))
    nxt = jnp.where(rid == rows - 1, nrow, pltpu.roll(p, rows - 1, 0))
    ps = p + mu_ref[0:1, :] * (prev - p) + mu_ref[1:2, :] * (nxt - p)

    w = BR_W
    r = ps[:, 0:w]
    k = ps[:, w:2 * w]
    v = ps[:, 2 * w:3 * w]
    wd = ps[:, 3 * w:3 * w + 128]
    ad = ps[:, 3 * w + 128:3 * w + 256]
    gd = ps[:, 3 * w + 256:3 * w + 384]

    wraw = _dot(jnp.tanh(wd), wup_ref[...])
    araw = _dot(ad, aup_ref[...])
    g = _dot(_sigmoid(gd), gup_ref[...])
    ones = ones_ref[...]
    kkv = k * kk_ref[...]
    ss = _x_dot01(kkv * kkv, ones)
    kkn = kkv / jnp.maximum(jnp.sqrt(ss), 1e-12)

    def put(i, x):
        o_ref[0, :, i * w:(i + 1) * w] = x.astype(o_ref.dtype)

    put(RW_R, r)
    put(RW_V, v)
    put(RW_KK, kkn)
    ksum = None
    for d in range(2):
        lw = -math.exp(-0.5) * _sigmoid(w0_ref[d:d + 1, :] + wraw[:, d * w:(d + 1) * w])
        lw_out_ref[0, :, d * w:(d + 1) * w] = lw
        a = _sigmoid(a0_ref[d:d + 1, :] + araw[:, d * w:(d + 1) * w])
        kd = k * (1.0 + (a - 1.0) * ka_ref[...])
        put(RW_KD + 2 * d, kd)
        put(RW_BD + 2 * d, kkn * a)
        ksum = kd if ksum is None else ksum + kd
    put(RW_G, g)
    put(RW_BONUS, _x_dot01(r * rk_ref[...] * ksum, ones) * v)


def _rw_prep(z, lp):
    bsz, length, fin = z.shape
    tr = _tile(length, 256, 8)
    nt = length // tr
    rb = tr // 8
    nh = length // 8
    full = lambda a: pl.BlockSpec(a.shape, lambda b, j: (0,) * a.ndim)
    params = (lp['rw_mu'], lp['rw_w0'], lp['rw_wup'], lp['rw_a0'], lp['rw_aup'], lp['rw_gup'],
              lp['rw_kk'], lp['rw_ka'], lp['rw_rk'], lp['ones_hd'])
    return pl.pallas_call(
        functools.partial(_rw_prep_kernel, nt=nt),
        grid=(bsz, nt),
        in_specs=[pl.BlockSpec((1, tr, fin), lambda b, j: (b, j, 0)),
                  pl.BlockSpec((1, 8, fin), lambda b, j: (b, jnp.maximum(j * rb - 1, 0), 0)),
                  pl.BlockSpec((1, 8, fin), lambda b, j: (b, jnp.minimum((j + 1) * rb, nh - 1), 0))]
                 + [full(a) for a in params],
        out_specs=[pl.BlockSpec((1, tr, RW_NARR * BR_W), lambda b, j: (b, j, 0)),
                   pl.BlockSpec((1, tr, 2 * BR_W), lambda b, j: (b, j, 0))],
        out_shape=[jax.ShapeDtypeStruct((bsz, length, RW_NARR * BR_W), BF16),
                   jax.ShapeDtypeStruct((bsz, length, 2 * BR_W), F32)],
        compiler_params=_cp("parallel", "parallel"),
        name="rwkv_prep",
    )(z, z, z, *params)


def _rw_chunks(dir_refs, o_refs, s_scr, nb):
    cs = RW_CHUNK
    gw = RW_GW
    ng = RW_HEADS // RW_GROUP
    ri = lax.broadcasted_iota(jnp.int32, (gw, gw), 0)
    ci = lax.broadcasted_iota(jnp.int32, (gw, gw), 1)
    head_mask = (ri >> 6) == (ci >> 6)
    tw = lax.broadcasted_iota(jnp.int32, (cs, gw), 0)
    iw = lax.broadcasted_iota(jnp.int32, (cs, gw), 1) & (cs - 1)
    eye = jnp.where(iw == tw, 1.0, 0.0)
    same = {s: (tw >> int(math.log2(s))) == (iw >> int(math.log2(s))) for s in (8, 16, 32)}
    ti = lax.broadcasted_iota(jnp.int32, (cs, cs), 0)
    ii = lax.broadcasted_iota(jnp.int32, (cs, cs), 1)

    def expand(x):
        xb = x.astype(BF16)
        return jnp.where(head_mask, jnp.concatenate([xb] * RW_GROUP, axis=0), jnp.zeros_like(xb[0:1, 0:1]))

    def mm(a, b):
        return _dot(a, expand(b))

    chains = []
    for bb in range(nb):
        for d, (r_ref, v_ref, kk_ref, lw_ref, k_ref, b_ref) in enumerate(dir_refs):
            sgn = 1 - 2 * d
            rel = (iw - tw) * sgn
            tri = jnp.where((ii - ti) * sgn <= 0, 1.0, 0.0).astype(BF16)
            lw = lw_ref[bb]
            cum = _dot01(tri, lw)
            tot = jnp.sum(lw, axis=0, keepdims=True)
            kd, bd = k_ref[bb].astype(F32), b_ref[bb].astype(F32)
            qt = kk_ref[bb].astype(F32) * jnp.exp(cum - lw)
            rt = r_ref[bb].astype(F32) * jnp.exp(cum)
            e_neg = jnp.exp(-cum)
            e_end = jnp.exp(tot - cum)
            kt = kd * e_neg
            bt = bd * e_neg
            kh = kd * e_end
            bh = bd * e_end
            p_tot = jnp.exp(tot)
            v = v_ref[bb].astype(F32)
            for g in range(ng):
                sl = slice(g * gw, (g + 1) * gw)
                chains.append(dict(bb=bb, d=d, g=g, sl=sl, strict=rel < 0, incl=rel <= 0,
                                   q=qt[:, sl], r=rt[:, sl], v=v[:, sl], kt=kt[:, sl],
                                   bt=bt[:, sl], kh=kh[:, sl], bh=bh[:, sl], p_tot=p_tot[:, sl],
                                   s0=s_scr[bb, d, g]))

    for c in chains:
        c['kte'], c['bte'], c['ve'] = expand(c['kt']), expand(c['bt']), expand(c['v'])
        c['s0b'] = c['s0'].astype(BF16)
    for c in chains:
        c['lmat'] = jnp.where(c['strict'], _dot_nt(c['q'], c['bte']), 0.0)
        c['ld'] = jnp.where(same[8], c['lmat'], 0.0)
    for c in chains:
        c['ld2'] = mm(c['ld'], c['ld'])
    for c in chains:
        c['a_qk'] = jnp.where(c['strict'], _dot_nt(c['q'], c['kte']), 0.0)
    for c in chains:
        c['ld4'] = mm(c['ld2'], c['ld2'])
    for c in chains:
        c['x'] = mm(eye - c['ld'], eye + c['ld2'])
    for c in chains:
        c['a_rk'] = jnp.where(c['incl'], _dot_nt(c['r'], c['kte']), 0.0)
    for c in chains:
        c['x'] = mm(c['x'], eye + c['ld4'])
    for c in chains:
        c['a_rb'] = jnp.where(c['incl'], _dot_nt(c['r'], c['bte']), 0.0)
    for c in chains:
        c['rhs'] = _dot_nt(c['q'], c['s0b']) + _dot(c['a_qk'], c['ve'])
    for s in (8, 16, 32):
        off_mask = jnp.logical_not(same[s])
        if 2 * s < cs:
            off_mask = jnp.logical_and(same[2 * s], off_mask)
        for c in chains:
            c['t'] = mm(c['x'], jnp.where(off_mask, c['lmat'], 0.0))
        if s == 8:
            for c in chains:
                c['o'] = _dot_nt(c['r'], c['s0b']) + _dot(c['a_rk'], c['ve'])
        for c in chains:
            c['x'] = c['x'] - mm(c['t'], c['x'])
    for c in chains:
        c['sa'] = mm(c['x'], c['rhs'])
    for c in chains:
        o_refs[c['d']][c['bb'], :, c['sl']] = c['o'] - mm(c['a_rb'], c['sa'])
    for c in chains:
        upd = _dot_tn(jnp.concatenate([c['v'], c['sa']], axis=0),
                      jnp.concatenate([c['kh'], -c['bh']], axis=0))
        s_scr[c['bb'], c['d'], c['g']] = c['s0'] * c['p_tot'] + jnp.where(head_mask, upd, 0.0)


def _rw_scan_kernel(*refs, nc, nb):
    fwd, bwd = refs[0:6], refs[6:12]
    s0_ref, of_ref, ob_ref, sfin_ref, s_scr = refs[12:]
    c = pl.program_id(1)

    @pl.when(c == 0)
    def _():
        s_scr[...] = s0_ref[...]

    _rw_chunks((fwd, bwd), (of_ref, ob_ref), s_scr, nb)

    @pl.when(c == nc - 1)
    def _():
        sfin_ref[...] = s_scr[...]


def _rw_scan(rwp, lwp, s_init):
    bsz, length, _ = rwp.shape
    cs = RW_CHUNK
    nc = length // cs
    ng = RW_HEADS // RW_GROUP
    nb = RW_BATCH if bsz % RW_BATCH == 0 else 1

    def arr(d, i):
        if d == 0:
            return pl.BlockSpec((nb, cs, BR_W), lambda b, c: (b, c, i))
        return pl.BlockSpec((nb, cs, BR_W), lambda b, c: (b, nc - 1 - c, i))

    def specs(d):
        return [arr(d, RW_R), arr(d, RW_V), arr(d, RW_KK), arr(d, d),
                arr(d, RW_KD + 2 * d), arr(d, RW_BD + 2 * d)]

    def operands():
        return [rwp, rwp, rwp, lwp, rwp, rwp]

    sspec = pl.BlockSpec((nb, 2, ng, RW_GW, RW_GW), lambda b, c: (b, 0, 0, 0, 0))
    oshape = jax.ShapeDtypeStruct((bsz, length, BR_W), F32)
    return pl.pallas_call(
        functools.partial(_rw_scan_kernel, nc=nc, nb=nb),
        grid=(bsz // nb, nc),
        in_specs=specs(0) + specs(1) + [sspec],
        out_specs=[pl.BlockSpec((nb, cs, BR_W), lambda b, c: (b, c, 0)),
                   pl.BlockSpec((nb, cs, BR_W), lambda b, c: (b, nc - 1 - c, 0)), sspec],
        out_shape=[oshape, oshape, jax.ShapeDtypeStruct((bsz, 2, ng, RW_GW, RW_GW), F32)],
        scratch_shapes=[pltpu.VMEM((nb, 2, ng, RW_GW, RW_GW), F32)],
        compiler_params=_cp("parallel", "arbitrary"),
        name="rwkv_scan",
    )(*operands(), *operands(), s_init)


def _rw_readout_kernel(of_ref, ob_ref, g_ref, bonus_ref, lng_ref, lnb_ref, mean_ref, out_ref):
    o = of_ref[0] + ob_ref[0]
    mean_m = mean_ref[...]
    mu = _x_dot01(o, mean_m)
    dlt = o - mu
    var = _x_dot01(dlt * dlt, mean_m)
    on = dlt * lax.rsqrt(var + RW_GN_EPS) * lng_ref[...] + lnb_ref[...]
    out_ref[0] = ((on + bonus_ref[0].astype(F32)) * g_ref[0].astype(F32)).astype(out_ref.dtype)


def _rw_readout(o_f, o_b, rwp, lp):
    bsz, length, w = o_f.shape
    tr = _tile(length, 256, 16)
    ospec = pl.BlockSpec((1, tr, w), lambda b, j: (b, j, 0))
    return pl.pallas_call(
        _rw_readout_kernel,
        grid=(bsz, length // tr),
        in_specs=[ospec, ospec,
                  pl.BlockSpec((1, tr, w), lambda b, j: (b, j, RW_G)),
                  pl.BlockSpec((1, tr, w), lambda b, j: (b, j, RW_BONUS)),
                  pl.BlockSpec((1, w), lambda b, j: (0, 0)),
                  pl.BlockSpec((1, w), lambda b, j: (0, 0)),
                  pl.BlockSpec((w, w), lambda b, j: (0, 0))],
        out_specs=ospec,
        out_shape=jax.ShapeDtypeStruct((bsz, length, w), BF16),
        compiler_params=_cp("parallel", "parallel"),
        name="rwkv_readout",
    )(o_f, o_b, rwp, rwp, lp['rw_lng'], lp['rw_lnb'], lp['mean_hd'])


def _ssd_prep_kernel(z_ref, zp_ref, zn_ref, cw_ref, cb_ref, dtb_ref, xbc_ref, dt_ref, *, nt):
    j = pl.program_id(1)
    x = z_ref[0, :, BR_W:BR_W + SSD_XBC]
    prow = jnp.where(j > 0, zp_ref[0, 7:8, BR_W:BR_W + SSD_XBC], 0.0)
    nrow = jnp.where(j < nt - 1, zn_ref[0, 0:1, BR_W:BR_W + SSD_XBC], 0.0)
    xbc_ref[0] = _silu(_conv3(x, prow, nrow, cw_ref, cb_ref))
    dt_raw = z_ref[0, :, BR_W + SSD_XBC:BR_W + SSD_XBC + LANES]
    dt_ref[0, 0] = _softplus(dt_raw + dtb_ref[0:1, :])
    dt_ref[0, 1] = _softplus(pltpu.roll(dt_raw, LANES - SSD_HEADS, 1) + dtb_ref[1:2, :])


def _ssd_prep(z, lp):
    bsz, length, fin = z.shape
    tr = _tile(length, 256, 8)
    nt = length // tr
    rb = tr // 8
    nh = length // 8
    return pl.pallas_call(
        functools.partial(_ssd_prep_kernel, nt=nt),
        grid=(bsz, nt),
        in_specs=[pl.BlockSpec((1, tr, fin), lambda b, j: (b, j, 0)),
                  pl.BlockSpec((1, 8, fin), lambda b, j: (b, jnp.maximum(j * rb - 1, 0), 0)),
                  pl.BlockSpec((1, 8, fin), lambda b, j: (b, jnp.minimum((j + 1) * rb, nh - 1), 0)),
                  pl.BlockSpec((3, SSD_XBC), lambda b, j: (0, 0)),
                  pl.BlockSpec((1, SSD_XBC), lambda b, j: (0, 0)),
                  pl.BlockSpec((2, LANES), lambda b, j: (0, 0))],
        out_specs=[pl.BlockSpec((1, tr, SSD_XBC), lambda b, j: (b, j, 0)),
                   pl.BlockSpec((1, 2, tr, LANES), lambda b, j: (b, 0, j, 0))],
        out_shape=[jax.ShapeDtypeStruct((bsz, length, SSD_XBC), F32),
                   jax.ShapeDtypeStruct((bsz, 2, length, LANES), F32)],
        compiler_params=_cp("parallel", "parallel"),
        name="ssd_prep",
    )(z, z, z, lp['ssd_cw'], lp['ssd_cb'], lp['ssd_dtb'])


def _ssd_chunks(dir_refs, a_ref, y_refs, h_scr, nb):
    q = SSD_CHUNK
    n = SSD_STATE
    rep = SSD_HEADS // SSD_GROUPS
    gw = SSD_GW
    ti = lax.broadcasted_iota(jnp.int32, (q, q), 0)
    ii = lax.broadcasted_iota(jnp.int32, (q, q), 1)
    ri = lax.broadcasted_iota(jnp.int32, (rep * q, gw), 0)
    ci = lax.broadcasted_iota(jnp.int32, (rep * q, gw), 1)
    head_mask = (ri >> int(math.log2(q))) == (ci >> int(math.log2(HD)))

    def per_head(x, h0):
        return jnp.concatenate([jnp.broadcast_to(x[:, h0 + h:h0 + h + 1], (x.shape[0], HD))
                                for h in range(rep)], axis=1)

    work = []
    for bb in range(nb):
        for d, (xbc_ref, dt_ref) in enumerate(dir_refs):
            incl = (ii - ti) * (1 - 2 * d) <= 0
            tri = jnp.where(incl, 1.0, 0.0).astype(BF16)
            dt = dt_ref[bb, 0]
            dta = dt * a_ref[d]
            acs = _dot01(tri, dta)
            acs_t = acs.T
            tot = jnp.sum(dta, axis=0, keepdims=True)
            e_acs = jnp.exp(acs)
            e_end = jnp.exp(tot - acs)
            e_tot = jnp.exp(tot)
            xbc = xbc_ref[bb]
            for g in range(SSD_GROUPS):
                h0 = g * rep
                bg = xbc[:, BR_W + g * n:BR_W + (g + 1) * n]
                cg = xbc[:, BR_W + SSD_GROUPS * n + g * n:BR_W + SSD_GROUPS * n + (g + 1) * n]
                xdt = xbc[:, g * gw:(g + 1) * gw] * per_head(dt, h0)
                dec = jnp.concatenate(
                    [jnp.exp(jnp.where(incl, acs[:, h0 + h:h0 + h + 1] - acs_t[h0 + h:h0 + h + 1, :], NEG))
                     for h in range(rep)], axis=1)
                work.append(dict(bb=bb, d=d, g=g, bg=bg, cg=cg, xdt=xdt, dec=dec,
                                 ht=h_scr[bb, d, g], e_acs=per_head(e_acs, h0),
                                 e_end=per_head(e_end, h0), e_tot=per_head(e_tot, h0)))
    for w in work:
        w['cb'] = _dot_nt(w['cg'], w['bg'])
    for w in work:
        w['yoff'] = _dot(w['cg'], w['ht'])
    for w in work:
        w['st'] = _dot_tn(w['bg'], w['xdt'] * w['e_end'])
    for w in work:
        xe = jnp.where(head_mask, jnp.concatenate([w['xdt']] * rep, axis=0), 0.0)
        scores = jnp.concatenate([w['cb']] * rep, axis=1) * w['dec']
        y = _dot(scores, xe) + w['e_acs'] * w['yoff']
        y_refs[w['d']][w['bb'], :, w['g'] * gw:(w['g'] + 1) * gw] = y
        h_scr[w['bb'], w['d'], w['g']] = w['ht'] * w['e_tot'] + w['st']


def _ssd_scan_kernel(xf_ref, dtf_ref, xb_ref, dtb_ref, a_ref, h0_ref, yf_ref, yb_ref, hfin_ref,
                     h_scr, *, nc, nb):
    c = pl.program_id(1)

    @pl.when(c == 0)
    def _():
        h_scr[...] = h0_ref[...]

    _ssd_chunks(((xf_ref, dtf_ref), (xb_ref, dtb_ref)), a_ref, (yf_ref, yb_ref), h_scr, nb)

    @pl.when(c == nc - 1)
    def _():
        hfin_ref[...] = h_scr[...]


def _ssd_scan(xbc, dt, a_neg, h_init):
    bsz, length, _ = xbc.shape
    q = SSD_CHUNK
    nc = length // q
    nb = RW_BATCH if bsz % RW_BATCH == 0 else 1
    hspec = pl.BlockSpec((nb, 2, SSD_GROUPS, SSD_STATE, SSD_GW), lambda b, c: (b, 0, 0, 0, 0))
    oshape = jax.ShapeDtypeStruct((bsz, length, BR_W), F32)
    return pl.pallas_call(
        functools.partial(_ssd_scan_kernel, nc=nc, nb=nb),
        grid=(bsz // nb, nc),
        in_specs=[pl.BlockSpec((nb, q, SSD_XBC), lambda b, c: (b, c, 0)),
                  pl.BlockSpec((nb, 1, q, LANES), lambda b, c: (b, 0, c, 0)),
                  pl.BlockSpec((nb, q, SSD_XBC), lambda b, c: (b, nc - 1 - c, 0)),
                  pl.BlockSpec((nb, 1, q, LANES), lambda b, c: (b, 1, nc - 1 - c, 0)),
                  pl.BlockSpec((2, 1, LANES), lambda b, c: (0, 0, 0)),
                  hspec],
        out_specs=[pl.BlockSpec((nb, q, BR_W), lambda b, c: (b, c, 0)),
                   pl.BlockSpec((nb, q, BR_W), lambda b, c: (b, nc - 1 - c, 0)), hspec],
        out_shape=[oshape, oshape,
                   jax.ShapeDtypeStruct((bsz, 2, SSD_GROUPS, SSD_STATE, SSD_GW), F32)],
        scratch_shapes=[pltpu.VMEM((nb, 2, SSD_GROUPS, SSD_STATE, SSD_GW), F32)],
        compiler_params=_cp("parallel", "arbitrary"),
        name="ssd_scan",
    )(xbc, dt, xbc, dt, a_neg, h_init)


def _ssd_readout_kernel(yf_ref, yb_ref, xs_ref, z_ref, dsk_ref, ng_ref, out_ref):
    y = yf_ref[0] + yb_ref[0] + dsk_ref[...] * xs_ref[0]
    y = y * _silu(z_ref[0])
    y = y * lax.rsqrt(jnp.mean(y * y, axis=-1, keepdims=True) + NORM_EPS) * ng_ref[...]
    out_ref[0] = y.astype(out_ref.dtype)


def _ssd_readout(y_f, y_b, xbc, z, lp):
    bsz, length, w = y_f.shape
    tr = _tile(length, 256, 16)
    ospec = pl.BlockSpec((1, tr, w), lambda b, j: (b, j, 0))
    return pl.pallas_call(
        _ssd_readout_kernel,
        grid=(bsz, length // tr),
        in_specs=[ospec, ospec, ospec, ospec,
                  pl.BlockSpec((1, w), lambda b, j: (0, 0)),
                  pl.BlockSpec((1, w), lambda b, j: (0, 0))],
        out_specs=ospec,
        out_shape=jax.ShapeDtypeStruct((bsz, length, w), BF16),
        compiler_params=_cp("parallel", "parallel"),
        name="ssd_readout",
    )(y_f, y_b, xbc, z, lp['ssd_dskip'], lp['ssd_ng'])


def _da_proj_kernel(h_ref, w_ref, cos_ref, sin_ref, q_ref, k_ref, v_ref, *, rope):
    w = BR_W
    z = jnp.dot(h_ref[0], w_ref[...], preferred_element_type=F32)

    def rot(x):
        if not rope:
            return x
        lane = lax.broadcasted_iota(jnp.int32, (1, LANES), 1)
        first = (lane & 31) < 16
        outs = []
        for cb in range(w // LANES):
            xb = x[:, cb * LANES:(cb + 1) * LANES]
            sw = jnp.where(first, pltpu.roll(xb, LANES - 16, 1), pltpu.roll(xb, 16, 1))
            outs.append(xb * cos_ref[:, cb * LANES:(cb + 1) * LANES]
                        + sw * sin_ref[:, cb * LANES:(cb + 1) * LANES])
        return jnp.concatenate(outs, axis=1)

    q_ref[0] = (rot(z[:, 0:w]) * (HD ** -0.5)).astype(q_ref.dtype)
    k_ref[0] = rot(z[:, w:2 * w]).astype(k_ref.dtype)
    v_ref[0] = z[:, 2 * w:3 * w].astype(v_ref.dtype)


def _da_proj(h, w_da, cos, sin, rope):
    bsz, length, d = h.shape
    tr = _tile(length, 512, 16)
    w = BR_W
    ospec = pl.BlockSpec((1, tr, w), lambda b, j: (b, j, 0))
    oshape = jax.ShapeDtypeStruct((bsz, length, w), BF16)
    return pl.pallas_call(
        functools.partial(_da_proj_kernel, rope=rope),
        grid=(bsz, length // tr),
        in_specs=[pl.BlockSpec((1, tr, d), lambda b, j: (b, j, 0)),
                  pl.BlockSpec((d, DA_IN), lambda b, j: (0, 0)),
                  pl.BlockSpec((tr, w), lambda b, j: (j, 0)),
                  pl.BlockSpec((tr, w), lambda b, j: (j, 0))],
        out_specs=[ospec, ospec, ospec],
        out_shape=[oshape, oshape, oshape],
        compiler_params=_cp("parallel", "parallel"),
        name="da_proj",
    )(h, w_da, cos, sin)


def _da_attn_kernel(*refs, nsrc, lam_init):
    lam_ref, q_ref = refs[0], refs[1]
    kv = refs[2:2 + 2 * nsrc]
    g_ref, o_ref = refs[2 + 2 * nsrc], refs[3 + 2 * nsrc]
    q = q_ref[0]
    lane = lax.broadcasted_iota(jnp.int32, (1, LANES), 1)
    zero = jnp.zeros_like(q)
    lam = lam_ref[0:1, 0:1]
    probs = []
    for m in range(2):
        qm = jnp.where((lane < HD) if m == 0 else (lane >= HD), q, zero)
        ss = [_dot_nt(qm, kv[2 * i][0]) for i in range(nsrc)]
        mx = functools.reduce(jnp.maximum, [jnp.max(s, axis=-1, keepdims=True) for s in ss])
        es = [jnp.exp(s - mx) for s in ss]
        den = functools.reduce(lambda a, b: a + b, [jnp.sum(e, axis=-1, keepdims=True) for e in es])
        inv = 1.0 / den
        probs.append([e * inv for e in es])
    o = None
    for i in range(nsrc):
        a = probs[0][i] - lam * probs[1][i]
        t = _dot(a, kv[2 * i + 1][0])
        o = t if o is None else o + t
    y = o * lax.rsqrt(jnp.mean(o * o, axis=-1, keepdims=True) + DA_SUBLN_EPS) * g_ref[...]
    o_ref[0] = (y * (1.0 - lam_init)).astype(o_ref.dtype)


def _da_attn(q, ks, vs, lam, subln_g, lam_init):
    bsz, lq, _ = q.shape
    tq = _tile(lq, 256, 16)
    nsrc = len(ks)
    in_specs = [pl.BlockSpec((1, LANES), lambda b, h, j: (0, 0)),
                pl.BlockSpec((1, tq, LANES), lambda b, h, j: (b, j, h))]
    args = [lam, q]
    for k, v in zip(ks, vs):
        lk = k.shape[1]
        in_specs.append(pl.BlockSpec((1, lk, LANES), lambda b, h, j: (b, 0, h)))
        in_specs.append(pl.BlockSpec((1, lk, LANES), lambda b, h, j: (b, 0, h)))
        args += [k, v]
    in_specs.append(pl.BlockSpec((1, LANES), lambda b, h, j: (0, 0)))
    args.append(subln_g)
    return pl.pallas_call(
        functools.partial(_da_attn_kernel, nsrc=nsrc, lam_init=lam_init),
        grid=(bsz, DA_HEADS, lq // tq),
        in_specs=in_specs,
        out_specs=pl.BlockSpec((1, tq, LANES), lambda b, h, j: (b, j, h)),
        out_shape=jax.ShapeDtypeStruct((bsz, lq, BR_W), BF16),
        compiler_params=_cp("parallel", "parallel", "parallel"),
        name="diff_attn",
    )(*args)


def _na_kb(i, rows):
    return jnp.clip(i * NA_QROWS - NA_WIN_R // 2, 0, rows - NA_KROWS)


def _na_attn_kernel(q_ref, k_ref, v_ref, kc_ref, vc_ref, bias_ref, o_ref, *, rows):
    i = pl.program_id(2)
    start = pl.multiple_of(_na_kb(i, rows) * GRID_W, GRID_W)
    nk = NA_KROWS * GRID_W
    q = q_ref[0] * (HD ** -0.5)
    kw = k_ref[0, pl.ds(start, nk), :]
    vw = v_ref[0, pl.ds(start, nk), :]
    kc = kc_ref[0]
    vc = vc_ref[0]
    lane = lax.broadcasted_iota(jnp.int32, (1, LANES), 1)
    outs = []
    for h in range(2):
        qm = jnp.where((lane < HD) if h == 0 else (lane >= HD), q, 0.0)
        sw = _dot_nt(qm, kw) + bias_ref[h, 0]
        sc = _dot_nt(qm, kc)
        mx = jnp.maximum(jnp.max(sw, axis=-1, keepdims=True), jnp.max(sc, axis=-1, keepdims=True))
        ew = jnp.exp(sw - mx)
        ec = jnp.exp(sc - mx)
        inv = 1.0 / (jnp.sum(ew, axis=-1, keepdims=True) + jnp.sum(ec, axis=-1, keepdims=True))
        outs.append((_dot(ew, vw) + _dot(ec, vc)) * inv)
    o_ref[0] = jnp.where(lane < HD, outs[0], outs[1]).astype(o_ref.dtype)


def _na_attn(z_l, z_c, bias, pat_id):
    bsz, length, _ = z_l.shape
    lc = z_c.shape[1]
    rows = length // GRID_W
    tq = NA_QROWS * GRID_W
    nblk = length // tq
    nkb = BR_W // LANES

    def pat(i):
        out = pat_id[0]
        for j in range(1, nblk):
            if pat_id[j] != pat_id[j - 1]:
                out = out + jnp.where(i >= j, pat_id[j] - pat_id[j - 1], 0)
        return out

    return pl.pallas_call(
        functools.partial(_na_attn_kernel, rows=rows),
        grid=(bsz, NA_HEADS // 2, nblk),
        in_specs=[pl.BlockSpec((1, tq, LANES), lambda b, p, i: (b, i, p)),
                  pl.BlockSpec((1, length, LANES), lambda b, p, i: (b, 0, nkb + p)),
                  pl.BlockSpec((1, length, LANES), lambda b, p, i: (b, 0, 2 * nkb + p)),
                  pl.BlockSpec((1, lc, LANES), lambda b, p, i: (b, 0, nkb + p)),
                  pl.BlockSpec((1, lc, LANES), lambda b, p, i: (b, 0, 2 * nkb + p)),
                  pl.BlockSpec((2, 1, tq, NA_KROWS * GRID_W), lambda b, p, i: (p, pat(i), 0, 0))],
        out_specs=pl.BlockSpec((1, tq, LANES), lambda b, p, i: (b, i, p)),
        out_shape=jax.ShapeDtypeStruct((bsz, length, BR_W), BF16),
        compiler_params=_cp("parallel", "parallel", "parallel"),
        name="na_attn",
    )(z_l, z_l, z_l, z_c, z_c, bias)


def _ctx_attn_kernel(q_ref, k_ref, v_ref, o_ref):
    q = q_ref[0] * (HD ** -0.5)
    k = k_ref[0]
    v = v_ref[0]
    lane = lax.broadcasted_iota(jnp.int32, (1, LANES), 1)
    outs = []
    for h in range(2):
        qm = jnp.where((lane < HD) if h == 0 else (lane >= HD), q, 0.0)
        s = _dot_nt(qm, k)
        e = jnp.exp(s - jnp.max(s, axis=-1, keepdims=True))
        outs.append(_dot(e, v) * (1.0 / jnp.sum(e, axis=-1, keepdims=True)))
    o_ref[0] = jnp.where(lane < HD, outs[0], outs[1]).astype(o_ref.dtype)


def _ctx_attn(z_c):
    bsz, lc, _ = z_c.shape
    nkb = BR_W // LANES
    return pl.pallas_call(
        _ctx_attn_kernel,
        grid=(bsz, NA_HEADS // 2),
        in_specs=[pl.BlockSpec((1, lc, LANES), lambda b, p: (b, 0, p)),
                  pl.BlockSpec((1, lc, LANES), lambda b, p: (b, 0, nkb + p)),
                  pl.BlockSpec((1, lc, LANES), lambda b, p: (b, 0, 2 * nkb + p))],
        out_specs=pl.BlockSpec((1, lc, LANES), lambda b, p: (b, 0, p)),
        out_shape=jax.ShapeDtypeStruct((bsz, lc, BR_W), BF16),
        compiler_params=_cp("parallel", "parallel"),
        name="ctx_attn",
    )(z_c, z_c, z_c)


def _rope_tables(length):
    n_freq = HD // 4
    t = np.arange(length)
    pos = np.stack([t // GRID_W, t % GRID_W], axis=-1).astype(np.float32)
    inv = (ROPE_BASE ** (-np.arange(n_freq, dtype=np.float32) / n_freq)).astype(np.float32)
    lane = np.arange(BR_W)
    which = (lane % HD) // (HD // 2)
    ang = pos[:, which] * inv[lane % n_freq][None, :]
    sign = np.where((lane % (HD // 2)) < n_freq, -1.0, 1.0).astype(np.float32)
    return jnp.asarray(np.cos(ang), F32), jnp.asarray(np.sin(ang) * sign[None, :], F32)


def _na_bias(rpb, length):
    rows = length // GRID_W
    wr = min(NA_WIN_R, rows)
    nblk = rows // NA_QROWS
    qr = np.arange(rows)
    rstart = np.clip(qr - wr // 2, 0, rows - wr)
    kb = np.clip(np.arange(nblk) * NA_QROWS - NA_WIN_R // 2, 0, rows - NA_KROWS)
    qrow = (np.arange(nblk)[:, None] * NA_QROWS + np.arange(NA_QROWS)[None, :])
    krow = kb[:, None] + np.arange(NA_KROWS)[None, :]
    dr = krow[:, None, :] - qrow[:, :, None] + NA_WIN_R - 1
    rvalid = (krow[:, None, :] >= rstart[qrow][:, :, None]) & (krow[:, None, :] < rstart[qrow][:, :, None] + wr)
    dr = np.where(rvalid, dr, -1)
    pats, pat_id = [], []
    for i in range(nblk):
        for j, pdr in enumerate(pats):
            if np.array_equal(pdr, dr[i]):
                pat_id.append(j)
                break
        else:
            pat_id.append(len(pats))
            pats.append(dr[i])
    pdr = np.stack(pats)
    n_dr = 2 * NA_WIN_R - 1
    oh_dr = (pdr[..., None] == np.arange(n_dr)).astype(np.float32)
    cid = np.arange(GRID_W)
    cstart = np.clip(cid - NA_WIN_C // 2, 0, GRID_W - NA_WIN_C)
    in_win = (cid[None, :] >= cstart[:, None]) & (cid[None, :] < cstart[:, None] + NA_WIN_C)
    ci = np.clip(cid[None, :] - cid[:, None], -(NA_WIN_C - 1), NA_WIN_C - 1) + NA_WIN_C - 1
    n_ci = 2 * NA_WIN_C - 1
    oh_ci = (ci[..., None] == np.arange(n_ci)).astype(np.float32)
    t1 = jnp.einsum('hdc,qkc->hdqk', rpb.astype(F32), jnp.asarray(oh_ci),
                    precision=lax.Precision.HIGHEST)
    big = jnp.einsum('brkd,hdxy->hbrxky', jnp.asarray(oh_dr), t1,
                     precision=lax.Precision.HIGHEST)
    valid = (pdr >= 0)[:, :, None, :, None] & in_win[None, None, :, None, :]
    big = jnp.where(jnp.asarray(valid)[None], big, NEG)
    big = big.reshape(rpb.shape[0], len(pats), NA_QROWS * GRID_W, NA_KROWS * GRID_W)
    return big, tuple(pat_id)


def _block_diag(blocks):
    n = len(blocks)
    r, c = blocks[0].shape
    out = jnp.zeros((n * r, n * c), blocks[0].dtype)
    for i, blk in enumerate(blocks):
        out = out.at[i * r:(i + 1) * r, i * c:(i + 1) * c].set(blk)
    return out


def _layer_params(p, i):
    w = BR_W
    w_in = p['w_in'][i]
    o1, o2, o3 = RW_IN, RW_IN + SSD_IN, RW_IN + SSD_IN + DA_IN
    d = w_in.shape[0]
    w_ssd = jnp.concatenate([w_in[:, o1:o2], jnp.zeros((d, SSD_IN_PAD - SSD_IN), w_in.dtype)], axis=1)
    head_id = np.arange(w) // HD
    ones_hd = jnp.asarray((head_id[:, None] == head_id[None, :]).astype(np.float32), BF16)
    lam_p = p['da_lambda'][i].astype(F32)
    lam_init = 0.8 - 0.6 * math.exp(-0.3 * i)
    lam = jnp.exp(jnp.sum(lam_p[0] * lam_p[1])) - jnp.exp(jnp.sum(lam_p[2] * lam_p[3])) + lam_init
    dtb = p['ssd_dt_bias'][i]
    return {
        'w_rw': w_in[:, :o1].astype(BF16), 'w_ssd': w_ssd.astype(BF16),
        'w_da': w_in[:, o2:o3].astype(BF16), 'w_na': w_in[:, o3:].astype(BF16),
        'rw_mu': p['rw_mu'][i], 'rw_w0': p['rw_w0'][i],
        'rw_wup': _block_diag([p['rw_w_up'][i, 0], p['rw_w_up'][i, 1]]).astype(BF16),
        'rw_a0': p['rw_a0'][i],
        'rw_aup': _block_diag([p['rw_a_up'][i, 0], p['rw_a_up'][i, 1]]).astype(BF16),
        'rw_gup': p['rw_g_up'][i].astype(BF16),
        'rw_kk': p['rw_k_k'][i].reshape(1, w), 'rw_ka': p['rw_k_a'][i].reshape(1, w),
        'rw_rk': p['rw_r_k'][i].reshape(1, w),
        'rw_lng': p['rw_ln_g'][i].reshape(1, w), 'rw_lnb': p['rw_ln_b'][i].reshape(1, w),
        'ones_hd': ones_hd, 'mean_hd': (ones_hd.astype(F32) / HD).astype(BF16),
        'ssd_cw': p['ssd_conv_w'][i], 'ssd_cb': p['ssd_conv_b'][i].reshape(1, SSD_XBC),
        'ssd_dtb': jnp.pad(dtb, ((0, 0), (0, LANES - SSD_HEADS))),
        'ssd_a': jnp.pad(-jnp.exp(p['ssd_a_log'][i].astype(F32)), ((0, 0), (0, LANES - SSD_HEADS))).reshape(2, 1, LANES),
        'ssd_dskip': jnp.repeat(p['ssd_d'][i], HD).reshape(1, w),
        'ssd_ng': p['ssd_norm_g'][i].reshape(1, w),
        'da_lam': jnp.broadcast_to(lam.reshape(1, 1), (1, LANES)).astype(F32),
        'da_lam_init': lam_init,
        'da_g': p['da_subln_g'][i].reshape(1, 2 * HD),
        'na_rpb': p['na_rpb'][i],
        'w_gate': p['w_gate'][i].astype(BF16),
        'gate_b': p['gate_b'][i][:, None, :],
        'w_br': p['w_br'][i].astype(BF16), 'w_out': p['w_out'][i].astype(BF16),
        'ffn_up': p['ffn_up'][i].astype(BF16), 'ffn_cw': p['ffn_conv_w'][i],
        'ffn_cb': p['ffn_conv_b'][i], 'ffn_down': p['ffn_down'][i].astype(BF16),
    }


def _project(h, lp):
    bsz, length, d = h.shape
    h2 = h.reshape(bsz * length, d)
    return tuple(_matmul(h2, lp[n], F32).reshape(bsz, length, -1)
                 for n in ('w_rw', 'w_ssd', 'w_na'))


def _ffn(x, h2, lp, gate):
    act = _ffn_up(h2, lp['ffn_up'], lp['ffn_cw'], lp['ffn_cb'])
    return _matmul_residual(act, lp['ffn_down'], x, gate)


def kernel(x, c, ctx, c_ctx, ada_w, ada_b, norm1_g, norm2_g, w_in, rw_mu, rw_w0, rw_w_up, rw_a0, rw_a_up, rw_g_up, rw_k_k, rw_k_a, rw_r_k, rw_ln_g, rw_ln_b, ssd_conv_w, ssd_conv_b, ssd_dt_bias, ssd_a_log, ssd_d, ssd_norm_g, da_lambda, da_subln_g, na_rpb, w_gate, gate_b, w_br, w_out, ffn_up, ffn_conv_w, ffn_conv_b, ffn_down, final_norm_g):
    p = dict(w_in=w_in, rw_mu=rw_mu, rw_w0=rw_w0, rw_w_up=rw_w_up, rw_a0=rw_a0, rw_a_up=rw_a_up,
             rw_g_up=rw_g_up, rw_k_k=rw_k_k, rw_k_a=rw_k_a, rw_r_k=rw_r_k, rw_ln_g=rw_ln_g,
             rw_ln_b=rw_ln_b, ssd_conv_w=ssd_conv_w, ssd_conv_b=ssd_conv_b, ssd_dt_bias=ssd_dt_bias,
             ssd_a_log=ssd_a_log, ssd_d=ssd_d, ssd_norm_g=ssd_norm_g, da_lambda=da_lambda,
             da_subln_g=da_subln_g, na_rpb=na_rpb, w_gate=w_gate, gate_b=gate_b, w_br=w_br,
             w_out=w_out, ffn_up=ffn_up, ffn_conv_w=ffn_conv_w, ffn_conv_b=ffn_conv_b,
             ffn_down=ffn_down)
    bsz, seq, d = x.shape
    depth = ada_w.shape[0]
    lctx = ctx.shape[1]
    mrows = -(-(bsz + 1) // 16) * 16
    cc = jnp.zeros((mrows, d), F32).at[:bsz].set(c).at[bsz].set(c_ctx)
    cos, sin = _rope_tables(seq)
    xl, xc = x, ctx
    for i in range(depth):
        last = i == depth - 1
        lp = _layer_params(p, i)
        mod = _modulation(cc, ada_w[i], ada_b[i]).reshape(mrows, 6, d)
        ml = [mod[:bsz, n][:, None, :] for n in range(6)]
        mc = [mod[bsz:bsz + 1, n][:, None, :] for n in range(6)]

        hl = _norm_mod(xl, norm1_g[i], ml[0], ml[1])
        hc = _norm_mod(xc, norm1_g[i], mc[0], mc[1])
        zl = _project(hl, lp)
        zc = _project(hc, lp)

        rp_c, lw_c = _rw_prep(zc[0], lp)
        rp_l, lw_l = _rw_prep(zl[0], lp)
        s_zero = jnp.zeros((bsz, 2, RW_HEADS // RW_GROUP, RW_GW, RW_GW), F32)
        of_c, ob_c, s_c = _rw_scan(rp_c, lw_c, s_zero)
        of_l, ob_l, _ = _rw_scan(rp_l, lw_l, s_c)
        a_l = _rw_readout(of_l, ob_l, rp_l, lp)

        xbc_c, dt_c = _ssd_prep(zc[1], lp)
        xbc_l, dt_l = _ssd_prep(zl[1], lp)
        h_zero = jnp.zeros((bsz, 2, SSD_GROUPS, SSD_STATE, SSD_GW), F32)
        yf_c, yb_c, hfin_c = _ssd_scan(xbc_c, dt_c, lp['ssd_a'], h_zero)
        yf_l, yb_l, _ = _ssd_scan(xbc_l, dt_l, lp['ssd_a'], hfin_c)
        b_l = _ssd_readout(yf_l, yb_l, xbc_l, zl[1], lp)

        q_l, k_l, v_l = _da_proj(hl, lp['w_da'], cos, sin, True)
        q_c, k_c, v_c = _da_proj(hc, lp['w_da'], cos, sin, False)
        c_l = _da_attn(q_l, [k_l, k_c], [v_l, v_c], lp['da_lam'], lp['da_g'], lp['da_lam_init'])

        d_l = _na_attn(zl[2], zc[2], *_na_bias(lp['na_rpb'], seq))

        merged = _gated_merge(hl.reshape(bsz * seq, d),
                              [t.reshape(bsz * seq, BR_W) for t in (a_l, b_l, c_l, d_l)],
                              lp['w_gate'], lp['gate_b'], lp['w_br']).reshape(bsz, seq, d)
        xl, hl2 = _matmul_residual_norm(merged, lp['w_out'], xl, ml[2], norm2_g[i], ml[3], ml[4])
        xl = _ffn(xl, hl2, lp, ml[5])

        if not last:
            a_c = _rw_readout(of_c, ob_c, rp_c, lp)
            b_c = _ssd_readout(yf_c, yb_c, xbc_c, zc[1], lp)
            c_c = _da_attn(q_c, [k_c], [v_c], lp['da_lam'], lp['da_g'], lp['da_lam_init'])
            d_c = _ctx_attn(zc[2])
            merged_c = _gated_merge(hc.reshape(bsz * lctx, d),
                                    [t.reshape(bsz * lctx, BR_W) for t in (a_c, b_c, c_c, d_c)],
                                    lp['w_gate'], lp['gate_b'], lp['w_br']).reshape(bsz, lctx, d)
            xc, hc2 = _matmul_residual_norm(merged_c, lp['w_out'], xc, mc[2], norm2_g[i], mc[3], mc[4])
            xc = _ffn(xc, hc2, lp, mc[5])
    return _final_norm(xl, final_norm_g)
```

```python
import functools
import math

import numpy as np
import jax
import jax.numpy as jnp
from jax import lax
from jax.experimental import pallas as pl
from jax.experimental.pallas import tpu as pltpu

F32 = jnp.float32
BF16 = jnp.bfloat16

GRID_W = 64
NORM_EPS = 1e-6
HD = 64
BR_W = 512
RW_HEADS = 8
RW_RANK = 64
RW_GATE_RANK = 128
RW_GN_EPS = 64e-5
RW_IN = 3 * BR_W + 4 * RW_RANK + RW_GATE_RANK
RW_CHUNK = 64
RW_NARR = 9
RW_R, RW_V, RW_KK, RW_KD, RW_BD, RW_G, RW_BONUS = 0, 1, 2, 3, 4, 7, 8
RW_GROUP = 2
RW_GW = RW_GROUP * HD
RW_BATCH = 2
SSD_HEADS = 8
SSD_GROUPS = 2
SSD_STATE = 128
SSD_XBC = BR_W + 2 * SSD_GROUPS * SSD_STATE
SSD_IN = BR_W + SSD_XBC + 2 * SSD_HEADS
SSD_IN_PAD = BR_W + SSD_XBC + 128
SSD_CHUNK = 128
SSD_GW = SSD_HEADS // SSD_GROUPS * HD
DA_HEADS = 4
DA_HPS = 2
DA_IN = 3 * BR_W
DA_SUBLN_EPS = 1e-5
NA_HEADS = 8
NA_IN = 3 * BR_W
NA_WIN_R = 8
NA_WIN_C = 16
NA_QROWS = 4
NA_KROWS = 12
ROPE_BASE = 10000.0
NEG = -1e30
LANES = 128
VMEM_LIMIT = 56 * 1024 * 1024


def _cp(*sem):
    return pltpu.CompilerParams(dimension_semantics=sem, vmem_limit_bytes=VMEM_LIMIT)


def _tile(n, pref, mult):
    t = min(n, pref)
    t -= t % mult
    while t >= mult:
        if n % t == 0:
            return t
        t -= mult
    return n


def _dot(a, b):
    return jnp.dot(a.astype(BF16), b.astype(BF16), preferred_element_type=F32)


def _dot_nt(a, b):
    return lax.dot_general(a.astype(BF16), b.astype(BF16), (((1,), (1,)), ((), ())),
                           preferred_element_type=F32)


def _dot_tn(a, b):
    return lax.dot_general(a.astype(BF16), b.astype(BF16), (((0,), (0,)), ((), ())),
                           preferred_element_type=F32)


def _split3(x):
    hi = x.astype(BF16)
    r1 = x - hi.astype(F32)
    mid = r1.astype(BF16)
    lo = (r1 - mid.astype(F32)).astype(BF16)
    return hi, mid, lo


def _dot01(m01, x):
    hi, mid, lo = _split3(x)
    return (jnp.dot(m01, hi, preferred_element_type=F32)
            + jnp.dot(m01, mid, preferred_element_type=F32)
            + jnp.dot(m01, lo, preferred_element_type=F32))


def _x_dot01(x, m01):
    hi, mid, lo = _split3(x)
    return (jnp.dot(hi, m01, preferred_element_type=F32)
            + jnp.dot(mid, m01, preferred_element_type=F32)
            + jnp.dot(lo, m01, preferred_element_type=F32))


def _sigmoid(x):
    return 1.0 / (1.0 + jnp.exp(-x))


def _silu(x):
    return x * _sigmoid(x)


def _softplus(x):
    return jnp.maximum(x, 0.0) + jnp.log(1.0 + jnp.exp(-jnp.abs(x)))


def _mod_kernel(c_ref, w_ref, b_ref, o_ref):
    a = _silu(c_ref[...])
    o_ref[...] = _dot(a, w_ref[...]) + b_ref[...]


def _modulation(cc, w, b):
    m, d = cc.shape
    n = w.shape[1]
    tn = _tile(n, 1024, LANES)
    return pl.pallas_call(
        _mod_kernel,
        grid=(n // tn,),
        in_specs=[pl.BlockSpec((m, d), lambda j: (0, 0)),
                  pl.BlockSpec((d, tn), lambda j: (0, j)),
                  pl.BlockSpec((1, tn), lambda j: (0, j))],
        out_specs=pl.BlockSpec((m, tn), lambda j: (0, j)),
        out_shape=jax.ShapeDtypeStruct((m, n), F32),
        compiler_params=_cp("parallel"),
        name="adaln_mod",
    )(cc, w, b.reshape(1, n))


def _norm_mod_kernel(x_ref, g_ref, sh_ref, sc_ref, o_ref):
    x = x_ref[0]
    y = x * lax.rsqrt(jnp.mean(x * x, axis=-1, keepdims=True) + NORM_EPS) * g_ref[...]
    o_ref[0] = (y * (1.0 + sc_ref[0]) + sh_ref[0]).astype(o_ref.dtype)


def _norm_mod(x, g, shift, scale):
    bsz, length, d = x.shape
    tr = _tile(length, 256, 16)
    bm = shift.shape[0]
    mod_map = (lambda b, j: (b, 0, 0)) if bm == bsz else (lambda b, j: (0, 0, 0))
    return pl.pallas_call(
        _norm_mod_kernel,
        grid=(bsz, length // tr),
        in_specs=[pl.BlockSpec((1, tr, d), lambda b, j: (b, j, 0)),
                  pl.BlockSpec((1, d), lambda b, j: (0, 0)),
                  pl.BlockSpec((1, 1, d), mod_map),
                  pl.BlockSpec((1, 1, d), mod_map)],
        out_specs=pl.BlockSpec((1, tr, d), lambda b, j: (b, j, 0)),
        out_shape=jax.ShapeDtypeStruct((bsz, length, d), BF16),
        compiler_params=_cp("parallel", "parallel"),
        name="norm_mod",
    )(x, g.reshape(1, d), shift, scale)


def _final_norm_kernel(x_ref, g_ref, o_ref):
    x = x_ref[0]
    o_ref[0] = x * lax.rsqrt(jnp.mean(x * x, axis=-1, keepdims=True) + NORM_EPS) * g_ref[...]


def _final_norm(x, g):
    bsz, length, d = x.shape
    tr = _tile(length, 256, 8)
    return pl.pallas_call(
        _final_norm_kernel,
        grid=(bsz, length // tr),
        in_specs=[pl.BlockSpec((1, tr, d), lambda b, j: (b, j, 0)),
                  pl.BlockSpec((1, d), lambda b, j: (0, 0))],
        out_specs=pl.BlockSpec((1, tr, d), lambda b, j: (b, j, 0)),
        out_shape=jax.ShapeDtypeStruct((bsz, length, d), F32),
        compiler_params=_cp("parallel", "parallel"),
        name="final_norm",
    )(x, g.reshape(1, d))


def _mm_kernel(a_ref, w_ref, o_ref):
    o_ref[...] = jnp.dot(a_ref[...], w_ref[...], preferred_element_type=F32).astype(o_ref.dtype)


def _matmul(a, w, out_dtype, tm_pref=512, tn_pref=2048):
    m, k = a.shape
    n = w.shape[1]
    tm = _tile(m, tm_pref, 16)
    tn = _tile(n, tn_pref, LANES)
    return pl.pallas_call(
        _mm_kernel,
        grid=(m // tm, n // tn),
        in_specs=[pl.BlockSpec((tm, k), lambda i, j: (i, 0)),
                  pl.BlockSpec((k, tn), lambda i, j: (0, j))],
        out_specs=pl.BlockSpec((tm, tn), lambda i, j: (i, j)),
        out_shape=jax.ShapeDtypeStruct((m, n), out_dtype),
        compiler_params=_cp("parallel", "parallel"),
        name="matmul",
    )(a, w)


def _mm_res_kernel(a_ref, w_ref, x_ref, g_ref, o_ref):
    y = jnp.dot(a_ref[0], w_ref[...], preferred_element_type=F32)
    o_ref[0] = x_ref[0] + g_ref[0] * y


def _matmul_residual(a, w, x, gate):
    bsz, length, k = a.shape
    n = w.shape[1]
    tm = _tile(length, 1024, 16)
    tn = _tile(n, 512, LANES)
    bm = gate.shape[0]
    gmap = (lambda b, i, j: (b, 0, j)) if bm == bsz else (lambda b, i, j: (0, 0, j))
    return pl.pallas_call(
        _mm_res_kernel,
        grid=(bsz, length // tm, n // tn),
        in_specs=[pl.BlockSpec((1, tm, k), lambda b, i, j: (b, i, 0)),
                  pl.BlockSpec((k, tn), lambda b, i, j: (0, j)),
                  pl.BlockSpec((1, tm, tn), lambda b, i, j: (b, i, j)),
                  pl.BlockSpec((1, 1, tn), gmap)],
        out_specs=pl.BlockSpec((1, tm, tn), lambda b, i, j: (b, i, j)),
        out_shape=jax.ShapeDtypeStruct((bsz, length, n), F32),
        compiler_params=_cp("parallel", "parallel", "parallel"),
        name="matmul_residual",
    )(a, w, x, gate)


def _mm_res_norm_kernel(a_ref, w_ref, x_ref, g_ref, ng_ref, sh_ref, sc_ref, xo_ref, h_ref):
    xn = x_ref[0] + g_ref[0] * jnp.dot(a_ref[0], w_ref[...], preferred_element_type=F32)
    xo_ref[0] = xn
    y = xn * lax.rsqrt(jnp.mean(xn * xn, axis=-1, keepdims=True) + NORM_EPS) * ng_ref[...]
    h_ref[0] = (y * (1.0 + sc_ref[0]) + sh_ref[0]).astype(h_ref.dtype)


def _matmul_residual_norm(a, w, x, gate, norm_g, shift, scale):
    bsz, length, k = a.shape
    n = w.shape[1]
    tm = _tile(length, 512, 16)
    bm = gate.shape[0]
    mmap = (lambda b, i: (b, 0, 0)) if bm == bsz else (lambda b, i: (0, 0, 0))
    row = pl.BlockSpec((1, tm, n), lambda b, i: (b, i, 0))
    mod = pl.BlockSpec((1, 1, n), mmap)
    return pl.pallas_call(
        _mm_res_norm_kernel,
        grid=(bsz, length // tm),
        in_specs=[pl.BlockSpec((1, tm, k), lambda b, i: (b, i, 0)),
                  pl.BlockSpec((k, n), lambda b, i: (0, 0)),
                  row, mod, pl.BlockSpec((1, n), lambda b, i: (0, 0)), mod, mod],
        out_specs=[row, row],
        out_shape=[jax.ShapeDtypeStruct((bsz, length, n), F32),
                   jax.ShapeDtypeStruct((bsz, length, n), BF16)],
        compiler_params=_cp("parallel", "parallel"),
        name="outproj_norm",
    )(a, w, x, gate, norm_g.reshape(1, n), shift, scale)


def _merge_kernel(h_ref, o0_ref, o1_ref, o2_ref, o3_ref, wg_ref, gb_ref, wbr_ref, out_ref):
    h = h_ref[...]
    acc = None
    for n, o_ref in enumerate((o0_ref, o1_ref, o2_ref, o3_ref)):
        gate = _sigmoid(jnp.dot(h, wg_ref[n], preferred_element_type=F32) + gb_ref[n])
        term = gate * jnp.dot(o_ref[...], wbr_ref[n], preferred_element_type=F32)
        acc = term if acc is None else acc + term
    out_ref[...] = acc.astype(out_ref.dtype)


def _gated_merge(h, branches, wg, gb, wbr):
    m, d = h.shape
    nb = len(branches)
    tm = _tile(m, 1024, 16)
    tn = _tile(d, 512, LANES)
    bspec = pl.BlockSpec((tm, BR_W), lambda i, j: (i, 0))
    return pl.pallas_call(
        _merge_kernel,
        grid=(m // tm, d // tn),
        in_specs=[pl.BlockSpec((tm, d), lambda i, j: (i, 0)), bspec, bspec, bspec, bspec,
                  pl.BlockSpec((nb, d, tn), lambda i, j: (0, 0, j)),
                  pl.BlockSpec((nb, 1, tn), lambda i, j: (0, 0, j)),
                  pl.BlockSpec((nb, BR_W, tn), lambda i, j: (0, 0, j))],
        out_specs=pl.BlockSpec((tm, tn), lambda i, j: (i, j)),
        out_shape=jax.ShapeDtypeStruct((m, d), BF16),
        compiler_params=_cp("parallel", "parallel"),
        name="gated_merge",
    )(h, *branches, wg, gb, wbr)


def _conv3(x, prev_row, next_row, w_ref, b_ref):
    rows = x.shape[0]
    rid = lax.broadcasted_iota(jnp.int32, x.shape, 0)
    prev = jnp.where(rid == 0, prev_row, pltpu.roll(x, 1, 0))
    nxt = jnp.where(rid == rows - 1, next_row, pltpu.roll(x, rows - 1, 0))
    return b_ref[...] + prev * w_ref[0:1, :] + x * w_ref[1:2, :] + nxt * w_ref[2:3, :]


def _ffn_up_kernel(*refs, nt, halo):
    if nt == 1:
        a_ref = refs[0]
        refs = refs[1:]
        halo = 0
        a_ext = a_ref[0]
    else:
        a_ref, ap_ref, an_ref = refs[0:3]
        refs = refs[3:]
        a_ext = jnp.concatenate([ap_ref[0], a_ref[0], an_ref[0]], axis=0)
    wg_ref, wv_ref, cwg_ref, cbg_ref, cwv_ref, cbv_ref, o_ref = refs
    i = pl.program_id(1)
    tm = a_ref.shape[1]
    n_ext = tm + 2 * halo
    rid = lax.broadcasted_iota(jnp.int32, (tm, 1), 0)
    no_prev = jnp.logical_and(i == 0, rid == 0)
    no_next = jnp.logical_and(i == nt - 1, rid == tm - 1)

    def branch(w_ref, cw_ref, cb_ref):
        u = jnp.dot(a_ext, w_ref[...], preferred_element_type=F32)
        um1 = jnp.where(no_prev, 0.0, pltpu.roll(u, 1, 0)[halo:halo + tm])
        up1 = jnp.where(no_next, 0.0, pltpu.roll(u, n_ext - 1, 0)[halo:halo + tm])
        return (cb_ref[...] + um1 * cw_ref[0:1, :] + u[halo:halo + tm] * cw_ref[1:2, :]
                + up1 * cw_ref[2:3, :])

    gate = branch(wg_ref, cwg_ref, cbg_ref)
    val = branch(wv_ref, cwv_ref, cbv_ref)
    o_ref[0] = (_silu(gate) * val).astype(o_ref.dtype)


def _ffn_up(h, w_up, conv_w, conv_b):
    bsz, length, d = h.shape
    f2 = w_up.shape[1]
    f = f2 // 2
    halo = 16
    tm = _tile(length, 1024, halo)
    tc = _tile(f, 512, LANES)
    nt = length // tm
    nc = f // tc
    nh = length // halo
    rb = tm // halo

    def wspec(rows, off):
        return pl.BlockSpec((rows, tc), lambda b, i, c: (0, c + off))

    a_specs = [pl.BlockSpec((1, tm, d), lambda b, i, c: (b, i, 0))]
    if nt > 1:
        a_specs += [
            pl.BlockSpec((1, halo, d), lambda b, i, c: (b, jnp.maximum(i * rb - 1, 0), 0)),
            pl.BlockSpec((1, halo, d), lambda b, i, c: (b, jnp.minimum((i + 1) * rb, nh - 1), 0))]
    cb = conv_b.reshape(1, f2)
    return pl.pallas_call(
        functools.partial(_ffn_up_kernel, nt=nt, halo=halo),
        grid=(bsz, nt, nc),
        in_specs=a_specs + [wspec(d, 0), wspec(d, nc), wspec(3, 0), wspec(1, 0), wspec(3, nc),
                            wspec(1, nc)],
        out_specs=pl.BlockSpec((1, tm, tc), lambda b, i, c: (b, i, c)),
        out_shape=jax.ShapeDtypeStruct((bsz, length, f), BF16),
        compiler_params=_cp("parallel", "parallel", "parallel"),
        name="ffn_up",
    )(*([h] * len(a_specs)), w_up, w_up, conv_w, cb, conv_w, cb)


def _rw_prep_kernel(z_ref, zp_ref, zn_ref, mu_ref, w0_ref, wup_ref, a0_ref, aup_ref, gup_ref,
                    kk_ref, ka_ref, rk_ref, ones_ref, o_ref, lw_out_ref, *, nt):
    j = pl.program_id(1)
    p = z_ref[0]
    rows = p.shape[0]
    rid = lax.broadcasted_iota(jnp.int32, p.shape, 0)
    prow = jnp.where(j > 0, zp_ref[0, 7:8, :], 0.0)
    nrow = jnp.where(j < nt - 1, zn_ref[0, 0:1, :], 0.0)
    prev = jnp.where(rid == 0, prow, pltpu.roll(p, 1, 0))
    nxt = jnp.where(rid == rows - 1, nrow, pltpu.roll(p, rows - 1, 0))
    ps = p + mu_ref[0:1, :] * (prev - p) + mu_ref[1:2, :] * (nxt - p)

    w = BR_W
    r = ps[:, 0:w]
    k = ps[:, w:2 * w]
    v = ps[:, 2 * w:3 * w]
    wd = ps[:, 3 * w:3 * w + 128]
    ad = ps[:, 3 * w + 128:3 * w + 256]
    gd = ps[:, 3 * w + 256:3 * w + 384]

    wraw = _dot(jnp.tanh(wd), wup_ref[...])
    araw = _dot(ad, aup_ref[...])
    g = _dot(_sigmoid(gd), gup_ref[...])
    ones = ones_ref[...]
    kkv = k * kk_ref[...]
    ss = _x_dot01(kkv * kkv, ones)
    kkn = kkv / jnp.maximum(jnp.sqrt(ss), 1e-12)

    def put(i, x):
        o_ref[0, :, i * w:(i + 1) * w] = x.astype(o_ref.dtype)

    put(RW_R, r)
    put(RW_V, v)
    put(RW_KK, kkn)
    ksum = None
    for d in range(2):
        lw = -math.exp(-0.5) * _sigmoid(w0_ref[d:d + 1, :] + wraw[:, d * w:(d + 1) * w])
        lw_out_ref[0, :, d * w:(d + 1) * w] = lw
        a = _sigmoid(a0_ref[d:d + 1, :] + araw[:, d * w:(d + 1) * w])
        kd = k * (1.0 + (a - 1.0) * ka_ref[...])
        put(RW_KD + 2 * d, kd)
        put(RW_BD + 2 * d, kkn * a)
        ksum = kd if ksum is None else ksum + kd
    put(RW_G, g)
    put(RW_BONUS, _x_dot01(r * rk_ref[...] * ksum, ones) * v)


def _rw_prep(z, lp):
    bsz, length, fin = z.shape
    tr = _tile(length, 256, 8)
    nt = length // tr
    rb = tr // 8
    nh = length // 8
    full = lambda a: pl.BlockSpec(a.shape, lambda b, j: (0,) * a.ndim)
    params = (lp['rw_mu'], lp['rw_w0'], lp['rw_wup'], lp['rw_a0'], lp['rw_aup'], lp['rw_gup'],
              lp['rw_kk'], lp['rw_ka'], lp['rw_rk'], lp['ones_hd'])
    return pl.pallas_call(
        functools.partial(_rw_prep_kernel, nt=nt),
        grid=(bsz, nt),
        in_specs=[pl.BlockSpec((1, tr, fin), lambda b, j: (b, j, 0)),
                  pl.BlockSpec((1, 8, fin), lambda b, j: (b, jnp.maximum(j * rb - 1, 0), 0)),
                  pl.BlockSpec((1, 8, fin), lambda b, j: (b, jnp.minimum((j + 1) * rb, nh - 1), 0))]
                 + [full(a) for a in params],
        out_specs=[pl.BlockSpec((1, tr, RW_NARR * BR_W), lambda b, j: (b, j, 0)),
                   pl.BlockSpec((1, tr, 2 * BR_W), lambda b, j: (b, j, 0))],
        out_shape=[jax.ShapeDtypeStruct((bsz, length, RW_NARR * BR_W), BF16),
                   jax.ShapeDtypeStruct((bsz, length, 2 * BR_W), F32)],
        compiler_params=_cp("parallel", "parallel"),
        name="rwkv_prep",
    )(z, z, z, *params)


def _rw_chunks(dir_refs, o_refs, s_scr, nb):
    cs = RW_CHUNK
    gw = RW_GW
    ng = RW_HEADS // RW_GROUP
    ri = lax.broadcasted_iota(jnp.int32, (gw, gw), 0)
    ci = lax.broadcasted_iota(jnp.int32, (gw, gw), 1)
    head_mask = (ri >> 6) == (ci >> 6)
    tw = lax.broadcasted_iota(jnp.int32, (cs, gw), 0)
    iw = lax.broadcasted_iota(jnp.int32, (cs, gw), 1) & (cs - 1)
    eye = jnp.where(iw == tw, 1.0, 0.0)
    same = {s: (tw >> int(math.log2(s))) == (iw >> int(math.log2(s))) for s in (8, 16, 32)}
    ti = lax.broadcasted_iota(jnp.int32, (cs, cs), 0)
    ii = lax.broadcasted_iota(jnp.int32, (cs, cs), 1)

    def expand(x):
        xb = x.astype(BF16)
        return jnp.where(head_mask, jnp.concatenate([xb] * RW_GROUP, axis=0), jnp.zeros_like(xb[0:1, 0:1]))

    def mm(a, b):
        return _dot(a, expand(b))

    chains = []
    for bb in range(nb):
        for d, (r_ref, v_ref, kk_ref, lw_ref, k_ref, b_ref) in enumerate(dir_refs):
            sgn = 1 - 2 * d
            rel = (iw - tw) * sgn
            tri = jnp.where((ii - ti) * sgn <= 0, 1.0, 0.0).astype(BF16)
            lw = lw_ref[bb]
            cum = _dot01(tri, lw)
            tot = jnp.sum(lw, axis=0, keepdims=True)
            kd, bd = k_ref[bb].astype(F32), b_ref[bb].astype(F32)
            qt = kk_ref[bb].astype(F32) * jnp.exp(cum - lw)
            rt = r_ref[bb].astype(F32) * jnp.exp(cum)
            e_neg = jnp.exp(-cum)
            e_end = jnp.exp(tot - cum)
            kt = kd * e_neg
            bt = bd * e_neg
            kh = kd * e_end
            bh = bd * e_end
            p_tot = jnp.exp(tot)
            v = v_ref[bb].astype(F32)
            for g in range(ng):
                sl = slice(g * gw, (g + 1) * gw)
                chains.append(dict(bb=bb, d=d, g=g, sl=sl, strict=rel < 0, incl=rel <= 0,
                                   q=qt[:, sl], r=rt[:, sl], v=v[:, sl], kt=kt[:, sl],
                                   bt=bt[:, sl], kh=kh[:, sl], bh=bh[:, sl], p_tot=p_tot[:, sl],
                                   s0=s_scr[bb, d, g]))

    for c in chains:
        c['kte'], c['bte'], c['ve'] = expand(c['kt']), expand(c['bt']), expand(c['v'])
        c['s0b'] = c['s0'].astype(BF16)
    for c in chains:
        c['lmat'] = jnp.where(c['strict'], _dot_nt(c['q'], c['bte']), 0.0)
        c['ld'] = jnp.where(same[8], c['lmat'], 0.0)
    for c in chains:
        c['ld2'] = mm(c['ld'], c['ld'])
    for c in chains:
        c['a_qk'] = jnp.where(c['strict'], _dot_nt(c['q'], c['kte']), 0.0)
    for c in chains:
        c['ld4'] = mm(c['ld2'], c['ld2'])
    for c in chains:
        c['x'] = mm(eye - c['ld'], eye + c['ld2'])
    for c in chains:
        c['a_rk'] = jnp.where(c['incl'], _dot_nt(c['r'], c['kte']), 0.0)
    for c in chains:
        c['x'] = mm(c['x'], eye + c['ld4'])
    for c in chains:
        c['a_rb'] = jnp.where(c['incl'], _dot_nt(c['r'], c['bte']), 0.0)
    for c in chains:
        c['rhs'] = _dot_nt(c['q'], c['s0b']) + _dot(c['a_qk'], c['ve'])
    for s in (8, 16, 32):
        off_mask = jnp.logical_not(same[s])
        if 2 * s < cs:
            off_mask = jnp.logical_and(same[2 * s], off_mask)
        for c in chains:
            c['t'] = mm(c['x'], jnp.where(off_mask, c['lmat'], 0.0))
        if s == 8:
            for c in chains:
                c['o'] = _dot_nt(c['r'], c['s0b']) + _dot(c['a_rk'], c['ve'])
        for c in chains:
            c['x'] = c['x'] - mm(c['t'], c['x'])
    for c in chains:
        c['sa'] = mm(c['x'], c['rhs'])
    for c in chains:
        o_refs[c['d']][c['bb'], :, c['sl']] = c['o'] - mm(c['a_rb'], c['sa'])
    for c in chains:
        upd = _dot_tn(jnp.concatenate([c['v'], c['sa']], axis=0),
                      jnp.concatenate([c['kh'], -c['bh']], axis=0))
        s_scr[c['bb'], c['d'], c['g']] = c['s0'] * c['p_tot'] + jnp.where(head_mask, upd, 0.0)


def _rw_scan_kernel(*refs, nc, nb):
    fwd, bwd = refs[0:6], refs[6:12]
    s0_ref, of_ref, ob_ref, sfin_ref, s_scr = refs[12:]
    c = pl.program_id(1)

    @pl.when(c == 0)
    def _():
        s_scr[...] = s0_ref[...]

    _rw_chunks((fwd, bwd), (of_ref, ob_ref), s_scr, nb)

    @pl.when(c == nc - 1)
    def _():
        sfin_ref[...] = s_scr[...]


def _rw_scan(rwp, lwp, s_init):
    bsz, length, _ = rwp.shape
    cs = RW_CHUNK
    nc = length // cs
    ng = RW_HEADS // RW_GROUP
    nb = RW_BATCH if bsz % RW_BATCH == 0 else 1

    def arr(d, i):
        if d == 0:
            return pl.BlockSpec((nb, cs, BR_W), lambda b, c: (b, c, i))
        return pl.BlockSpec((nb, cs, BR_W), lambda b, c: (b, nc - 1 - c, i))

    def specs(d):
        return [arr(d, RW_R), arr(d, RW_V), arr(d, RW_KK), arr(d, d),
                arr(d, RW_KD + 2 * d), arr(d, RW_BD + 2 * d)]

    def operands():
        return [rwp, rwp, rwp, lwp, rwp, rwp]

    sspec = pl.BlockSpec((nb, 2, ng, RW_GW, RW_GW), lambda b, c: (b, 0, 0, 0, 0))
    oshape = jax.ShapeDtypeStruct((bsz, length, BR_W), F32)
    return pl.pallas_call(
        functools.partial(_rw_scan_kernel, nc=nc, nb=nb),
        grid=(bsz // nb, nc),
        in_specs=specs(0) + specs(1) + [sspec],
        out_specs=[pl.BlockSpec((nb, cs, BR_W), lambda b, c: (b, c, 0)),
                   pl.BlockSpec((nb, cs, BR_W), lambda b, c: (b, nc - 1 - c, 0)), sspec],
        out_shape=[oshape, oshape, jax.ShapeDtypeStruct((bsz, 2, ng, RW_GW, RW_GW), F32)],
        scratch_shapes=[pltpu.VMEM((nb, 2, ng, RW_GW, RW_GW), F32)],
        compiler_params=_cp("parallel", "arbitrary"),
        name="rwkv_scan",
    )(*operands(), *operands(), s_init)


def _rw_readout_kernel(of_ref, ob_ref, g_ref, bonus_ref, lng_ref, lnb_ref, mean_ref, out_ref):
    o = of_ref[0] + ob_ref[0]
    mean_m = mean_ref[...]
    mu = _x_dot01(o, mean_m)
    dlt = o - mu
    var = _x_dot01(dlt * dlt, mean_m)
    on = dlt * lax.rsqrt(var + RW_GN_EPS) * lng_ref[...] + lnb_ref[...]
    out_ref[0] = ((on + bonus_ref[0].astype(F32)) * g_ref[0].astype(F32)).astype(out_ref.dtype)


def _rw_readout(o_f, o_b, rwp, lp):
    bsz, length, w = o_f.shape
    tr = _tile(length, 256, 16)
    ospec = pl.BlockSpec((1, tr, w), lambda b, j: (b, j, 0))
    return pl.pallas_call(
        _rw_readout_kernel,
        grid=(bsz, length // tr),
        in_specs=[ospec, ospec,
                  pl.BlockSpec((1, tr, w), lambda b, j: (b, j, RW_G)),
                  pl.BlockSpec((1, tr, w), lambda b, j: (b, j, RW_BONUS)),
                  pl.BlockSpec((1, w), lambda b, j: (0, 0)),
                  pl.BlockSpec((1, w), lambda b, j: (0, 0)),
                  pl.BlockSpec((w, w), lambda b, j: (0, 0))],
        out_specs=ospec,
        out_shape=jax.ShapeDtypeStruct((bsz, length, w), BF16),
        compiler_params=_cp("parallel", "parallel"),
        name="rwkv_readout",
    )(o_f, o_b, rwp, rwp, lp['rw_lng'], lp['rw_lnb'], lp['mean_hd'])


def _ssd_prep_kernel(z_ref, zp_ref, zn_ref, cw_ref, cb_ref, dtb_ref, xbc_ref, dt_ref, *, nt):
    j = pl.program_id(1)
    x = z_ref[0, :, BR_W:BR_W + SSD_XBC]
    prow = jnp.where(j > 0, zp_ref[0, 7:8, BR_W:BR_W + SSD_XBC], 0.0)
    nrow = jnp.where(j < nt - 1, zn_ref[0, 0:1, BR_W:BR_W + SSD_XBC], 0.0)
    xbc_ref[0] = _silu(_conv3(x, prow, nrow, cw_ref, cb_ref))
    dt_raw = z_ref[0, :, BR_W + SSD_XBC:BR_W + SSD_XBC + LANES]
    dt_ref[0, 0] = _softplus(dt_raw + dtb_ref[0:1, :])
    dt_ref[0, 1] = _softplus(pltpu.roll(dt_raw, LANES - SSD_HEADS, 1) + dtb_ref[1:2, :])


def _ssd_prep(z, lp):
    bsz, length, fin = z.shape
    tr = _tile(length, 256, 8)
    nt = length // tr
    rb = tr // 8
    nh = length // 8
    return pl.pallas_call(
        functools.partial(_ssd_prep_kernel, nt=nt),
        grid=(bsz, nt),
        in_specs=[pl.BlockSpec((1, tr, fin), lambda b, j: (b, j, 0)),
                  pl.BlockSpec((1, 8, fin), lambda b, j: (b, jnp.maximum(j * rb - 1, 0), 0)),
                  pl.BlockSpec((1, 8, fin), lambda b, j: (b, jnp.minimum((j + 1) * rb, nh - 1), 0)),
                  pl.BlockSpec((3, SSD_XBC), lambda b, j: (0, 0)),
                  pl.BlockSpec((1, SSD_XBC), lambda b, j: (0, 0)),
                  pl.BlockSpec((2, LANES), lambda b, j: (0, 0))],
        out_specs=[pl.BlockSpec((1, tr, SSD_XBC), lambda b, j: (b, j, 0)),
                   pl.BlockSpec((1, 2, tr, LANES), lambda b, j: (b, 0, j, 0))],
        out_shape=[jax.ShapeDtypeStruct((bsz, length, SSD_XBC), F32),
                   jax.ShapeDtypeStruct((bsz, 2, length, LANES), F32)],
        compiler_params=_cp("parallel", "parallel"),
        name="ssd_prep",
    )(z, z, z, lp['ssd_cw'], lp['ssd_cb'], lp['ssd_dtb'])


def _ssd_chunks(dir_refs, a_ref, y_refs, h_scr, nb):
    q = SSD_CHUNK
    n = SSD_STATE
    rep = SSD_HEADS // SSD_GROUPS
    gw = SSD_GW
    ti = lax.broadcasted_iota(jnp.int32, (q, q), 0)
    ii = lax.broadcasted_iota(jnp.int32, (q, q), 1)
    ri = lax.broadcasted_iota(jnp.int32, (rep * q, gw), 0)
    ci = lax.broadcasted_iota(jnp.int32, (rep * q, gw), 1)
    head_mask = (ri >> int(math.log2(q))) == (ci >> int(math.log2(HD)))

    def per_head(x, h0):
        return jnp.concatenate([jnp.broadcast_to(x[:, h0 + h:h0 + h + 1], (x.shape[0], HD))
                                for h in range(rep)], axis=1)

    work = []
    for bb in range(nb):
        for d, (xbc_ref, dt_ref) in enumerate(dir_refs):
            incl = (ii - ti) * (1 - 2 * d) <= 0
            tri = jnp.where(incl, 1.0, 0.0).astype(BF16)
            dt = dt_ref[bb, 0]
            dta = dt * a_ref[d]
            acs = _dot01(tri, dta)
            acs_t = acs.T
            tot = jnp.sum(dta, axis=0, keepdims=True)
            e_acs = jnp.exp(acs)
            e_end = jnp.exp(tot - acs)
            e_tot = jnp.exp(tot)
            xbc = xbc_ref[bb]
            for g in range(SSD_GROUPS):
                h0 = g * rep
                bg = xbc[:, BR_W + g * n:BR_W + (g + 1) * n]
                cg = xbc[:, BR_W + SSD_GROUPS * n + g * n:BR_W + SSD_GROUPS * n + (g + 1) * n]
                xdt = xbc[:, g * gw:(g + 1) * gw] * per_head(dt, h0)
                dec = jnp.concatenate(
                    [jnp.exp(jnp.where(incl, acs[:, h0 + h:h0 + h + 1] - acs_t[h0 + h:h0 + h + 1, :], NEG))
                     for h in range(rep)], axis=1)
                work.append(dict(bb=bb, d=d, g=g, bg=bg, cg=cg, xdt=xdt, dec=dec,
                                 ht=h_scr[bb, d, g], e_acs=per_head(e_acs, h0),
                                 e_end=per_head(e_end, h0), e_tot=per_head(e_tot, h0)))
    for w in work:
        w['cb'] = _dot_nt(w['cg'], w['bg'])
    for w in work:
        w['yoff'] = _dot(w['cg'], w['ht'])
    for w in work:
        w['st'] = _dot_tn(w['bg'], w['xdt'] * w['e_end'])
    for w in work:
        xe = jnp.where(head_mask, jnp.concatenate([w['xdt']] * rep, axis=0), 0.0)
        scores = jnp.concatenate([w['cb']] * rep, axis=1) * w['dec']
        y = _dot(scores, xe) + w['e_acs'] * w['yoff']
        y_refs[w['d']][w['bb'], :, w['g'] * gw:(w['g'] + 1) * gw] = y
        h_scr[w['bb'], w['d'], w['g']] = w['ht'] * w['e_tot'] + w['st']


def _ssd_scan_kernel(xf_ref, dtf_ref, xb_ref, dtb_ref, a_ref, h0_ref, yf_ref, yb_ref, hfin_ref,
                     h_scr, *, nc, nb):
    c = pl.program_id(1)

    @pl.when(c == 0)
    def _():
        h_scr[...] = h0_ref[...]

    _ssd_chunks(((xf_ref, dtf_ref), (xb_ref, dtb_ref)), a_ref, (yf_ref, yb_ref), h_scr, nb)

    @pl.when(c == nc - 1)
    def _():
        hfin_ref[...] = h_scr[...]


def _ssd_scan(xbc, dt, a_neg, h_init):
    bsz, length, _ = xbc.shape
    q = SSD_CHUNK
    nc = length // q
    nb = RW_BATCH if bsz % RW_BATCH == 0 else 1
    hspec = pl.BlockSpec((nb, 2, SSD_GROUPS, SSD_STATE, SSD_GW), lambda b, c: (b, 0, 0, 0, 0))
    oshape = jax.ShapeDtypeStruct((bsz, length, BR_W), F32)
    return pl.pallas_call(
        functools.partial(_ssd_scan_kernel, nc=nc, nb=nb),
        grid=(bsz // nb, nc),
        in_specs=[pl.BlockSpec((nb, q, SSD_XBC), lambda b, c: (b, c, 0)),
                  pl.BlockSpec((nb, 1, q, LANES), lambda b, c: (b, 0, c, 0)),
                  pl.BlockSpec((nb, q, SSD_XBC), lambda b, c: (b, nc - 1 - c, 0)),
                  pl.BlockSpec((nb, 1, q, LANES), lambda b, c: (b, 1, nc - 1 - c, 0)),
                  pl.BlockSpec((2, 1, LANES), lambda b, c: (0, 0, 0)),
                  hspec],
        out_specs=[pl.BlockSpec((nb, q, BR_W), lambda b, c: (b, c, 0)),
                   pl.BlockSpec((nb, q, BR_W), lambda b, c: (b, nc - 1 - c, 0)), hspec],
        out_shape=[oshape, oshape,
                   jax.ShapeDtypeStruct((bsz, 2, SSD_GROUPS, SSD_STATE, SSD_GW), F32)],
        scratch_shapes=[pltpu.VMEM((nb, 2, SSD_GROUPS, SSD_STATE, SSD_GW), F32)],
        compiler_params=_cp("parallel", "arbitrary"),
        name="ssd_scan",
    )(xbc, dt, xbc, dt, a_neg, h_init)


def _ssd_readout_kernel(yf_ref, yb_ref, xs_ref, z_ref, dsk_ref, ng_ref, out_ref):
    y = yf_ref[0] + yb_ref[0] + dsk_ref[...] * xs_ref[0]
    y = y * _silu(z_ref[0])
    y = y * lax.rsqrt(jnp.mean(y * y, axis=-1, keepdims=True) + NORM_EPS) * ng_ref[...]
    out_ref[0] = y.astype(out_ref.dtype)


def _ssd_readout(y_f, y_b, xbc, z, lp):
    bsz, length, w = y_f.shape
    tr = _tile(length, 256, 16)
    ospec = pl.BlockSpec((1, tr, w), lambda b, j: (b, j, 0))
    return pl.pallas_call(
        _ssd_readout_kernel,
        grid=(bsz, length // tr),
        in_specs=[ospec, ospec, ospec, ospec,
                  pl.BlockSpec((1, w), lambda b, j: (0, 0)),
                  pl.BlockSpec((1, w), lambda b, j: (0, 0))],
        out_specs=ospec,
        out_shape=jax.ShapeDtypeStruct((bsz, length, w), BF16),
        compiler_params=_cp("parallel", "parallel"),
        name="ssd_readout",
    )(y_f, y_b, xbc, z, lp['ssd_dskip'], lp['ssd_ng'])


def _da_proj_kernel(h_ref, w_ref, cos_ref, sin_ref, q_ref, k_ref, v_ref, *, rope):
    w = BR_W
    z = jnp.dot(h_ref[0], w_ref[...], preferred_element_type=F32)

    def rot(x):
        if not rope:
            return x
        lane = lax.broadcasted_iota(jnp.int32, (1, LANES), 1)
        first = (lane & 31) < 16
        outs = []
        for cb in range(w // LANES):
            xb = x[:, cb * LANES:(cb + 1) * LANES]
            sw = jnp.where(first, pltpu.roll(xb, LANES - 16, 1), pltpu.roll(xb, 16, 1))
            outs.append(xb * cos_ref[:, cb * LANES:(cb + 1) * LANES]
                        + sw * sin_ref[:, cb * LANES:(cb + 1) * LANES])
        return jnp.concatenate(outs, axis=1)

    q_ref[0] = (rot(z[:, 0:w]) * (HD ** -0.5)).astype(q_ref.dtype)
    k_ref[0] = rot(z[:, w:2 * w]).astype(k_ref.dtype)
    v_ref[0] = z[:, 2 * w:3 * w].astype(v_ref.dtype)


def _da_proj(h, w_da, cos, sin, rope):
    bsz, length, d = h.shape
    tr = _tile(length, 512, 16)
    w = BR_W
    ospec = pl.BlockSpec((1, tr, w), lambda b, j: (b, j, 0))
    oshape = jax.ShapeDtypeStruct((bsz, length, w), BF16)
    return pl.pallas_call(
        functools.partial(_da_proj_kernel, rope=rope),
        grid=(bsz, length // tr),
        in_specs=[pl.BlockSpec((1, tr, d), lambda b, j: (b, j, 0)),
                  pl.BlockSpec((d, DA_IN), lambda b, j: (0, 0)),
                  pl.BlockSpec((tr, w), lambda b, j: (j, 0)),
                  pl.BlockSpec((tr, w), lambda b, j: (j, 0))],
        out_specs=[ospec, ospec, ospec],
        out_shape=[oshape, oshape, oshape],
        compiler_params=_cp("parallel", "parallel"),
        name="da_proj",
    )(h, w_da, cos, sin)


def _da_attn_kernel(*refs, nsrc, lam_init):
    lam_ref, q_ref = refs[0], refs[1]
    kv = refs[2:2 + 2 * nsrc]
    g_ref, o_ref = refs[2 + 2 * nsrc], refs[3 + 2 * nsrc]
    lane = lax.broadcasted_iota(jnp.int32, (1, LANES), 1)
    lam = lam_ref[0:1, 0:1]
    heads = [slice(hh * LANES, (hh + 1) * LANES) for hh in range(DA_HPS)]
    scores = {}
    for hh, hs in enumerate(heads):
        q = q_ref[0, :, hs]
        zero = jnp.zeros_like(q)
        for m in range(2):
            qm = jnp.where((lane < HD) if m == 0 else (lane >= HD), q, zero)
            scores[hh, m] = [_dot_nt(qm, kv[2 * i][0, :, hs]) for i in range(nsrc)]
    for hh, hs in enumerate(heads):
        probs = []
        for m in range(2):
            ss = scores[hh, m]
            mx = functools.reduce(jnp.maximum, [jnp.max(s, axis=-1, keepdims=True) for s in ss])
            es = [jnp.exp(s - mx) for s in ss]
            den = functools.reduce(lambda a, b: a + b,
                                   [jnp.sum(e, axis=-1, keepdims=True) for e in es])
            inv = 1.0 / den
            probs.append([e * inv for e in es])
        o = None
        for i in range(nsrc):
            a = probs[0][i] - lam * probs[1][i]
            t = _dot(a, kv[2 * i + 1][0, :, hs])
            o = t if o is None else o + t
        y = o * lax.rsqrt(jnp.mean(o * o, axis=-1, keepdims=True) + DA_SUBLN_EPS) * g_ref[...]
        o_ref[0, :, hs] = (y * (1.0 - lam_init)).astype(o_ref.dtype)


def _da_attn(q, ks, vs, lam, subln_g, lam_init):
    bsz, lq, _ = q.shape
    tq = _tile(lq, 256, 16)
    nsrc = len(ks)
    hw = DA_HPS * LANES
    in_specs = [pl.BlockSpec((1, LANES), lambda b, h, j: (0, 0)),
                pl.BlockSpec((1, tq, hw), lambda b, h, j: (b, j, h))]
    args = [lam, q]
    for k, v in zip(ks, vs):
        lk = k.shape[1]
        in_specs.append(pl.BlockSpec((1, lk, hw), lambda b, h, j: (b, 0, h)))
        in_specs.append(pl.BlockSpec((1, lk, hw), lambda b, h, j: (b, 0, h)))
        args += [k, v]
    in_specs.append(pl.BlockSpec((1, LANES), lambda b, h, j: (0, 0)))
    args.append(subln_g)
    return pl.pallas_call(
        functools.partial(_da_attn_kernel, nsrc=nsrc, lam_init=lam_init),
        grid=(bsz, DA_HEADS // DA_HPS, lq // tq),
        in_specs=in_specs,
        out_specs=pl.BlockSpec((1, tq, hw), lambda b, h, j: (b, j, h)),
        out_shape=jax.ShapeDtypeStruct((bsz, lq, BR_W), BF16),
        compiler_params=_cp("parallel", "parallel", "parallel"),
        name="diff_attn",
    )(*args)


def _na_kb(i, rows):
    return jnp.clip(i * NA_QROWS - NA_WIN_R // 2, 0, rows - NA_KROWS)


def _na_attn_kernel(q_ref, k_ref, v_ref, kc_ref, vc_ref, bias_ref, o_ref, *, rows):
    i = pl.program_id(2)
    start = pl.multiple_of(_na_kb(i, rows) * GRID_W, GRID_W)
    nk = NA_KROWS * GRID_W
    q = q_ref[0] * (HD ** -0.5)
    kw = k_ref[0, pl.ds(start, nk), :]
    vw = v_ref[0, pl.ds(start, nk), :]
    kc = kc_ref[0]
    vc = vc_ref[0]
    lane = lax.broadcasted_iota(jnp.int32, (1, LANES), 1)
    outs = []
    for h in range(2):
        qm = jnp.where((lane < HD) if h == 0 else (lane >= HD), q, 0.0)
        sw = _dot_nt(qm, kw) + bias_ref[h, 0]
        sc = _dot_nt(qm, kc)
        mx = jnp.maximum(jnp.max(sw, axis=-1, keepdims=True), jnp.max(sc, axis=-1, keepdims=True))
        ew = jnp.exp(sw - mx)
        ec = jnp.exp(sc - mx)
        inv = 1.0 / (jnp.sum(ew, axis=-1, keepdims=True) + jnp.sum(ec, axis=-1, keepdims=True))
        outs.append((_dot(ew, vw) + _dot(ec, vc)) * inv)
    o_ref[0] = jnp.where(lane < HD, outs[0], outs[1]).astype(o_ref.dtype)


def _na_attn(z_l, z_c, bias, pat_id):
    bsz, length, _ = z_l.shape
    lc = z_c.shape[1]
    rows = length // GRID_W
    tq = NA_QROWS * GRID_W
    nblk = length // tq
    nkb = BR_W // LANES

    def pat(i):
        out = pat_id[0]
        for j in range(1, nblk):
            if pat_id[j] != pat_id[j - 1]:
                out = out + jnp.where(i >= j, pat_id[j] - pat_id[j - 1], 0)
        return out

    return pl.pallas_call(
        functools.partial(_na_attn_kernel, rows=rows),
        grid=(bsz, NA_HEADS // 2, nblk),
        in_specs=[pl.BlockSpec((1, tq, LANES), lambda b, p, i: (b, i, p)),
                  pl.BlockSpec((1, length, LANES), lambda b, p, i: (b, 0, nkb + p)),
                  pl.BlockSpec((1, length, LANES), lambda b, p, i: (b, 0, 2 * nkb + p)),
                  pl.BlockSpec((1, lc, LANES), lambda b, p, i: (b, 0, nkb + p)),
                  pl.BlockSpec((1, lc, LANES), lambda b, p, i: (b, 0, 2 * nkb + p)),
                  pl.BlockSpec((2, 1, tq, NA_KROWS * GRID_W), lambda b, p, i: (p, pat(i), 0, 0))],
        out_specs=pl.BlockSpec((1, tq, LANES), lambda b, p, i: (b, i, p)),
        out_shape=jax.ShapeDtypeStruct((bsz, length, BR_W), BF16),
        compiler_params=_cp("parallel", "parallel", "parallel"),
        name="na_attn",
    )(z_l, z_l, z_l, z_c, z_c, bias)


def _ctx_attn_kernel(q_ref, k_ref, v_ref, o_ref):
    q = q_ref[0] * (HD ** -0.5)
    k = k_ref[0]
    v = v_ref[0]
    lane = lax.broadcasted_iota(jnp.int32, (1, LANES), 1)
    outs = []
    for h in range(2):
        qm = jnp.where((lane < HD) if h == 0 else (lane >= HD), q, 0.0)
        s = _dot_nt(qm, k)
        e = jnp.exp(s - jnp.max(s, axis=-1, keepdims=True))
        outs.append(_dot(e, v) * (1.0 / jnp.sum(e, axis=-1, keepdims=True)))
    o_ref[0] = jnp.where(lane < HD, outs[0], outs[1]).astype(o_ref.dtype)


def _ctx_attn(z_c):
    bsz, lc, _ = z_c.shape
    nkb = BR_W // LANES
    return pl.pallas_call(
        _ctx_attn_kernel,
        grid=(bsz, NA_HEADS // 2),
        in_specs=[pl.BlockSpec((1, lc, LANES), lambda b, p: (b, 0, p)),
                  pl.BlockSpec((1, lc, LANES), lambda b, p: (b, 0, nkb + p)),
                  pl.BlockSpec((1, lc, LANES), lambda b, p: (b, 0, 2 * nkb + p))],
        out_specs=pl.BlockSpec((1, lc, LANES), lambda b, p: (b, 0, p)),
        out_shape=jax.ShapeDtypeStruct((bsz, lc, BR_W), BF16),
        compiler_params=_cp("parallel", "parallel"),
        name="ctx_attn",
    )(z_c, z_c, z_c)


def _rope_tables(length):
    n_freq = HD // 4
    t = np.arange(length)
    pos = np.stack([t // GRID_W, t % GRID_W], axis=-1).astype(np.float32)
    inv = (ROPE_BASE ** (-np.arange(n_freq, dtype=np.float32) / n_freq)).astype(np.float32)
    lane = np.arange(BR_W)
    which = (lane % HD) // (HD // 2)
    ang = pos[:, which] * inv[lane % n_freq][None, :]
    sign = np.where((lane % (HD // 2)) < n_freq, -1.0, 1.0).astype(np.float32)
    return jnp.asarray(np.cos(ang), F32), jnp.asarray(np.sin(ang) * sign[None, :], F32)


def _na_bias(rpb, length):
    rows = length // GRID_W
    wr = min(NA_WIN_R, rows)
    nblk = rows // NA_QROWS
    qr = np.arange(rows)
    rstart = np.clip(qr - wr // 2, 0, rows - wr)
    kb = np.clip(np.arange(nblk) * NA_QROWS - NA_WIN_R // 2, 0, rows - NA_KROWS)
    qrow = (np.arange(nblk)[:, None] * NA_QROWS + np.arange(NA_QROWS)[None, :])
    krow = kb[:, None] + np.arange(NA_KROWS)[None, :]
    dr = krow[:, None, :] - qrow[:, :, None] + NA_WIN_R - 1
    rvalid = (krow[:, None, :] >= rstart[qrow][:, :, None]) & (krow[:, None, :] < rstart[qrow][:, :, None] + wr)
    dr = np.where(rvalid, dr, -1)
    pats, pat_id = [], []
    for i in range(nblk):
        for j, pdr in enumerate(pats):
            if np.array_equal(pdr, dr[i]):
                pat_id.append(j)
                break
        else:
            pat_id.append(len(pats))
            pats.append(dr[i])
    pdr = np.stack(pats)
    n_dr = 2 * NA_WIN_R - 1
    oh_dr = (pdr[..., None] == np.arange(n_dr)).astype(np.float32)
    cid = np.arange(GRID_W)
    cstart = np.clip(cid - NA_WIN_C // 2, 0, GRID_W - NA_WIN_C)
    in_win = (cid[None, :] >= cstart[:, None]) & (cid[None, :] < cstart[:, None] + NA_WIN_C)
    ci = np.clip(cid[None, :] - cid[:, None], -(NA_WIN_C - 1), NA_WIN_C - 1) + NA_WIN_C - 1
    n_ci = 2 * NA_WIN_C - 1
    oh_ci = (ci[..., None] == np.arange(n_ci)).astype(np.float32)
    t1 = jnp.einsum('hdc,qkc->hdqk', rpb.astype(F32), jnp.asarray(oh_ci),
                    precision=lax.Precision.HIGHEST)
    big = jnp.einsum('brkd,hdxy->hbrxky', jnp.asarray(oh_dr), t1,
                     precision=lax.Precision.HIGHEST)
    valid = (pdr >= 0)[:, :, None, :, None] & in_win[None, None, :, None, :]
    big = jnp.where(jnp.asarray(valid)[None], big, NEG)
    big = big.reshape(rpb.shape[0], len(pats), NA_QROWS * GRID_W, NA_KROWS * GRID_W)
    return big, tuple(pat_id)


def _block_diag(blocks):
    n = len(blocks)
    r, c = blocks[0].shape
    out = jnp.zeros((n * r, n * c), blocks[0].dtype)
    for i, blk in enumerate(blocks):
        out = out.at[i * r:(i + 1) * r, i * c:(i + 1) * c].set(blk)
    return out


def _layer_params(p, i):
    w = BR_W
    w_in = p['w_in'][i]
    o1, o2, o3 = RW_IN, RW_IN + SSD_IN, RW_IN + SSD_IN + DA_IN
    d = w_in.shape[0]
    w_ssd = jnp.concatenate([w_in[:, o1:o2], jnp.zeros((d, SSD_IN_PAD - SSD_IN), w_in.dtype)], axis=1)
    head_id = np.arange(w) // HD
    ones_hd = jnp.asarray((head_id[:, None] == head_id[None, :]).astype(np.float32), BF16)
    lam_p = p['da_lambda'][i].astype(F32)
    lam_init = 0.8 - 0.6 * math.exp(-0.3 * i)
    lam = jnp.exp(jnp.sum(lam_p[0] * lam_p[1])) - jnp.exp(jnp.sum(lam_p[2] * lam_p[3])) + lam_init
    dtb = p['ssd_dt_bias'][i]
    return {
        'w_rw': w_in[:, :o1].astype(BF16), 'w_ssd': w_ssd.astype(BF16),
        'w_da': w_in[:, o2:o3].astype(BF16), 'w_na': w_in[:, o3:].astype(BF16),
        'rw_mu': p['rw_mu'][i], 'rw_w0': p['rw_w0'][i],
        'rw_wup': _block_diag([p['rw_w_up'][i, 0], p['rw_w_up'][i, 1]]).astype(BF16),
        'rw_a0': p['rw_a0'][i],
        'rw_aup': _block_diag([p['rw_a_up'][i, 0], p['rw_a_up'][i, 1]]).astype(BF16),
        'rw_gup': p['rw_g_up'][i].astype(BF16),
        'rw_kk': p['rw_k_k'][i].reshape(1, w), 'rw_ka': p['rw_k_a'][i].reshape(1, w),
        'rw_rk': p['rw_r_k'][i].reshape(1, w),
        'rw_lng': p['rw_ln_g'][i].reshape(1, w), 'rw_lnb': p['rw_ln_b'][i].reshape(1, w),
        'ones_hd': ones_hd, 'mean_hd': (ones_hd.astype(F32) / HD).astype(BF16),
        'ssd_cw': p['ssd_conv_w'][i], 'ssd_cb': p['ssd_conv_b'][i].reshape(1, SSD_XBC),
        'ssd_dtb': jnp.pad(dtb, ((0, 0), (0, LANES - SSD_HEADS))),
        'ssd_a': jnp.pad(-jnp.exp(p['ssd_a_log'][i].astype(F32)), ((0, 0), (0, LANES - SSD_HEADS))).reshape(2, 1, LANES),
        'ssd_dskip': jnp.repeat(p['ssd_d'][i], HD).reshape(1, w),
        'ssd_ng': p['ssd_norm_g'][i].reshape(1, w),
        'da_lam': jnp.broadcast_to(lam.reshape(1, 1), (1, LANES)).astype(F32),
        'da_lam_init': lam_init,
        'da_g': p['da_subln_g'][i].reshape(1, 2 * HD),
        'na_rpb': p['na_rpb'][i],
        'w_gate': p['w_gate'][i].astype(BF16),
        'gate_b': p['gate_b'][i][:, None, :],
        'w_br': p['w_br'][i].astype(BF16), 'w_out': p['w_out'][i].astype(BF16),
        'ffn_up': p['ffn_up'][i].astype(BF16), 'ffn_cw': p['ffn_conv_w'][i],
        'ffn_cb': p['ffn_conv_b'][i], 'ffn_down': p['ffn_down'][i].astype(BF16),
    }


def _project(h, lp):
    bsz, length, d = h.shape
    h2 = h.reshape(bsz * length, d)
    return tuple(_matmul(h2, lp[n], F32).reshape(bsz, length, -1)
                 for n in ('w_rw', 'w_ssd', 'w_na'))


def _ffn(x, h2, lp, gate):
    act = _ffn_up(h2, lp['ffn_up'], lp['ffn_cw'], lp['ffn_cb'])
    return _matmul_residual(act, lp['ffn_down'], x, gate)


def kernel(x, c, ctx, c_ctx, ada_w, ada_b, norm1_g, norm2_g, w_in, rw_mu, rw_w0, rw_w_up, rw_a0, rw_a_up, rw_g_up, rw_k_k, rw_k_a, rw_r_k, rw_ln_g, rw_ln_b, ssd_conv_w, ssd_conv_b, ssd_dt_bias, ssd_a_log, ssd_d, ssd_norm_g, da_lambda, da_subln_g, na_rpb, w_gate, gate_b, w_br, w_out, ffn_up, ffn_conv_w, ffn_conv_b, ffn_down, final_norm_g):
    p = dict(w_in=w_in, rw_mu=rw_mu, rw_w0=rw_w0, rw_w_up=rw_w_up, rw_a0=rw_a0, rw_a_up=rw_a_up,
             rw_g_up=rw_g_up, rw_k_k=rw_k_k, rw_k_a=rw_k_a, rw_r_k=rw_r_k, rw_ln_g=rw_ln_g,
             rw_ln_b=rw_ln_b, ssd_conv_w=ssd_conv_w, ssd_conv_b=ssd_conv_b, ssd_dt_bias=ssd_dt_bias,
             ssd_a_log=ssd_a_log, ssd_d=ssd_d, ssd_norm_g=ssd_norm_g, da_lambda=da_lambda,
             da_subln_g=da_subln_g, na_rpb=na_rpb, w_gate=w_gate, gate_b=gate_b, w_br=w_br,
             w_out=w_out, ffn_up=ffn_up, ffn_conv_w=ffn_conv_w, ffn_conv_b=ffn_conv_b,
             ffn_down=ffn_down)
    bsz, seq, d = x.shape
    depth = ada_w.shape[0]
    lctx = ctx.shape[1]
    mrows = -(-(bsz + 1) // 16) * 16
    cc = jnp.zeros((mrows, d), F32).at[:bsz].set(c).at[bsz].set(c_ctx)
    cos, sin = _rope_tables(seq)
    xl, xc = x, ctx
    for i in range(depth):
        last = i == depth - 1
        lp = _layer_params(p, i)
        mod = _modulation(cc, ada_w[i], ada_b[i]).reshape(mrows, 6, d)
        ml = [mod[:bsz, n][:, None, :] for n in range(6)]
        mc = [mod[bsz:bsz + 1, n][:, None, :] for n in range(6)]

        hl = _norm_mod(xl, norm1_g[i], ml[0], ml[1])
        hc = _norm_mod(xc, norm1_g[i], mc[0], mc[1])
        zl = _project(hl, lp)
        zc = _project(hc, lp)

        rp_c, lw_c = _rw_prep(zc[0], lp)
        rp_l, lw_l = _rw_prep(zl[0], lp)
        s_zero = jnp.zeros((bsz, 2, RW_HEADS // RW_GROUP, RW_GW, RW_GW), F32)
        of_c, ob_c, s_c = _rw_scan(rp_c, lw_c, s_zero)
        of_l, ob_l, _ = _rw_scan(rp_l, lw_l, s_c)
        a_l = _rw_readout(of_l, ob_l, rp_l, lp)

        xbc_c, dt_c = _ssd_prep(zc[1], lp)
        xbc_l, dt_l = _ssd_prep(zl[1], lp)
        h_zero = jnp.zeros((bsz, 2, SSD_GROUPS, SSD_STATE, SSD_GW), F32)
        yf_c, yb_c, hfin_c = _ssd_scan(xbc_c, dt_c, lp['ssd_a'], h_zero)
        yf_l, yb_l, _ = _ssd_scan(xbc_l, dt_l, lp['ssd_a'], hfin_c)
        b_l = _ssd_readout(yf_l, yb_l, xbc_l, zl[1], lp)

        q_l, k_l, v_l = _da_proj(hl, lp['w_da'], cos, sin, True)
        q_c, k_c, v_c = _da_proj(hc, lp['w_da'], cos, sin, False)
        c_l = _da_attn(q_l, [k_l, k_c], [v_l, v_c], lp['da_lam'], lp['da_g'], lp['da_lam_init'])

        d_l = _na_attn(zl[2], zc[2], *_na_bias(lp['na_rpb'], seq))

        merged = _gated_merge(hl.reshape(bsz * seq, d),
                              [t.reshape(bsz * seq, BR_W) for t in (a_l, b_l, c_l, d_l)],
                              lp['w_gate'], lp['gate_b'], lp['w_br']).reshape(bsz, seq, d)
        xl, hl2 = _matmul_residual_norm(merged, lp['w_out'], xl, ml[2], norm2_g[i], ml[3], ml[4])
        xl = _ffn(xl, hl2, lp, ml[5])

        if not last:
            a_c = _rw_readout(of_c, ob_c, rp_c, lp)
            b_c = _ssd_readout(yf_c, yb_c, xbc_c, zc[1], lp)
            c_c = _da_attn(q_c, [k_c], [v_c], lp['da_lam'], lp['da_g'], lp['da_lam_init'])
            d_c = _ctx_attn(zc[2])
            merged_c = _gated_merge(hc.reshape(bsz * lctx, d),
                                    [t.reshape(bsz * lctx, BR_W) for t in (a_c, b_c, c_c, d_c)],
                                    lp['w_gate'], lp['gate_b'], lp['w_br']).reshape(bsz, lctx, d)
            xc, hc2 = _matmul_residual_norm(merged_c, lp['w_out'], xc, mc[2], norm2_g[i], mc[3], mc[4])
            xc = _ffn(xc, hc2, lp, mc[5])
    return _final_norm(xl, final_norm_g)
```

```python
import functools
import math

import numpy as np
import jax
import jax.numpy as jnp
from jax import lax
from jax.experimental import pallas as pl
from jax.experimental.pallas import tpu as pltpu

F32 = jnp.float32
BF16 = jnp.bfloat16

GRID_W = 64
NORM_EPS = 1e-6
HD = 64
BR_W = 512
RW_HEADS = 8
RW_RANK = 64
RW_GATE_RANK = 128
RW_GN_EPS = 64e-5
RW_IN = 3 * BR_W + 4 * RW_RANK + RW_GATE_RANK
RW_CHUNK = 64
RW_NARR = 9
RW_R, RW_V, RW_KK, RW_KD, RW_BD, RW_G, RW_BONUS = 0, 1, 2, 3, 4, 7, 8
RW_GROUP = 2
RW_GW = RW_GROUP * HD
RW_BATCH = 2
SSD_HEADS = 8
SSD_GROUPS = 2
SSD_STATE = 128
SSD_XBC = BR_W + 2 * SSD_GROUPS * SSD_STATE
SSD_IN = BR_W + SSD_XBC + 2 * SSD_HEADS
SSD_IN_PAD = BR_W + SSD_XBC + 128
SSD_CHUNK = 128
SSD_GW = SSD_HEADS // SSD_GROUPS * HD
DA_HEADS = 4
DA_HPS = 2
DA_IN = 3 * BR_W
DA_SUBLN_EPS = 1e-5
NA_HEADS = 8
NA_IN = 3 * BR_W
NA_WIN_R = 8
NA_WIN_C = 16
NA_QROWS = 4
NA_KROWS = 12
ROPE_BASE = 10000.0
NEG = -1e30
LANES = 128
VMEM_LIMIT = 56 * 1024 * 1024


def _cp(*sem):
    return pltpu.CompilerParams(dimension_semantics=sem, vmem_limit_bytes=VMEM_LIMIT)


def _tile(n, pref, mult):
    t = min(n, pref)
    t -= t % mult
    while t >= mult:
        if n % t == 0:
            return t
        t -= mult
    return n


def _dot(a, b):
    return jnp.dot(a.astype(BF16), b.astype(BF16), preferred_element_type=F32)


def _dot_nt(a, b):
    return lax.dot_general(a.astype(BF16), b.astype(BF16), (((1,), (1,)), ((), ())),
                           preferred_element_type=F32)


def _dot_tn(a, b):
    return lax.dot_general(a.astype(BF16), b.astype(BF16), (((0,), (0,)), ((), ())),
                           preferred_element_type=F32)


def _split3(x):
    hi = x.astype(BF16)
    r1 = x - hi.astype(F32)
    mid = r1.astype(BF16)
    lo = (r1 - mid.astype(F32)).astype(BF16)
    return hi, mid, lo


def _dot01(m01, x):
    hi, mid, lo = _split3(x)
    return (jnp.dot(m01, hi, preferred_element_type=F32)
            + jnp.dot(m01, mid, preferred_element_type=F32)
            + jnp.dot(m01, lo, preferred_element_type=F32))


def _x_dot01(x, m01):
    hi, mid, lo = _split3(x)
    return (jnp.dot(hi, m01, preferred_element_type=F32)
            + jnp.dot(mid, m01, preferred_element_type=F32)
            + jnp.dot(lo, m01, preferred_element_type=F32))


def _sigmoid(x):
    return 1.0 / (1.0 + jnp.exp(-x))


def _silu(x):
    return x * _sigmoid(x)


def _softplus(x):
    return jnp.maximum(x, 0.0) + jnp.log(1.0 + jnp.exp(-jnp.abs(x)))


def _mod_kernel(c_ref, w_ref, b_ref, o_ref):
    a = _silu(c_ref[...])
    o_ref[...] = _dot(a, w_ref[...]) + b_ref[...]


def _modulation(cc, w, b):
    m, d = cc.shape
    n = w.shape[1]
    tn = _tile(n, 1024, LANES)
    return pl.pallas_call(
        _mod_kernel,
        grid=(n // tn,),
        in_specs=[pl.BlockSpec((m, d), lambda j: (0, 0)),
                  pl.BlockSpec((d, tn), lambda j: (0, j)),
                  pl.BlockSpec((1, tn), lambda j: (0, j))],
        out_specs=pl.BlockSpec((m, tn), lambda j: (0, j)),
        out_shape=jax.ShapeDtypeStruct((m, n), F32),
        compiler_params=_cp("parallel"),
        name="adaln_mod",
    )(cc, w, b.reshape(1, n))


def _norm_mod_kernel(x_ref, g_ref, sh_ref, sc_ref, o_ref):
    x = x_ref[0]
    y = x * lax.rsqrt(jnp.mean(x * x, axis=-1, keepdims=True) + NORM_EPS) * g_ref[...]
    o_ref[0] = (y * (1.0 + sc_ref[0]) + sh_ref[0]).astype(o_ref.dtype)


def _norm_mod(x, g, shift, scale):
    bsz, length, d = x.shape
    tr = _tile(length, 256, 16)
    bm = shift.shape[0]
    mod_map = (lambda b, j: (b, 0, 0)) if bm == bsz else (lambda b, j: (0, 0, 0))
    return pl.pallas_call(
        _norm_mod_kernel,
        grid=(bsz, length // tr),
        in_specs=[pl.BlockSpec((1, tr, d), lambda b, j: (b, j, 0)),
                  pl.BlockSpec((1, d), lambda b, j: (0, 0)),
                  pl.BlockSpec((1, 1, d), mod_map),
                  pl.BlockSpec((1, 1, d), mod_map)],
        out_specs=pl.BlockSpec((1, tr, d), lambda b, j: (b, j, 0)),
        out_shape=jax.ShapeDtypeStruct((bsz, length, d), BF16),
        compiler_params=_cp("parallel", "parallel"),
        name="norm_mod",
    )(x, g.reshape(1, d), shift, scale)


def _final_norm_kernel(x_ref, g_ref, o_ref):
    x = x_ref[0]
    o_ref[0] = x * lax.rsqrt(jnp.mean(x * x, axis=-1, keepdims=True) + NORM_EPS) * g_ref[...]


def _final_norm(x, g):
    bsz, length, d = x.shape
    tr = _tile(length, 256, 8)
    return pl.pallas_call(
        _final_norm_kernel,
        grid=(bsz, length // tr),
        in_specs=[pl.BlockSpec((1, tr, d), lambda b, j: (b, j, 0)),
                  pl.BlockSpec((1, d), lambda b, j: (0, 0))],
        out_specs=pl.BlockSpec((1, tr, d), lambda b, j: (b, j, 0)),
        out_shape=jax.ShapeDtypeStruct((bsz, length, d), F32),
        compiler_params=_cp("parallel", "parallel"),
        name="final_norm",
    )(x, g.reshape(1, d))


def _mm_kernel(a_ref, w_ref, o_ref):
    o_ref[...] = jnp.dot(a_ref[...], w_ref[...], preferred_element_type=F32).astype(o_ref.dtype)


def _matmul(a, w, out_dtype, tm_pref=512, tn_pref=2048):
    m, k = a.shape
    n = w.shape[1]
    tm = _tile(m, tm_pref, 16)
    tn = _tile(n, tn_pref, LANES)
    return pl.pallas_call(
        _mm_kernel,
        grid=(m // tm, n // tn),
        in_specs=[pl.BlockSpec((tm, k), lambda i, j: (i, 0)),
                  pl.BlockSpec((k, tn), lambda i, j: (0, j))],
        out_specs=pl.BlockSpec((tm, tn), lambda i, j: (i, j)),
        out_shape=jax.ShapeDtypeStruct((m, n), out_dtype),
        compiler_params=_cp("parallel", "parallel"),
        name="matmul",
    )(a, w)


def _mm_res_kernel(a_ref, w_ref, x_ref, g_ref, o_ref):
    y = jnp.dot(a_ref[0], w_ref[...], preferred_element_type=F32)
    o_ref[0] = x_ref[0] + g_ref[0] * y


def _matmul_residual(a, w, x, gate):
    bsz, length, k = a.shape
    n = w.shape[1]
    tm = _tile(length, 1024, 16)
    tn = _tile(n, 512, LANES)
    bm = gate.shape[0]
    gmap = (lambda b, i, j: (b, 0, j)) if bm == bsz else (lambda b, i, j: (0, 0, j))
    return pl.pallas_call(
        _mm_res_kernel,
        grid=(bsz, length // tm, n // tn),
        in_specs=[pl.BlockSpec((1, tm, k), lambda b, i, j: (b, i, 0)),
                  pl.BlockSpec((k, tn), lambda b, i, j: (0, j)),
                  pl.BlockSpec((1, tm, tn), lambda b, i, j: (b, i, j)),
                  pl.BlockSpec((1, 1, tn), gmap)],
        out_specs=pl.BlockSpec((1, tm, tn), lambda b, i, j: (b, i, j)),
        out_shape=jax.ShapeDtypeStruct((bsz, length, n), F32),
        compiler_params=_cp("parallel", "parallel", "parallel"),
        name="matmul_residual",
    )(a, w, x, gate)


def _mm_res_norm_kernel(a_ref, w_ref, x_ref, g_ref, ng_ref, sh_ref, sc_ref, xo_ref, h_ref):
    xn = x_ref[0] + g_ref[0] * jnp.dot(a_ref[0], w_ref[...], preferred_element_type=F32)
    xo_ref[0] = xn
    y = xn * lax.rsqrt(jnp.mean(xn * xn, axis=-1, keepdims=True) + NORM_EPS) * ng_ref[...]
    h_ref[0] = (y * (1.0 + sc_ref[0]) + sh_ref[0]).astype(h_ref.dtype)


def _matmul_residual_norm(a, w, x, gate, norm_g, shift, scale):
    bsz, length, k = a.shape
    n = w.shape[1]
    tm = _tile(length, 512, 16)
    bm = gate.shape[0]
    mmap = (lambda b, i: (b, 0, 0)) if bm == bsz else (lambda b, i: (0, 0, 0))
    row = pl.BlockSpec((1, tm, n), lambda b, i: (b, i, 0))
    mod = pl.BlockSpec((1, 1, n), mmap)
    return pl.pallas_call(
        _mm_res_norm_kernel,
        grid=(bsz, length // tm),
        in_specs=[pl.BlockSpec((1, tm, k), lambda b, i: (b, i, 0)),
                  pl.BlockSpec((k, n), lambda b, i: (0, 0)),
                  row, mod, pl.BlockSpec((1, n), lambda b, i: (0, 0)), mod, mod],
        out_specs=[row, row],
        out_shape=[jax.ShapeDtypeStruct((bsz, length, n), F32),
                   jax.ShapeDtypeStruct((bsz, length, n), BF16)],
        compiler_params=_cp("parallel", "parallel"),
        name="outproj_norm",
    )(a, w, x, gate, norm_g.reshape(1, n), shift, scale)


def _merge_kernel(h_ref, o0_ref, o1_ref, o2_ref, o3_ref, wg_ref, gb_ref, wbr_ref, out_ref):
    h = h_ref[...]
    acc = None
    for n, o_ref in enumerate((o0_ref, o1_ref, o2_ref, o3_ref)):
        gate = _sigmoid(jnp.dot(h, wg_ref[n], preferred_element_type=F32) + gb_ref[n])
        term = gate * jnp.dot(o_ref[...], wbr_ref[n], preferred_element_type=F32)
        acc = term if acc is None else acc + term
    out_ref[...] = acc.astype(out_ref.dtype)


def _gated_merge(h, branches, wg, gb, wbr):
    m, d = h.shape
    nb = len(branches)
    tm = _tile(m, 1024, 16)
    tn = _tile(d, 512, LANES)
    bspec = pl.BlockSpec((tm, BR_W), lambda i, j: (i, 0))
    return pl.pallas_call(
        _merge_kernel,
        grid=(m // tm, d // tn),
        in_specs=[pl.BlockSpec((tm, d), lambda i, j: (i, 0)), bspec, bspec, bspec, bspec,
                  pl.BlockSpec((nb, d, tn), lambda i, j: (0, 0, j)),
                  pl.BlockSpec((nb, 1, tn), lambda i, j: (0, 0, j)),
                  pl.BlockSpec((nb, BR_W, tn), lambda i, j: (0, 0, j))],
        out_specs=pl.BlockSpec((tm, tn), lambda i, j: (i, j)),
        out_shape=jax.ShapeDtypeStruct((m, d), BF16),
        compiler_params=_cp("parallel", "parallel"),
        name="gated_merge",
    )(h, *branches, wg, gb, wbr)


def _conv3(x, prev_row, next_row, w_ref, b_ref):
    rows = x.shape[0]
    rid = lax.broadcasted_iota(jnp.int32, x.shape, 0)
    prev = jnp.where(rid == 0, prev_row, pltpu.roll(x, 1, 0))
    nxt = jnp.where(rid == rows - 1, next_row, pltpu.roll(x, rows - 1, 0))
    return b_ref[...] + prev * w_ref[0:1, :] + x * w_ref[1:2, :] + nxt * w_ref[2:3, :]


def _ffn_up_kernel(*refs, nt, halo):
    if nt == 1:
        a_ref = refs[0]
        refs = refs[1:]
        halo = 0
        a_ext = a_ref[0]
    else:
        a_ref, ap_ref, an_ref = refs[0:3]
        refs = refs[3:]
        a_ext = jnp.concatenate([ap_ref[0], a_ref[0], an_ref[0]], axis=0)
    wg_ref, wv_ref, cwg_ref, cbg_ref, cwv_ref, cbv_ref, o_ref = refs
    i = pl.program_id(1)
    tm = a_ref.shape[1]
    n_ext = tm + 2 * halo
    rid = lax.broadcasted_iota(jnp.int32, (tm, 1), 0)
    no_prev = jnp.logical_and(i == 0, rid == 0)
    no_next = jnp.logical_and(i == nt - 1, rid == tm - 1)

    def branch(w_ref, cw_ref, cb_ref):
        u = jnp.dot(a_ext, w_ref[...], preferred_element_type=F32)
        um1 = jnp.where(no_prev, 0.0, pltpu.roll(u, 1, 0)[halo:halo + tm])
        up1 = jnp.where(no_next, 0.0, pltpu.roll(u, n_ext - 1, 0)[halo:halo + tm])
        return (cb_ref[...] + um1 * cw_ref[0:1, :] + u[halo:halo + tm] * cw_ref[1:2, :]
                + up1 * cw_ref[2:3, :])

    gate = branch(wg_ref, cwg_ref, cbg_ref)
    val = branch(wv_ref, cwv_ref, cbv_ref)
    o_ref[0] = (_silu(gate) * val).astype(o_ref.dtype)


def _ffn_up(h, w_up, conv_w, conv_b):
    bsz, length, d = h.shape
    f2 = w_up.shape[1]
    f = f2 // 2
    halo = 16
    tm = _tile(length, 1024, halo)
    tc = _tile(f, 512, LANES)
    nt = length // tm
    nc = f // tc
    nh = length // halo
    rb = tm // halo

    def wspec(rows, off):
        return pl.BlockSpec((rows, tc), lambda b, i, c: (0, c + off))

    a_specs = [pl.BlockSpec((1, tm, d), lambda b, i, c: (b, i, 0))]
    if nt > 1:
        a_specs += [
            pl.BlockSpec((1, halo, d), lambda b, i, c: (b, jnp.maximum(i * rb - 1, 0), 0)),
            pl.BlockSpec((1, halo, d), lambda b, i, c: (b, jnp.minimum((i + 1) * rb, nh - 1), 0))]
    cb = conv_b.reshape(1, f2)
    return pl.pallas_call(
        functools.partial(_ffn_up_kernel, nt=nt, halo=halo),
        grid=(bsz, nt, nc),
        in_specs=a_specs + [wspec(d, 0), wspec(d, nc), wspec(3, 0), wspec(1, 0), wspec(3, nc),
                            wspec(1, nc)],
        out_specs=pl.BlockSpec((1, tm, tc), lambda b, i, c: (b, i, c)),
        out_shape=jax.ShapeDtypeStruct((bsz, length, f), BF16),
        compiler_params=_cp("parallel", "parallel", "parallel"),
        name="ffn_up",
    )(*([h] * len(a_specs)), w_up, w_up, conv_w, cb, conv_w, cb)


def _rw_prep_kernel(z_ref, zp_ref, zn_ref, mu_ref, w0_ref, wup_ref, a0_ref, aup_ref, gup_ref,
                    kk_ref, ka_ref, rk_ref, ones_ref, o_ref, lw_out_ref, *, nt):
    j = pl.program_id(1)
    p = z_ref[0]
    rows = p.shape[0]
    rid = lax.broadcasted_iota(jnp.int32, p.shape, 0)
    prow = jnp.where(j > 0, zp_ref[0, 7:8, :], 0.0)
    nrow = jnp.where(j < nt - 1, zn_ref[0, 0:1, :], 0.0)
    prev = jnp.where(rid == 0, prow, pltpu.roll(p, 1, 0))
    nxt = jnp.where(rid == rows - 1, nrow, pltpu.roll(p, rows - 1, 0))
    ps = p + mu_ref[0:1, :] * (prev - p) + mu_ref[1:2, :] * (nxt - p)

    w = BR_W
    r = ps[:, 0:w]
    k = ps[:, w:2 * w]
    v = ps[:, 2 * w:3 * w]
    wd = ps[:, 3 * w:3 * w + 128]
    ad = ps[:, 3 * w + 128:3 * w + 256]
    gd = ps[:, 3 * w + 256:3 * w + 384]

    wraw = _dot(jnp.tanh(wd), wup_ref[...])
    araw = _dot(ad, aup_ref[...])
    g = _dot(_sigmoid(gd), gup_ref[...])
    ones = ones_ref[...]
    kkv = k * kk_ref[...]
    ss = _x_dot01(kkv * kkv, ones)
    kkn = kkv / jnp.maximum(jnp.sqrt(ss), 1e-12)

    def put(i, x):
        o_ref[0, :, i * w:(i + 1) * w] = x.astype(o_ref.dtype)

    put(RW_R, r)
    put(RW_V, v)
    put(RW_KK, kkn)
    ksum = None
    for d in range(2):
        lw = -math.exp(-0.5) * _sigmoid(w0_ref[d:d + 1, :] + wraw[:, d * w:(d + 1) * w])
        lw_out_ref[0, :, d * w:(d + 1) * w] = lw
        a = _sigmoid(a0_ref[d:d + 1, :] + araw[:, d * w:(d + 1) * w])
        kd = k * (1.0 + (a - 1.0) * ka_ref[...])
        put(RW_KD + 2 * d, kd)
        put(RW_BD + 2 * d, kkn * a)
        ksum = kd if ksum is None else ksum + kd
    put(RW_G, g)
    put(RW_BONUS, _x_dot01(r * rk_ref[...] * ksum, ones) * v)


def _rw_prep(z, lp):
    bsz, length, fin = z.shape
    tr = _tile(length, 256, 8)
    nt = length // tr
    rb = tr // 8
    nh = length // 8
    full = lambda a: pl.BlockSpec(a.shape, lambda b, j: (0,) * a.ndim)
    params = (lp['rw_mu'], lp['rw_w0'], lp['rw_wup'], lp['rw_a0'], lp['rw_aup'], lp['rw_gup'],
              lp['rw_kk'], lp['rw_ka'], lp['rw_rk'], lp['ones_hd'])
    return pl.pallas_call(
        functools.partial(_rw_prep_kernel, nt=nt),
        grid=(bsz, nt),
        in_specs=[pl.BlockSpec((1, tr, fin), lambda b, j: (b, j, 0)),
                  pl.BlockSpec((1, 8, fin), lambda b, j: (b, jnp.maximum(j * rb - 1, 0), 0)),
                  pl.BlockSpec((1, 8, fin), lambda b, j: (b, jnp.minimum((j + 1) * rb, nh - 1), 0))]
                 + [full(a) for a in params],
        out_specs=[pl.BlockSpec((1, tr, RW_NARR * BR_W), lambda b, j: (b, j, 0)),
                   pl.BlockSpec((1, tr, 2 * BR_W), lambda b, j: (b, j, 0))],
        out_shape=[jax.ShapeDtypeStruct((bsz, length, RW_NARR * BR_W), BF16),
                   jax.ShapeDtypeStruct((bsz, length, 2 * BR_W), F32)],
        compiler_params=_cp("parallel", "parallel"),
        name="rwkv_prep",
    )(z, z, z, *params)


def _rw_chunks(dir_refs, o_refs, s_scr, nb):
    cs = RW_CHUNK
    gw = RW_GW
    ng = RW_HEADS // RW_GROUP
    ri = lax.broadcasted_iota(jnp.int32, (gw, gw), 0)
    ci = lax.broadcasted_iota(jnp.int32, (gw, gw), 1)
    head_mask = (ri >> 6) == (ci >> 6)
    tw = lax.broadcasted_iota(jnp.int32, (cs, gw), 0)
    iw = lax.broadcasted_iota(jnp.int32, (cs, gw), 1) & (cs - 1)
    eye = jnp.where(iw == tw, 1.0, 0.0)
    same = {s: (tw >> int(math.log2(s))) == (iw >> int(math.log2(s))) for s in (8, 16, 32)}
    ti = lax.broadcasted_iota(jnp.int32, (cs, cs), 0)
    ii = lax.broadcasted_iota(jnp.int32, (cs, cs), 1)

    def expand(x):
        xb = x.astype(BF16)
        return jnp.where(head_mask, jnp.concatenate([xb] * RW_GROUP, axis=0), jnp.zeros_like(xb[0:1, 0:1]))

    def mm(a, b):
        return _dot(a, expand(b))

    chains = []
    for bb in range(nb):
        for d, (r_ref, v_ref, kk_ref, lw_ref, k_ref, b_ref) in enumerate(dir_refs):
            sgn = 1 - 2 * d
            rel = (iw - tw) * sgn
            tri = jnp.where((ii - ti) * sgn <= 0, 1.0, 0.0).astype(BF16)
            lw = lw_ref[bb]
            cum = _dot01(tri, lw)
            tot = jnp.sum(lw, axis=0, keepdims=True)
            kd, bd = k_ref[bb].astype(F32), b_ref[bb].astype(F32)
            qt = kk_ref[bb].astype(F32) * jnp.exp(cum - lw)
            rt = r_ref[bb].astype(F32) * jnp.exp(cum)
            e_neg = jnp.exp(-cum)
            e_end = jnp.exp(tot - cum)
            kt = kd * e_neg
            bt = bd * e_neg
            kh = kd * e_end
            bh = bd * e_end
            p_tot = jnp.exp(tot)
            v = v_ref[bb].astype(F32)
            for g in range(ng):
                sl = slice(g * gw, (g + 1) * gw)
                chains.append(dict(bb=bb, d=d, g=g, sl=sl, strict=rel < 0, incl=rel <= 0,
                                   q=qt[:, sl], r=rt[:, sl], v=v[:, sl], kt=kt[:, sl],
                                   bt=bt[:, sl], kh=kh[:, sl], bh=bh[:, sl], p_tot=p_tot[:, sl],
                                   s0=s_scr[bb, d, g]))

    for c in chains:
        c['kte'], c['bte'], c['ve'] = expand(c['kt']), expand(c['bt']), expand(c['v'])
        c['s0b'] = c['s0'].astype(BF16)
    for c in chains:
        c['lmat'] = jnp.where(c['strict'], _dot_nt(c['q'], c['bte']), 0.0)
        c['ld'] = jnp.where(same[8], c['lmat'], 0.0)
    for c in chains:
        c['ld2'] = mm(c['ld'], c['ld'])
    for c in chains:
        c['a_qk'] = jnp.where(c['strict'], _dot_nt(c['q'], c['kte']), 0.0)
    for c in chains:
        c['ld4'] = mm(c['ld2'], c['ld2'])
    for c in chains:
        c['x'] = mm(eye - c['ld'], eye + c['ld2'])
    for c in chains:
        c['a_rk'] = jnp.where(c['incl'], _dot_nt(c['r'], c['kte']), 0.0)
    for c in chains:
        c['x'] = mm(c['x'], eye + c['ld4'])
    for c in chains:
        c['a_rb'] = jnp.where(c['incl'], _dot_nt(c['r'], c['bte']), 0.0)
    for c in chains:
        c['rhs'] = _dot_nt(c['q'], c['s0b']) + _dot(c['a_qk'], c['ve'])
    for s in (8, 16, 32):
        off_mask = jnp.logical_not(same[s])
        if 2 * s < cs:
            off_mask = jnp.logical_and(same[2 * s], off_mask)
        for c in chains:
            c['t'] = mm(c['x'], jnp.where(off_mask, c['lmat'], 0.0))
        if s == 8:
            for c in chains:
                c['o'] = _dot_nt(c['r'], c['s0b']) + _dot(c['a_rk'], c['ve'])
        for c in chains:
            c['x'] = c['x'] - mm(c['t'], c['x'])
    for c in chains:
        c['sa'] = mm(c['x'], c['rhs'])
    for c in chains:
        o_refs[c['d']][c['bb'], :, c['sl']] = c['o'] - mm(c['a_rb'], c['sa'])
    for c in chains:
        upd = _dot_tn(jnp.concatenate([c['v'], c['sa']], axis=0),
                      jnp.concatenate([c['kh'], -c['bh']], axis=0))
        s_scr[c['bb'], c['d'], c['g']] = c['s0'] * c['p_tot'] + jnp.where(head_mask, upd, 0.0)


def _rw_scan_kernel(*refs, nc, nb):
    fwd, bwd = refs[0:6], refs[6:12]
    s0_ref, of_ref, ob_ref, sfin_ref, s_scr = refs[12:]
    c = pl.program_id(1)

    @pl.when(c == 0)
    def _():
        s_scr[...] = s0_ref[...]

    _rw_chunks((fwd, bwd), (of_ref, ob_ref), s_scr, nb)

    @pl.when(c == nc - 1)
    def _():
        sfin_ref[...] = s_scr[...]


def _rw_scan(rwp, lwp, s_init):
    bsz, length, _ = rwp.shape
    cs = RW_CHUNK
    nc = length // cs
    ng = RW_HEADS // RW_GROUP
    nb = RW_BATCH if bsz % RW_BATCH == 0 else 1

    def arr(d, i):
        if d == 0:
            return pl.BlockSpec((nb, cs, BR_W), lambda b, c: (b, c, i))
        return pl.BlockSpec((nb, cs, BR_W), lambda b, c: (b, nc - 1 - c, i))

    def specs(d):
        return [arr(d, RW_R), arr(d, RW_V), arr(d, RW_KK), arr(d, d),
                arr(d, RW_KD + 2 * d), arr(d, RW_BD + 2 * d)]

    def operands():
        return [rwp, rwp, rwp, lwp, rwp, rwp]

    sspec = pl.BlockSpec((nb, 2, ng, RW_GW, RW_GW), lambda b, c: (b, 0, 0, 0, 0))
    oshape = jax.ShapeDtypeStruct((bsz, length, BR_W), F32)
    return pl.pallas_call(
        functools.partial(_rw_scan_kernel, nc=nc, nb=nb),
        grid=(bsz // nb, nc),
        in_specs=specs(0) + specs(1) + [sspec],
        out_specs=[pl.BlockSpec((nb, cs, BR_W), lambda b, c: (b, c, 0)),
                   pl.BlockSpec((nb, cs, BR_W), lambda b, c: (b, nc - 1 - c, 0)), sspec],
        out_shape=[oshape, oshape, jax.ShapeDtypeStruct((bsz, 2, ng, RW_GW, RW_GW), F32)],
        scratch_shapes=[pltpu.VMEM((nb, 2, ng, RW_GW, RW_GW), F32)],
        compiler_params=_cp("parallel", "arbitrary"),
        name="rwkv_scan",
    )(*operands(), *operands(), s_init)


def _rw_readout_kernel(of_ref, ob_ref, g_ref, bonus_ref, lng_ref, lnb_ref, mean_ref, out_ref):
    o = of_ref[0] + ob_ref[0]
    mean_m = mean_ref[...]
    mu = _x_dot01(o, mean_m)
    dlt = o - mu
    var = _x_dot01(dlt * dlt, mean_m)
    on = dlt * lax.rsqrt(var + RW_GN_EPS) * lng_ref[...] + lnb_ref[...]
    out_ref[0] = ((on + bonus_ref[0].astype(F32)) * g_ref[0].astype(F32)).astype(out_ref.dtype)


def _rw_readout(o_f, o_b, rwp, lp):
    bsz, length, w = o_f.shape
    tr = _tile(length, 256, 16)
    ospec = pl.BlockSpec((1, tr, w), lambda b, j: (b, j, 0))
    return pl.pallas_call(
        _rw_readout_kernel,
        grid=(bsz, length // tr),
        in_specs=[ospec, ospec,
                  pl.BlockSpec((1, tr, w), lambda b, j: (b, j, RW_G)),
                  pl.BlockSpec((1, tr, w), lambda b, j: (b, j, RW_BONUS)),
                  pl.BlockSpec((1, w), lambda b, j: (0, 0)),
                  pl.BlockSpec((1, w), lambda b, j: (0, 0)),
                  pl.BlockSpec((w, w), lambda b, j: (0, 0))],
        out_specs=ospec,
        out_shape=jax.ShapeDtypeStruct((bsz, length, w), BF16),
        compiler_params=_cp("parallel", "parallel"),
        name="rwkv_readout",
    )(o_f, o_b, rwp, rwp, lp['rw_lng'], lp['rw_lnb'], lp['mean_hd'])


def _ssd_prep_kernel(z_ref, zp_ref, zn_ref, cw_ref, cb_ref, dtb_ref, xbc_ref, dt_ref, *, nt):
    j = pl.program_id(1)
    x = z_ref[0, :, BR_W:BR_W + SSD_XBC]
    prow = jnp.where(j > 0, zp_ref[0, 7:8, BR_W:BR_W + SSD_XBC], 0.0)
    nrow = jnp.where(j < nt - 1, zn_ref[0, 0:1, BR_W:BR_W + SSD_XBC], 0.0)
    xbc_ref[0] = _silu(_conv3(x, prow, nrow, cw_ref, cb_ref))
    dt_raw = z_ref[0, :, BR_W + SSD_XBC:BR_W + SSD_XBC + LANES]
    dt_ref[0, 0] = _softplus(dt_raw + dtb_ref[0:1, :])
    dt_ref[0, 1] = _softplus(pltpu.roll(dt_raw, LANES - SSD_HEADS, 1) + dtb_ref[1:2, :])


def _ssd_prep(z, lp):
    bsz, length, fin = z.shape
    tr = _tile(length, 256, 8)
    nt = length // tr
    rb = tr // 8
    nh = length // 8
    return pl.pallas_call(
        functools.partial(_ssd_prep_kernel, nt=nt),
        grid=(bsz, nt),
        in_specs=[pl.BlockSpec((1, tr, fin), lambda b, j: (b, j, 0)),
                  pl.BlockSpec((1, 8, fin), lambda b, j: (b, jnp.maximum(j * rb - 1, 0), 0)),
                  pl.BlockSpec((1, 8, fin), lambda b, j: (b, jnp.minimum((j + 1) * rb, nh - 1), 0)),
                  pl.BlockSpec((3, SSD_XBC), lambda b, j: (0, 0)),
                  pl.BlockSpec((1, SSD_XBC), lambda b, j: (0, 0)),
                  pl.BlockSpec((2, LANES), lambda b, j: (0, 0))],
        out_specs=[pl.BlockSpec((1, tr, SSD_XBC), lambda b, j: (b, j, 0)),
                   pl.BlockSpec((1, 2, tr, LANES), lambda b, j: (b, 0, j, 0))],
        out_shape=[jax.ShapeDtypeStruct((bsz, length, SSD_XBC), F32),
                   jax.ShapeDtypeStruct((bsz, 2, length, LANES), F32)],
        compiler_params=_cp("parallel", "parallel"),
        name="ssd_prep",
    )(z, z, z, lp['ssd_cw'], lp['ssd_cb'], lp['ssd_dtb'])


def _ssd_chunks(dir_refs, a_ref, y_refs, h_scr, nb):
    q = SSD_CHUNK
    n = SSD_STATE
    rep = SSD_HEADS // SSD_GROUPS
    gw = SSD_GW
    ti = lax.broadcasted_iota(jnp.int32, (q, q), 0)
    ii = lax.broadcasted_iota(jnp.int32, (q, q), 1)
    ri = lax.broadcasted_iota(jnp.int32, (rep * q, gw), 0)
    ci = lax.broadcasted_iota(jnp.int32, (rep * q, gw), 1)
    head_mask = (ri >> int(math.log2(q))) == (ci >> int(math.log2(HD)))

    def per_head(x, h0):
        return jnp.concatenate([jnp.broadcast_to(x[:, h0 + h:h0 + h + 1], (x.shape[0], HD))
                                for h in range(rep)], axis=1)

    work = []
    for bb in range(nb):
        for d, (xbc_ref, dt_ref) in enumerate(dir_refs):
            incl = (ii - ti) * (1 - 2 * d) <= 0
            tri = jnp.where(incl, 1.0, 0.0).astype(BF16)
            dt = dt_ref[bb, 0]
            dta = dt * a_ref[d]
            acs = _dot01(tri, dta)
            acs_t = acs.T
            tot = jnp.sum(dta, axis=0, keepdims=True)
            e_acs = jnp.exp(acs)
            e_end = jnp.exp(tot - acs)
            e_tot = jnp.exp(tot)
            xbc = xbc_ref[bb]
            for g in range(SSD_GROUPS):
                h0 = g * rep
                bg = xbc[:, BR_W + g * n:BR_W + (g + 1) * n]
                cg = xbc[:, BR_W + SSD_GROUPS * n + g * n:BR_W + SSD_GROUPS * n + (g + 1) * n]
                xdt = xbc[:, g * gw:(g + 1) * gw] * per_head(dt, h0)
                dec = jnp.concatenate(
                    [jnp.exp(jnp.where(incl, acs[:, h0 + h:h0 + h + 1] - acs_t[h0 + h:h0 + h + 1, :], NEG))
                     for h in range(rep)], axis=1)
                work.append(dict(bb=bb, d=d, g=g, bg=bg, cg=cg, xdt=xdt, dec=dec,
                                 ht=h_scr[bb, d, g], e_acs=per_head(e_acs, h0),
                                 e_end=per_head(e_end, h0), e_tot=per_head(e_tot, h0)))
    for w in work:
        w['cb'] = _dot_nt(w['cg'], w['bg'])
    for w in work:
        w['yoff'] = _dot(w['cg'], w['ht'])
    for w in work:
        w['st'] = _dot_tn(w['bg'], w['xdt'] * w['e_end'])
    for w in work:
        xe = jnp.where(head_mask, jnp.concatenate([w['xdt']] * rep, axis=0), 0.0)
        scores = jnp.concatenate([w['cb']] * rep, axis=1) * w['dec']
        y = _dot(scores, xe) + w['e_acs'] * w['yoff']
        y_refs[w['d']][w['bb'], :, w['g'] * gw:(w['g'] + 1) * gw] = y
        h_scr[w['bb'], w['d'], w['g']] = w['ht'] * w['e_tot'] + w['st']


def _ssd_scan_kernel(xf_ref, dtf_ref, xb_ref, dtb_ref, a_ref, h0_ref, yf_ref, yb_ref, hfin_ref,
                     h_scr, *, nc, nb):
    c = pl.program_id(1)

    @pl.when(c == 0)
    def _():
        h_scr[...] = h0_ref[...]

    _ssd_chunks(((xf_ref, dtf_ref), (xb_ref, dtb_ref)), a_ref, (yf_ref, yb_ref), h_scr, nb)

    @pl.when(c == nc - 1)
    def _():
        hfin_ref[...] = h_scr[...]


def _ssd_scan(xbc, dt, a_neg, h_init):
    bsz, length, _ = xbc.shape
    q = SSD_CHUNK
    nc = length // q
    nb = RW_BATCH if bsz % RW_BATCH == 0 else 1
    hspec = pl.BlockSpec((nb, 2, SSD_GROUPS, SSD_STATE, SSD_GW), lambda b, c: (b, 0, 0, 0, 0))
    oshape = jax.ShapeDtypeStruct((bsz, length, BR_W), F32)
    return pl.pallas_call(
        functools.partial(_ssd_scan_kernel, nc=nc, nb=nb),
        grid=(bsz // nb, nc),
        in_specs=[pl.BlockSpec((nb, q, SSD_XBC), lambda b, c: (b, c, 0)),
                  pl.BlockSpec((nb, 1, q, LANES), lambda b, c: (b, 0, c, 0)),
                  pl.BlockSpec((nb, q, SSD_XBC), lambda b, c: (b, nc - 1 - c, 0)),
                  pl.BlockSpec((nb, 1, q, LANES), lambda b, c: (b, 1, nc - 1 - c, 0)),
                  pl.BlockSpec((2, 1, LANES), lambda b, c: (0, 0, 0)),
                  hspec],
        out_specs=[pl.BlockSpec((nb, q, BR_W), lambda b, c: (b, c, 0)),
                   pl.BlockSpec((nb, q, BR_W), lambda b, c: (b, nc - 1 - c, 0)), hspec],
        out_shape=[oshape, oshape,
                   jax.ShapeDtypeStruct((bsz, 2, SSD_GROUPS, SSD_STATE, SSD_GW), F32)],
        scratch_shapes=[pltpu.VMEM((nb, 2, SSD_GROUPS, SSD_STATE, SSD_GW), F32)],
        compiler_params=_cp("parallel", "arbitrary"),
        name="ssd_scan",
    )(xbc, dt, xbc, dt, a_neg, h_init)


def _ssd_readout_kernel(yf_ref, yb_ref, xs_ref, z_ref, dsk_ref, ng_ref, out_ref):
    y = yf_ref[0] + yb_ref[0] + dsk_ref[...] * xs_ref[0]
    y = y * _silu(z_ref[0])
    y = y * lax.rsqrt(jnp.mean(y * y, axis=-1, keepdims=True) + NORM_EPS) * ng_ref[...]
    out_ref[0] = y.astype(out_ref.dtype)


def _ssd_readout(y_f, y_b, xbc, z, lp):
    bsz, length, w = y_f.shape
    tr = _tile(length, 256, 16)
    ospec = pl.BlockSpec((1, tr, w), lambda b, j: (b, j, 0))
    return pl.pallas_call(
        _ssd_readout_kernel,
        grid=(bsz, length // tr),
        in_specs=[ospec, ospec, ospec, ospec,
                  pl.BlockSpec((1, w), lambda b, j: (0, 0)),
                  pl.BlockSpec((1, w), lambda b, j: (0, 0))],
        out_specs=ospec,
        out_shape=jax.ShapeDtypeStruct((bsz, length, w), BF16),
        compiler_params=_cp("parallel", "parallel"),
        name="ssd_readout",
    )(y_f, y_b, xbc, z, lp['ssd_dskip'], lp['ssd_ng'])


def _da_proj_kernel(h_ref, w_ref, cos_ref, sin_ref, q_ref, k_ref, v_ref, *, rope):
    w = BR_W
    z = jnp.dot(h_ref[0], w_ref[...], preferred_element_type=F32)

    def rot(x):
        if not rope:
            return x
        lane = lax.broadcasted_iota(jnp.int32, (1, LANES), 1)
        first = (lane & 31) < 16
        outs = []
        for cb in range(w // LANES):
            xb = x[:, cb * LANES:(cb + 1) * LANES]
            sw = jnp.where(first, pltpu.roll(xb, LANES - 16, 1), pltpu.roll(xb, 16, 1))
            outs.append(xb * cos_ref[:, cb * LANES:(cb + 1) * LANES]
                        + sw * sin_ref[:, cb * LANES:(cb + 1) * LANES])
        return jnp.concatenate(outs, axis=1)

    q_ref[0] = (rot(z[:, 0:w]) * (HD ** -0.5)).astype(q_ref.dtype)
    k_ref[0] = rot(z[:, w:2 * w]).astype(k_ref.dtype)
    v_ref[0] = z[:, 2 * w:3 * w].astype(v_ref.dtype)


def _da_proj(h, w_da, cos, sin, rope):
    bsz, length, d = h.shape
    tr = _tile(length, 512, 16)
    w = BR_W
    ospec = pl.BlockSpec((1, tr, w), lambda b, j: (b, j, 0))
    oshape = jax.ShapeDtypeStruct((bsz, length, w), BF16)
    return pl.pallas_call(
        functools.partial(_da_proj_kernel, rope=rope),
        grid=(bsz, length // tr),
        in_specs=[pl.BlockSpec((1, tr, d), lambda b, j: (b, j, 0)),
                  pl.BlockSpec((d, 3 * w), lambda b, j: (0, 0)),
                  pl.BlockSpec((tr, w), lambda b, j: (j, 0)),
                  pl.BlockSpec((tr, w), lambda b, j: (j, 0))],
        out_specs=[ospec, ospec, ospec],
        out_shape=[oshape, oshape, oshape],
        compiler_params=_cp("parallel", "parallel"),
        name="da_proj",
    )(h, w_da, cos, sin)


def _da_attn_kernel(*refs, nsrc, lam_init):
    lam_ref, q_ref = refs[0], refs[1]
    kv = refs[2:2 + 2 * nsrc]
    g_ref, o_ref = refs[2 + 2 * nsrc], refs[3 + 2 * nsrc]
    lane = lax.broadcasted_iota(jnp.int32, (1, LANES), 1)
    lam = lam_ref[0:1, 0:1]
    heads = [slice(hh * LANES, (hh + 1) * LANES) for hh in range(DA_HPS)]
    scores = {}
    for hh, hs in enumerate(heads):
        q = q_ref[0, :, hs]
        zero = jnp.zeros_like(q)
        for m in range(2):
            qm = jnp.where((lane < HD) if m == 0 else (lane >= HD), q, zero)
            scores[hh, m] = [_dot_nt(qm, kv[2 * i][0, :, hs]) for i in range(nsrc)]
    for hh, hs in enumerate(heads):
        probs = []
        for m in range(2):
            ss = scores[hh, m]
            mx = functools.reduce(jnp.maximum, [jnp.max(s, axis=-1, keepdims=True) for s in ss])
            es = [jnp.exp(s - mx) for s in ss]
            den = functools.reduce(lambda a, b: a + b,
                                   [jnp.sum(e, axis=-1, keepdims=True) for e in es])
            inv = 1.0 / den
            probs.append([e * inv for e in es])
        o = None
        for i in range(nsrc):
            a = probs[0][i] - lam * probs[1][i]
            t = _dot(a, kv[2 * i + 1][0, :, hs])
            o = t if o is None else o + t
        y = o * lax.rsqrt(jnp.mean(o * o, axis=-1, keepdims=True) + DA_SUBLN_EPS) * g_ref[...]
        o_ref[0, :, hs] = (y * (1.0 - lam_init)).astype(o_ref.dtype)


def _da_attn(q, ks, vs, lam, subln_g, lam_init):
    bsz, lq, _ = q.shape
    tq = _tile(lq, 256, 16)
    nsrc = len(ks)
    hw = DA_HPS * LANES
    in_specs = [pl.BlockSpec((1, LANES), lambda b, h, j: (0, 0)),
                pl.BlockSpec((1, tq, hw), lambda b, h, j: (b, j, h))]
    args = [lam, q]
    for k, v in zip(ks, vs):
        lk = k.shape[1]
        in_specs.append(pl.BlockSpec((1, lk, hw), lambda b, h, j: (b, 0, h)))
        in_specs.append(pl.BlockSpec((1, lk, hw), lambda b, h, j: (b, 0, h)))
        args += [k, v]
    in_specs.append(pl.BlockSpec((1, LANES), lambda b, h, j: (0, 0)))
    args.append(subln_g)
    return pl.pallas_call(
        functools.partial(_da_attn_kernel, nsrc=nsrc, lam_init=lam_init),
        grid=(bsz, DA_HEADS // DA_HPS, lq // tq),
        in_specs=in_specs,
        out_specs=pl.BlockSpec((1, tq, hw), lambda b, h, j: (b, j, h)),
        out_shape=jax.ShapeDtypeStruct((bsz, lq, BR_W), BF16),
        compiler_params=_cp("parallel", "parallel", "parallel"),
        name="diff_attn",
    )(*args)


def _na_kb(i, rows):
    return jnp.clip(i * NA_QROWS - NA_WIN_R // 2, 0, rows - NA_KROWS)


def _na_attn_kernel(q_ref, k_ref, v_ref, kc_ref, vc_ref, *rest, rows, nsub):
    bias_refs, o_ref = rest[:nsub], rest[nsub]
    i = pl.program_id(2)
    tq = NA_QROWS * GRID_W
    nk = NA_KROWS * GRID_W
    kc = kc_ref[0]
    vc = vc_ref[0]
    lane = lax.broadcasted_iota(jnp.int32, (1, LANES), 1)
    for j in range(nsub):
        start = pl.multiple_of(_na_kb(i * nsub + j, rows) * GRID_W, GRID_W)
        q = q_ref[0, j * tq:(j + 1) * tq, :]
        zero = jnp.zeros_like(q)
        kw = k_ref[0, pl.ds(start, nk), :]
        vw = v_ref[0, pl.ds(start, nk), :]
        outs = []
        for h in range(2):
            qm = jnp.where((lane < HD) if h == 0 else (lane >= HD), q, zero)
            sw = _dot_nt(qm, kw) + bias_refs[j][h, 0]
            sc = _dot_nt(qm, kc)
            mx = jnp.maximum(jnp.max(sw, axis=-1, keepdims=True), jnp.max(sc, axis=-1, keepdims=True))
            ew = jnp.exp(sw - mx)
            ec = jnp.exp(sc - mx)
            inv = 1.0 / (jnp.sum(ew, axis=-1, keepdims=True) + jnp.sum(ec, axis=-1, keepdims=True))
            outs.append((_dot(ew, vw) + _dot(ec, vc)) * inv)
        o_ref[0, j * tq:(j + 1) * tq, :] = jnp.where(lane < HD, outs[0], outs[1]).astype(o_ref.dtype)


def _na_attn(qkv_l, qkv_c, bias, pat_id):
    q_l, k_l, v_l = qkv_l
    _, k_c, v_c = qkv_c
    bsz, length, _ = q_l.shape
    lc = k_c.shape[1]
    rows = length // GRID_W
    tq = NA_QROWS * GRID_W
    nblk = length // tq
    nsub = 2 if nblk % 2 == 0 else 1

    def pat(i):
        out = pat_id[0]
        for j in range(1, nblk):
            if pat_id[j] != pat_id[j - 1]:
                out = out + jnp.where(i >= j, pat_id[j] - pat_id[j - 1], 0)
        return out

    def bias_spec(j):
        return pl.BlockSpec((2, 1, tq, NA_KROWS * GRID_W),
                            lambda b, p, i: (p, pat(i * nsub + j), 0, 0))

    qspec = pl.BlockSpec((1, nsub * tq, LANES), lambda b, p, i: (b, i, p))
    whole = lambda n: pl.BlockSpec((1, n, LANES), lambda b, p, i: (b, 0, p))
    return pl.pallas_call(
        functools.partial(_na_attn_kernel, rows=rows, nsub=nsub),
        grid=(bsz, NA_HEADS // 2, nblk // nsub),
        in_specs=[qspec, whole(length), whole(length), whole(lc), whole(lc)]
                 + [bias_spec(j) for j in range(nsub)],
        out_specs=qspec,
        out_shape=jax.ShapeDtypeStruct((bsz, length, BR_W), BF16),
        compiler_params=_cp("parallel", "parallel", "parallel"),
        name="na_attn",
    )(q_l, k_l, v_l, k_c, v_c, *([bias] * nsub))


def _ctx_attn_kernel(q_ref, k_ref, v_ref, o_ref):
    q = q_ref[0]
    zero = jnp.zeros_like(q)
    k = k_ref[0]
    v = v_ref[0]
    lane = lax.broadcasted_iota(jnp.int32, (1, LANES), 1)
    outs = []
    for h in range(2):
        qm = jnp.where((lane < HD) if h == 0 else (lane >= HD), q, zero)
        s = _dot_nt(qm, k)
        e = jnp.exp(s - jnp.max(s, axis=-1, keepdims=True))
        outs.append(_dot(e, v) * (1.0 / jnp.sum(e, axis=-1, keepdims=True)))
    o_ref[0] = jnp.where(lane < HD, outs[0], outs[1]).astype(o_ref.dtype)


def _ctx_attn(qkv_c):
    q_c, k_c, v_c = qkv_c
    bsz, lc, _ = q_c.shape
    spec = pl.BlockSpec((1, lc, LANES), lambda b, p: (b, 0, p))
    return pl.pallas_call(
        _ctx_attn_kernel,
        grid=(bsz, NA_HEADS // 2),
        in_specs=[spec, spec, spec],
        out_specs=spec,
        out_shape=jax.ShapeDtypeStruct((bsz, lc, BR_W), BF16),
        compiler_params=_cp("parallel", "parallel"),
        name="ctx_attn",
    )(q_c, k_c, v_c)


def _rope_tables(length):
    n_freq = HD // 4
    t = np.arange(length)
    pos = np.stack([t // GRID_W, t % GRID_W], axis=-1).astype(np.float32)
    inv = (ROPE_BASE ** (-np.arange(n_freq, dtype=np.float32) / n_freq)).astype(np.float32)
    lane = np.arange(BR_W)
    which = (lane % HD) // (HD // 2)
    ang = pos[:, which] * inv[lane % n_freq][None, :]
    sign = np.where((lane % (HD // 2)) < n_freq, -1.0, 1.0).astype(np.float32)
    return jnp.asarray(np.cos(ang), F32), jnp.asarray(np.sin(ang) * sign[None, :], F32)


def _na_bias(rpb, length):
    rows = length // GRID_W
    wr = min(NA_WIN_R, rows)
    nblk = rows // NA_QROWS
    qr = np.arange(rows)
    rstart = np.clip(qr - wr // 2, 0, rows - wr)
    kb = np.clip(np.arange(nblk) * NA_QROWS - NA_WIN_R // 2, 0, rows - NA_KROWS)
    qrow = (np.arange(nblk)[:, None] * NA_QROWS + np.arange(NA_QROWS)[None, :])
    krow = kb[:, None] + np.arange(NA_KROWS)[None, :]
    dr = krow[:, None, :] - qrow[:, :, None] + NA_WIN_R - 1
    rvalid = (krow[:, None, :] >= rstart[qrow][:, :, None]) & (krow[:, None, :] < rstart[qrow][:, :, None] + wr)
    dr = np.where(rvalid, dr, -1)
    pats, pat_id = [], []
    for i in range(nblk):
        for j, pdr in enumerate(pats):
            if np.array_equal(pdr, dr[i]):
                pat_id.append(j)
                break
        else:
            pat_id.append(len(pats))
            pats.append(dr[i])
    pdr = np.stack(pats)
    n_dr = 2 * NA_WIN_R - 1
    oh_dr = (pdr[..., None] == np.arange(n_dr)).astype(np.float32)
    cid = np.arange(GRID_W)
    cstart = np.clip(cid - NA_WIN_C // 2, 0, GRID_W - NA_WIN_C)
    in_win = (cid[None, :] >= cstart[:, None]) & (cid[None, :] < cstart[:, None] + NA_WIN_C)
    ci = np.clip(cid[None, :] - cid[:, None], -(NA_WIN_C - 1), NA_WIN_C - 1) + NA_WIN_C - 1
    n_ci = 2 * NA_WIN_C - 1
    oh_ci = (ci[..., None] == np.arange(n_ci)).astype(np.float32)
    t1 = jnp.einsum('hdc,qkc->hdqk', rpb.astype(F32), jnp.asarray(oh_ci),
                    precision=lax.Precision.HIGHEST)
    big = jnp.einsum('brkd,hdxy->hbrxky', jnp.asarray(oh_dr), t1,
                     precision=lax.Precision.HIGHEST)
    valid = (pdr >= 0)[:, :, None, :, None] & in_win[None, None, :, None, :]
    big = jnp.where(jnp.asarray(valid)[None], big, NEG)
    big = big.reshape(rpb.shape[0], len(pats), NA_QROWS * GRID_W, NA_KROWS * GRID_W)
    return big, tuple(pat_id)


def _block_diag(blocks):
    n = len(blocks)
    r, c = blocks[0].shape
    out = jnp.zeros((n * r, n * c), blocks[0].dtype)
    for i, blk in enumerate(blocks):
        out = out.at[i * r:(i + 1) * r, i * c:(i + 1) * c].set(blk)
    return out


def _layer_params(p, i):
    w = BR_W
    w_in = p['w_in'][i]
    o1, o2, o3 = RW_IN, RW_IN + SSD_IN, RW_IN + SSD_IN + DA_IN
    d = w_in.shape[0]
    w_ssd = jnp.concatenate([w_in[:, o1:o2], jnp.zeros((d, SSD_IN_PAD - SSD_IN), w_in.dtype)], axis=1)
    head_id = np.arange(w) // HD
    ones_hd = jnp.asarray((head_id[:, None] == head_id[None, :]).astype(np.float32), BF16)
    lam_p = p['da_lambda'][i].astype(F32)
    lam_init = 0.8 - 0.6 * math.exp(-0.3 * i)
    lam = jnp.exp(jnp.sum(lam_p[0] * lam_p[1])) - jnp.exp(jnp.sum(lam_p[2] * lam_p[3])) + lam_init
    dtb = p['ssd_dt_bias'][i]
    return {
        'w_rw': w_in[:, :o1].astype(BF16), 'w_ssd': w_ssd.astype(BF16),
        'w_da': w_in[:, o2:o3].astype(BF16), 'w_na': w_in[:, o3:].astype(BF16),
        'rw_mu': p['rw_mu'][i], 'rw_w0': p['rw_w0'][i],
        'rw_wup': _block_diag([p['rw_w_up'][i, 0], p['rw_w_up'][i, 1]]).astype(BF16),
        'rw_a0': p['rw_a0'][i],
        'rw_aup': _block_diag([p['rw_a_up'][i, 0], p['rw_a_up'][i, 1]]).astype(BF16),
        'rw_gup': p['rw_g_up'][i].astype(BF16),
        'rw_kk': p['rw_k_k'][i].reshape(1, w), 'rw_ka': p['rw_k_a'][i].reshape(1, w),
        'rw_rk': p['rw_r_k'][i].reshape(1, w),
        'rw_lng': p['rw_ln_g'][i].reshape(1, w), 'rw_lnb': p['rw_ln_b'][i].reshape(1, w),
        'ones_hd': ones_hd, 'mean_hd': (ones_hd.astype(F32) / HD).astype(BF16),
        'ssd_cw': p['ssd_conv_w'][i], 'ssd_cb': p['ssd_conv_b'][i].reshape(1, SSD_XBC),
        'ssd_dtb': jnp.pad(dtb, ((0, 0), (0, LANES - SSD_HEADS))),
        'ssd_a': jnp.pad(-jnp.exp(p['ssd_a_log'][i].astype(F32)), ((0, 0), (0, LANES - SSD_HEADS))).reshape(2, 1, LANES),
        'ssd_dskip': jnp.repeat(p['ssd_d'][i], HD).reshape(1, w),
        'ssd_ng': p['ssd_norm_g'][i].reshape(1, w),
        'da_lam': jnp.broadcast_to(lam.reshape(1, 1), (1, LANES)).astype(F32),
        'da_lam_init': lam_init,
        'da_g': p['da_subln_g'][i].reshape(1, 2 * HD),
        'na_rpb': p['na_rpb'][i],
        'w_gate': p['w_gate'][i].astype(BF16),
        'gate_b': p['gate_b'][i][:, None, :],
        'w_br': p['w_br'][i].astype(BF16), 'w_out': p['w_out'][i].astype(BF16),
        'ffn_up': p['ffn_up'][i].astype(BF16), 'ffn_cw': p['ffn_conv_w'][i],
        'ffn_cb': p['ffn_conv_b'][i], 'ffn_down': p['ffn_down'][i].astype(BF16),
    }


def _project(h, lp):
    bsz, length, d = h.shape
    h2 = h.reshape(bsz * length, d)
    return tuple(_matmul(h2, lp[n], F32).reshape(bsz, length, -1)
                 for n in ('w_rw', 'w_ssd'))


def _ffn(x, h2, lp, gate):
    act = _ffn_up(h2, lp['ffn_up'], lp['ffn_cw'], lp['ffn_cb'])
    return _matmul_residual(act, lp['ffn_down'], x, gate)


def kernel(x, c, ctx, c_ctx, ada_w, ada_b, norm1_g, norm2_g, w_in, rw_mu, rw_w0, rw_w_up, rw_a0, rw_a_up, rw_g_up, rw_k_k, rw_k_a, rw_r_k, rw_ln_g, rw_ln_b, ssd_conv_w, ssd_conv_b, ssd_dt_bias, ssd_a_log, ssd_d, ssd_norm_g, da_lambda, da_subln_g, na_rpb, w_gate, gate_b, w_br, w_out, ffn_up, ffn_conv_w, ffn_conv_b, ffn_down, final_norm_g):
    p = dict(w_in=w_in, rw_mu=rw_mu, rw_w0=rw_w0, rw_w_up=rw_w_up, rw_a0=rw_a0, rw_a_up=rw_a_up,
             rw_g_up=rw_g_up, rw_k_k=rw_k_k, rw_k_a=rw_k_a, rw_r_k=rw_r_k, rw_ln_g=rw_ln_g,
             rw_ln_b=rw_ln_b, ssd_conv_w=ssd_conv_w, ssd_conv_b=ssd_conv_b, ssd_dt_bias=ssd_dt_bias,
             ssd_a_log=ssd_a_log, ssd_d=ssd_d, ssd_norm_g=ssd_norm_g, da_lambda=da_lambda,
             da_subln_g=da_subln_g, na_rpb=na_rpb, w_gate=w_gate, gate_b=gate_b, w_br=w_br,
             w_out=w_out, ffn_up=ffn_up, ffn_conv_w=ffn_conv_w, ffn_conv_b=ffn_conv_b,
             ffn_down=ffn_down)
    bsz, seq, d = x.shape
    depth = ada_w.shape[0]
    lctx = ctx.shape[1]
    mrows = -(-(bsz + 1) // 16) * 16
    cc = jnp.zeros((mrows, d), F32).at[:bsz].set(c).at[bsz].set(c_ctx)
    cos, sin = _rope_tables(seq)
    xl, xc = x, ctx
    for i in range(depth):
        last = i == depth - 1
        lp = _layer_params(p, i)
        mod = _modulation(cc, ada_w[i], ada_b[i]).reshape(mrows, 6, d)
        ml = [mod[:bsz, n][:, None, :] for n in range(6)]
        mc = [mod[bsz:bsz + 1, n][:, None, :] for n in range(6)]

        hl = _norm_mod(xl, norm1_g[i], ml[0], ml[1])
        hc = _norm_mod(xc, norm1_g[i], mc[0], mc[1])
        zl = _project(hl, lp)
        zc = _project(hc, lp)

        rp_c, lw_c = _rw_prep(zc[0], lp)
        rp_l, lw_l = _rw_prep(zl[0], lp)
        s_zero = jnp.zeros((bsz, 2, RW_HEADS // RW_GROUP, RW_GW, RW_GW), F32)
        of_c, ob_c, s_c = _rw_scan(rp_c, lw_c, s_zero)
        of_l, ob_l, _ = _rw_scan(rp_l, lw_l, s_c)
        a_l = _rw_readout(of_l, ob_l, rp_l, lp)

        xbc_c, dt_c = _ssd_prep(zc[1], lp)
        xbc_l, dt_l = _ssd_prep(zl[1], lp)
        h_zero = jnp.zeros((bsz, 2, SSD_GROUPS, SSD_STATE, SSD_GW), F32)
        yf_c, yb_c, hfin_c = _ssd_scan(xbc_c, dt_c, lp['ssd_a'], h_zero)
        yf_l, yb_l, _ = _ssd_scan(xbc_l, dt_l, lp['ssd_a'], hfin_c)
        b_l = _ssd_readout(yf_l, yb_l, xbc_l, zl[1], lp)

        q_l, k_l, v_l = _da_proj(hl, lp['w_da'], cos, sin, True)
        q_c, k_c, v_c = _da_proj(hc, lp['w_da'], cos, sin, False)
        c_l = _da_attn(q_l, [k_l, k_c], [v_l, v_c], lp['da_lam'], lp['da_g'], lp['da_lam_init'])

        na_l = _da_proj(hl, lp['w_na'], cos, sin, False)
        na_c = _da_proj(hc, lp['w_na'], cos, sin, False)
        d_l = _na_attn(na_l, na_c, *_na_bias(lp['na_rpb'], seq))

        merged = _gated_merge(hl.reshape(bsz * seq, d),
                              [t.reshape(bsz * seq, BR_W) for t in (a_l, b_l, c_l, d_l)],
                              lp['w_gate'], lp['gate_b'], lp['w_br']).reshape(bsz, seq, d)
        xl, hl2 = _matmul_residual_norm(merged, lp['w_out'], xl, ml[2], norm2_g[i], ml[3], ml[4])
        xl = _ffn(xl, hl2, lp, ml[5])

        if not last:
            a_c = _rw_readout(of_c, ob_c, rp_c, lp)
            b_c = _ssd_readout(yf_c, yb_c, xbc_c, zc[1], lp)
            c_c = _da_attn(q_c, [k_c], [v_c], lp['da_lam'], lp['da_g'], lp['da_lam_init'])
            d_c = _ctx_attn(na_c)
            merged_c = _gated_merge(hc.reshape(bsz * lctx, d),
                                    [t.reshape(bsz * lctx, BR_W) for t in (a_c, b_c, c_c, d_c)],
                                    lp['w_gate'], lp['gate_b'], lp['w_br']).reshape(bsz, lctx, d)
            xc, hc2 = _matmul_residual_norm(merged_c, lp['w_out'], xc, mc[2], norm2_g[i], mc[3], mc[4])
            xc = _ffn(xc, hc2, lp, mc[5])
    return _final_norm(xl, final_norm_g)
```

```python
import functools
import math

import numpy as np
import jax
import jax.numpy as jnp
from jax import lax
from jax.experimental import pallas as pl
from jax.experimental.pallas import tpu as pltpu

F32 = jnp.float32
BF16 = jnp.bfloat16

GRID_W = 64
NORM_EPS = 1e-6
HD = 64
BR_W = 512
RW_HEADS = 8
RW_RANK = 64
RW_GATE_RANK = 128
RW_GN_EPS = 64e-5
RW_IN = 3 * BR_W + 4 * RW_RANK + RW_GATE_RANK
RW_CHUNK = 64
RW_NARR = 9
RW_R, RW_V, RW_KK, RW_KD, RW_BD, RW_G, RW_BONUS = 0, 1, 2, 3, 4, 7, 8
RW_GROUP = 2
RW_GW = RW_GROUP * HD
RW_BATCH = 2
SSD_HEADS = 8
SSD_GROUPS = 2
SSD_STATE = 128
SSD_XBC = BR_W + 2 * SSD_GROUPS * SSD_STATE
SSD_IN = BR_W + SSD_XBC + 2 * SSD_HEADS
SSD_IN_PAD = BR_W + SSD_XBC + 128
SSD_CHUNK = 128
SSD_GW = SSD_HEADS // SSD_GROUPS * HD
DA_HEADS = 4
DA_HPS = 4
DA_IN = 3 * BR_W
DA_SUBLN_EPS = 1e-5
NA_HEADS = 8
NA_WIN_R = 8
NA_WIN_C = 16
NA_QROWS = 4
NA_KROWS = 12
ROPE_BASE = 10000.0
NEG = -1e30
LANES = 128
VMEM_LIMIT = 56 * 1024 * 1024


def _cp(*sem):
    return pltpu.CompilerParams(dimension_semantics=sem, vmem_limit_bytes=VMEM_LIMIT)


def _tile(n, pref, mult):
    t = min(n, pref)
    t -= t % mult
    while t >= mult:
        if n % t == 0:
            return t
        t -= mult
    return n


def _dot(a, b):
    return jnp.dot(a.astype(BF16), b.astype(BF16), preferred_element_type=F32)


def _dot_nt(a, b):
    return lax.dot_general(a.astype(BF16), b.astype(BF16), (((1,), (1,)), ((), ())),
                           preferred_element_type=F32)


def _dot_tn(a, b):
    return lax.dot_general(a.astype(BF16), b.astype(BF16), (((0,), (0,)), ((), ())),
                           preferred_element_type=F32)


def _split3(x):
    hi = x.astype(BF16)
    r1 = x - hi.astype(F32)
    mid = r1.astype(BF16)
    lo = (r1 - mid.astype(F32)).astype(BF16)
    return hi, mid, lo


def _dot01(m01, x):
    hi, mid, lo = _split3(x)
    return (jnp.dot(m01, hi, preferred_element_type=F32)
            + jnp.dot(m01, mid, preferred_element_type=F32)
            + jnp.dot(m01, lo, preferred_element_type=F32))


def _x_dot01(x, m01):
    hi, mid, lo = _split3(x)
    return (jnp.dot(hi, m01, preferred_element_type=F32)
            + jnp.dot(mid, m01, preferred_element_type=F32)
            + jnp.dot(lo, m01, preferred_element_type=F32))


def _sigmoid(x):
    return 1.0 / (1.0 + jnp.exp(-x))


def _silu(x):
    return x * _sigmoid(x)


def _softplus(x):
    return jnp.maximum(x, 0.0) + jnp.log(1.0 + jnp.exp(-jnp.abs(x)))


def _mod_kernel(c_ref, w_ref, b_ref, o_ref):
    a = _silu(c_ref[...])
    o_ref[...] = _dot(a, w_ref[...]) + b_ref[...]


def _modulation(cc, w, b):
    m, d = cc.shape
    n = w.shape[1]
    tn = _tile(n, 1024, LANES)
    return pl.pallas_call(
        _mod_kernel,
        grid=(n // tn,),
        in_specs=[pl.BlockSpec((m, d), lambda j: (0, 0)),
                  pl.BlockSpec((d, tn), lambda j: (0, j)),
                  pl.BlockSpec((1, tn), lambda j: (0, j))],
        out_specs=pl.BlockSpec((m, tn), lambda j: (0, j)),
        out_shape=jax.ShapeDtypeStruct((m, n), F32),
        compiler_params=_cp("parallel"),
        name="adaln_mod",
    )(cc, w, b.reshape(1, n))


def _norm_mod_kernel(x_ref, g_ref, sh_ref, sc_ref, o_ref):
    x = x_ref[0]
    y = x * lax.rsqrt(jnp.mean(x * x, axis=-1, keepdims=True) + NORM_EPS) * g_ref[...]
    o_ref[0] = (y * (1.0 + sc_ref[0]) + sh_ref[0]).astype(o_ref.dtype)


def _norm_mod(x, g, shift, scale):
    bsz, length, d = x.shape
    tr = _tile(length, 256, 16)
    bm = shift.shape[0]
    mod_map = (lambda b, j: (b, 0, 0)) if bm == bsz else (lambda b, j: (0, 0, 0))
    return pl.pallas_call(
        _norm_mod_kernel,
        grid=(bsz, length // tr),
        in_specs=[pl.BlockSpec((1, tr, d), lambda b, j: (b, j, 0)),
                  pl.BlockSpec((1, d), lambda b, j: (0, 0)),
                  pl.BlockSpec((1, 1, d), mod_map),
                  pl.BlockSpec((1, 1, d), mod_map)],
        out_specs=pl.BlockSpec((1, tr, d), lambda b, j: (b, j, 0)),
        out_shape=jax.ShapeDtypeStruct((bsz, length, d), BF16),
        compiler_params=_cp("parallel", "parallel"),
        name="norm_mod",
    )(x, g.reshape(1, d), shift, scale)


def _final_norm_kernel(x_ref, g_ref, o_ref):
    x = x_ref[0]
    o_ref[0] = x * lax.rsqrt(jnp.mean(x * x, axis=-1, keepdims=True) + NORM_EPS) * g_ref[...]


def _final_norm(x, g):
    bsz, length, d = x.shape
    tr = _tile(length, 256, 8)
    return pl.pallas_call(
        _final_norm_kernel,
        grid=(bsz, length // tr),
        in_specs=[pl.BlockSpec((1, tr, d), lambda b, j: (b, j, 0)),
                  pl.BlockSpec((1, d), lambda b, j: (0, 0))],
        out_specs=pl.BlockSpec((1, tr, d), lambda b, j: (b, j, 0)),
        out_shape=jax.ShapeDtypeStruct((bsz, length, d), F32),
        compiler_params=_cp("parallel", "parallel"),
        name="final_norm",
    )(x, g.reshape(1, d))


def _mm_kernel(a_ref, w_ref, o_ref):
    o_ref[...] = jnp.dot(a_ref[...], w_ref[...], preferred_element_type=F32).astype(o_ref.dtype)


def _matmul(a, w, out_dtype, tm_pref=512, tn_pref=2048):
    m, k = a.shape
    n = w.shape[1]
    tm = _tile(m, tm_pref, 16)
    tn = _tile(n, tn_pref, LANES)
    return pl.pallas_call(
        _mm_kernel,
        grid=(m // tm, n // tn),
        in_specs=[pl.BlockSpec((tm, k), lambda i, j: (i, 0)),
                  pl.BlockSpec((k, tn), lambda i, j: (0, j))],
        out_specs=pl.BlockSpec((tm, tn), lambda i, j: (i, j)),
        out_shape=jax.ShapeDtypeStruct((m, n), out_dtype),
        compiler_params=_cp("parallel", "parallel"),
        name="matmul",
    )(a, w)


def _mm_res_kernel(a_ref, w_ref, x_ref, g_ref, o_ref):
    y = jnp.dot(a_ref[0], w_ref[...], preferred_element_type=F32)
    o_ref[0] = x_ref[0] + g_ref[0] * y


def _matmul_residual(a, w, x, gate):
    bsz, length, k = a.shape
    n = w.shape[1]
    tm = _tile(length, 1024, 16)
    tn = _tile(n, 512, LANES)
    bm = gate.shape[0]
    gmap = (lambda b, i, j: (b, 0, j)) if bm == bsz else (lambda b, i, j: (0, 0, j))
    return pl.pallas_call(
        _mm_res_kernel,
        grid=(bsz, length // tm, n // tn),
        in_specs=[pl.BlockSpec((1, tm, k), lambda b, i, j: (b, i, 0)),
                  pl.BlockSpec((k, tn), lambda b, i, j: (0, j)),
                  pl.BlockSpec((1, tm, tn), lambda b, i, j: (b, i, j)),
                  pl.BlockSpec((1, 1, tn), gmap)],
        out_specs=pl.BlockSpec((1, tm, tn), lambda b, i, j: (b, i, j)),
        out_shape=jax.ShapeDtypeStruct((bsz, length, n), F32),
        compiler_params=_cp("parallel", "parallel", "parallel"),
        name="matmul_residual",
    )(a, w, x, gate)


def _mm_res_norm_kernel(a_ref, w_ref, x_ref, g_ref, ng_ref, sh_ref, sc_ref, xo_ref, h_ref):
    xn = x_ref[0] + g_ref[0] * jnp.dot(a_ref[0], w_ref[...], preferred_element_type=F32)
    xo_ref[0] = xn
    y = xn * lax.rsqrt(jnp.mean(xn * xn, axis=-1, keepdims=True) + NORM_EPS) * ng_ref[...]
    h_ref[0] = (y * (1.0 + sc_ref[0]) + sh_ref[0]).astype(h_ref.dtype)


def _matmul_residual_norm(a, w, x, gate, norm_g, shift, scale):
    bsz, length, k = a.shape
    n = w.shape[1]
    tm = _tile(length, 512, 16)
    bm = gate.shape[0]
    mmap = (lambda b, i: (b, 0, 0)) if bm == bsz else (lambda b, i: (0, 0, 0))
    row = pl.BlockSpec((1, tm, n), lambda b, i: (b, i, 0))
    mod = pl.BlockSpec((1, 1, n), mmap)
    return pl.pallas_call(
        _mm_res_norm_kernel,
        grid=(bsz, length // tm),
        in_specs=[pl.BlockSpec((1, tm, k), lambda b, i: (b, i, 0)),
                  pl.BlockSpec((k, n), lambda b, i: (0, 0)),
                  row, mod, pl.BlockSpec((1, n), lambda b, i: (0, 0)), mod, mod],
        out_specs=[row, row],
        out_shape=[jax.ShapeDtypeStruct((bsz, length, n), F32),
                   jax.ShapeDtypeStruct((bsz, length, n), BF16)],
        compiler_params=_cp("parallel", "parallel"),
        name="outproj_norm",
    )(a, w, x, gate, norm_g.reshape(1, n), shift, scale)


def _merge_kernel(h_ref, o0_ref, o1_ref, o2_ref, o3_ref, wg_ref, gb_ref, wbr_ref, out_ref):
    h = h_ref[...]
    acc = None
    for n, o_ref in enumerate((o0_ref, o1_ref, o2_ref, o3_ref)):
        gate = _sigmoid(jnp.dot(h, wg_ref[n], preferred_element_type=F32) + gb_ref[n])
        term = gate * jnp.dot(o_ref[...], wbr_ref[n], preferred_element_type=F32)
        acc = term if acc is None else acc + term
    out_ref[...] = acc.astype(out_ref.dtype)


def _gated_merge(h, branches, wg, gb, wbr):
    m, d = h.shape
    nb = len(branches)
    tm = _tile(m, 1024, 16)
    tn = _tile(d, 512, LANES)
    bspec = pl.BlockSpec((tm, BR_W), lambda i, j: (i, 0))
    return pl.pallas_call(
        _merge_kernel,
        grid=(m // tm, d // tn),
        in_specs=[pl.BlockSpec((tm, d), lambda i, j: (i, 0)), bspec, bspec, bspec, bspec,
                  pl.BlockSpec((nb, d, tn), lambda i, j: (0, 0, j)),
                  pl.BlockSpec((nb, 1, tn), lambda i, j: (0, 0, j)),
                  pl.BlockSpec((nb, BR_W, tn), lambda i, j: (0, 0, j))],
        out_specs=pl.BlockSpec((tm, tn), lambda i, j: (i, j)),
        out_shape=jax.ShapeDtypeStruct((m, d), BF16),
        compiler_params=_cp("parallel", "parallel"),
        name="gated_merge",
    )(h, *branches, wg, gb, wbr)


def _conv3(x, prev_row, next_row, w_ref, b_ref):
    rows = x.shape[0]
    rid = lax.broadcasted_iota(jnp.int32, x.shape, 0)
    prev = jnp.where(rid == 0, prev_row, pltpu.roll(x, 1, 0))
    nxt = jnp.where(rid == rows - 1, next_row, pltpu.roll(x, rows - 1, 0))
    return b_ref[...] + prev * w_ref[0:1, :] + x * w_ref[1:2, :] + nxt * w_ref[2:3, :]


def _ffn_up_kernel(*refs, nt, halo):
    if nt == 1:
        a_ref = refs[0]
        refs = refs[1:]
        halo = 0
        a_ext = a_ref[0]
    else:
        a_ref, ap_ref, an_ref = refs[0:3]
        refs = refs[3:]
        a_ext = jnp.concatenate([ap_ref[0], a_ref[0], an_ref[0]], axis=0)
    wg_ref, wv_ref, cwg_ref, cbg_ref, cwv_ref, cbv_ref, o_ref = refs
    i = pl.program_id(1)
    tm = a_ref.shape[1]
    n_ext = tm + 2 * halo
    rid = lax.broadcasted_iota(jnp.int32, (tm, 1), 0)
    no_prev = jnp.logical_and(i == 0, rid == 0)
    no_next = jnp.logical_and(i == nt - 1, rid == tm - 1)

    def branch(w_ref, cw_ref, cb_ref):
        u = jnp.dot(a_ext, w_ref[...], preferred_element_type=F32)
        um1 = jnp.where(no_prev, 0.0, pltpu.roll(u, 1, 0)[halo:halo + tm])
        up1 = jnp.where(no_next, 0.0, pltpu.roll(u, n_ext - 1, 0)[halo:halo + tm])
        return (cb_ref[...] + um1 * cw_ref[0:1, :] + u[halo:halo + tm] * cw_ref[1:2, :]
                + up1 * cw_ref[2:3, :])

    gate = branch(wg_ref, cwg_ref, cbg_ref)
    val = branch(wv_ref, cwv_ref, cbv_ref)
    o_ref[0] = (_silu(gate) * val).astype(o_ref.dtype)


def _ffn_up(h, w_up, conv_w, conv_b):
    bsz, length, d = h.shape
    f2 = w_up.shape[1]
    f = f2 // 2
    halo = 16
    tm = _tile(length, 1024, halo)
    tc = _tile(f, 512, LANES)
    nt = length // tm
    nc = f // tc
    nh = length // halo
    rb = tm // halo

    def wspec(rows, off):
        return pl.BlockSpec((rows, tc), lambda b, i, c: (0, c + off))

    a_specs = [pl.BlockSpec((1, tm, d), lambda b, i, c: (b, i, 0))]
    if nt > 1:
        a_specs += [
            pl.BlockSpec((1, halo, d), lambda b, i, c: (b, jnp.maximum(i * rb - 1, 0), 0)),
            pl.BlockSpec((1, halo, d), lambda b, i, c: (b, jnp.minimum((i + 1) * rb, nh - 1), 0))]
    cb = conv_b.reshape(1, f2)
    return pl.pallas_call(
        functools.partial(_ffn_up_kernel, nt=nt, halo=halo),
        grid=(bsz, nt, nc),
        in_specs=a_specs + [wspec(d, 0), wspec(d, nc), wspec(3, 0), wspec(1, 0), wspec(3, nc),
                            wspec(1, nc)],
        out_specs=pl.BlockSpec((1, tm, tc), lambda b, i, c: (b, i, c)),
        out_shape=jax.ShapeDtypeStruct((bsz, length, f), BF16),
        compiler_params=_cp("parallel", "parallel", "parallel"),
        name="ffn_up",
    )(*([h] * len(a_specs)), w_up, w_up, conv_w, cb, conv_w, cb)


def _rw_prep_kernel(z_ref, zp_ref, zn_ref, mu_ref, w0_ref, wup_ref, a0_ref, aup_ref, gup_ref,
                    kk_ref, ka_ref, rk_ref, ones_ref, o_ref, lw_out_ref, *, nt):
    j = pl.program_id(1)
    p = z_ref[0]
    rows = p.shape[0]
    rid = lax.broadcasted_iota(jnp.int32, p.shape, 0)
    prow = jnp.where(j > 0, zp_ref[0, 7:8, :], 0.0)
    nrow = jnp.where(j < nt - 1, zn_ref[0, 0:1, :], 0.0)
    prev = jnp.where(rid == 0, prow, pltpu.roll(p, 1, 0))
    nxt = jnp.where(rid == rows - 1, nrow, pltpu.roll(p, rows - 1, 0))
    ps = p + mu_ref[0:1, :] * (prev - p) + mu_ref[1:2, :] * (nxt - p)

    w = BR_W
    r = ps[:, 0:w]
    k = ps[:, w:2 * w]
    v = ps[:, 2 * w:3 * w]
    wd = ps[:, 3 * w:3 * w + 128]
    ad = ps[:, 3 * w + 128:3 * w + 256]
    gd = ps[:, 3 * w + 256:3 * w + 384]

    wraw = _dot(jnp.tanh(wd), wup_ref[...])
    araw = _dot(ad, aup_ref[...])
    g = _dot(_sigmoid(gd), gup_ref[...])
    ones = ones_ref[...]
    kkv = k * kk_ref[...]
    ss = _x_dot01(kkv * kkv, ones)
    kkn = kkv / jnp.maximum(jnp.sqrt(ss), 1e-12)

    def put(i, x):
        o_ref[0, :, i * w:(i + 1) * w] = x.astype(o_ref.dtype)

    put(RW_R, r)
    put(RW_V, v)
    put(RW_KK, kkn)
    ksum = None
    for d in range(2):
        lw = -math.exp(-0.5) * _sigmoid(w0_ref[d:d + 1, :] + wraw[:, d * w:(d + 1) * w])
        lw_out_ref[0, :, d * w:(d + 1) * w] = lw
        a = _sigmoid(a0_ref[d:d + 1, :] + araw[:, d * w:(d + 1) * w])
        kd = k * (1.0 + (a - 1.0) * ka_ref[...])
        put(RW_KD + 2 * d, kd)
        put(RW_BD + 2 * d, kkn * a)
        ksum = kd if ksum is None else ksum + kd
    put(RW_G, g)
    put(RW_BONUS, _x_dot01(r * rk_ref[...] * ksum, ones) * v)


def _rw_prep(z, lp):
    bsz, length, fin = z.shape
    tr = _tile(length, 256, 8)
    nt = length // tr
    rb = tr // 8
    nh = length // 8
    full = lambda a: pl.BlockSpec(a.shape, lambda b, j: (0,) * a.ndim)
    params = (lp['rw_mu'], lp['rw_w0'], lp['rw_wup'], lp['rw_a0'], lp['rw_aup'], lp['rw_gup'],
              lp['rw_kk'], lp['rw_ka'], lp['rw_rk'], lp['ones_hd'])
    return pl.pallas_call(
        functools.partial(_rw_prep_kernel, nt=nt),
        grid=(bsz, nt),
        in_specs=[pl.BlockSpec((1, tr, fin), lambda b, j: (b, j, 0)),
                  pl.BlockSpec((1, 8, fin), lambda b, j: (b, jnp.maximum(j * rb - 1, 0), 0)),
                  pl.BlockSpec((1, 8, fin), lambda b, j: (b, jnp.minimum((j + 1) * rb, nh - 1), 0))]
                 + [full(a) for a in params],
        out_specs=[pl.BlockSpec((1, tr, RW_NARR * BR_W), lambda b, j: (b, j, 0)),
                   pl.BlockSpec((1, tr, 2 * BR_W), lambda b, j: (b, j, 0))],
        out_shape=[jax.ShapeDtypeStruct((bsz, length, RW_NARR * BR_W), BF16),
                   jax.ShapeDtypeStruct((bsz, length, 2 * BR_W), F32)],
        compiler_params=_cp("parallel", "parallel"),
        name="rwkv_prep",
    )(z, z, z, *params)


def _rw_chunks(dir_refs, o_refs, s_scr, nb):
    cs = RW_CHUNK
    gw = RW_GW
    ng = RW_HEADS // RW_GROUP
    ri = lax.broadcasted_iota(jnp.int32, (gw, gw), 0)
    ci = lax.broadcasted_iota(jnp.int32, (gw, gw), 1)
    head_mask = (ri >> 6) == (ci >> 6)
    tw = lax.broadcasted_iota(jnp.int32, (cs, gw), 0)
    iw = lax.broadcasted_iota(jnp.int32, (cs, gw), 1) & (cs - 1)
    eye = jnp.where(iw == tw, 1.0, 0.0)
    same = {s: (tw >> int(math.log2(s))) == (iw >> int(math.log2(s))) for s in (8, 16, 32)}
    ti = lax.broadcasted_iota(jnp.int32, (cs, cs), 0)
    ii = lax.broadcasted_iota(jnp.int32, (cs, cs), 1)

    def expand(x):
        xb = x.astype(BF16)
        return jnp.where(head_mask, jnp.concatenate([xb] * RW_GROUP, axis=0), jnp.zeros_like(xb[0:1, 0:1]))

    def mm(a, b):
        return _dot(a, expand(b))

    chains = []
    for bb in range(nb):
        for d, (r_ref, v_ref, kk_ref, lw_ref, k_ref, b_ref) in enumerate(dir_refs):
            sgn = 1 - 2 * d
            rel = (iw - tw) * sgn
            tri = jnp.where((ii - ti) * sgn <= 0, 1.0, 0.0).astype(BF16)
            lw = lw_ref[bb]
            cum = _dot01(tri, lw)
            tot = jnp.sum(lw, axis=0, keepdims=True)
            kd, bd = k_ref[bb].astype(F32), b_ref[bb].astype(F32)
            qt = kk_ref[bb].astype(F32) * jnp.exp(cum - lw)
            rt = r_ref[bb].astype(F32) * jnp.exp(cum)
            e_neg = jnp.exp(-cum)
            e_end = jnp.exp(tot - cum)
            kt = kd * e_neg
            bt = bd * e_neg
            kh = kd * e_end
            bh = bd * e_end
            p_tot = jnp.exp(tot)
            v = v_ref[bb].astype(F32)
            for g in range(ng):
                sl = slice(g * gw, (g + 1) * gw)
                chains.append(dict(bb=bb, d=d, g=g, sl=sl, strict=rel < 0, incl=rel <= 0,
                                   q=qt[:, sl], r=rt[:, sl], v=v[:, sl], kt=kt[:, sl],
                                   bt=bt[:, sl], kh=kh[:, sl], bh=bh[:, sl], p_tot=p_tot[:, sl],
                                   s0=s_scr[bb, d, g]))

    for c in chains:
        c['kte'], c['bte'], c['ve'] = expand(c['kt']), expand(c['bt']), expand(c['v'])
        c['s0b'] = c['s0'].astype(BF16)
    for c in chains:
        c['lmat'] = jnp.where(c['strict'], _dot_nt(c['q'], c['bte']), 0.0)
        c['ld'] = jnp.where(same[8], c['lmat'], 0.0)
    for c in chains:
        c['ld2'] = mm(c['ld'], c['ld'])
    for c in chains:
        c['a_qk'] = jnp.where(c['strict'], _dot_nt(c['q'], c['kte']), 0.0)
    for c in chains:
        c['ld4'] = mm(c['ld2'], c['ld2'])
    for c in chains:
        c['x'] = mm(eye - c['ld'], eye + c['ld2'])
    for c in chains:
        c['a_rk'] = jnp.where(c['incl'], _dot_nt(c['r'], c['kte']), 0.0)
    for c in chains:
        c['x'] = mm(c['x'], eye + c['ld4'])
    for c in chains:
        c['a_rb'] = jnp.where(c['incl'], _dot_nt(c['r'], c['bte']), 0.0)
    for c in chains:
        c['rhs'] = _dot_nt(c['q'], c['s0b']) + _dot(c['a_qk'], c['ve'])
    for s in (8, 16, 32):
        off_mask = jnp.logical_not(same[s])
        if 2 * s < cs:
            off_mask = jnp.logical_and(same[2 * s], off_mask)
        for c in chains:
            c['t'] = mm(c['x'], jnp.where(off_mask, c['lmat'], 0.0))
        if s == 8:
            for c in chains:
                c['o'] = _dot_nt(c['r'], c['s0b']) + _dot(c['a_rk'], c['ve'])
        for c in chains:
            c['x'] = c['x'] - mm(c['t'], c['x'])
    for c in chains:
        c['sa'] = mm(c['x'], c['rhs'])
    for c in chains:
        o_refs[c['d']][c['bb'], :, c['sl']] = c['o'] - mm(c['a_rb'], c['sa'])
    for c in chains:
        upd = _dot_tn(jnp.concatenate([c['v'], c['sa']], axis=0),
                      jnp.concatenate([c['kh'], -c['bh']], axis=0))
        s_scr[c['bb'], c['d'], c['g']] = c['s0'] * c['p_tot'] + jnp.where(head_mask, upd, 0.0)


def _rw_scan_kernel(*refs, nc, nb):
    fwd, bwd = refs[0:6], refs[6:12]
    s0_ref, of_ref, ob_ref, sfin_ref, s_scr = refs[12:]
    c = pl.program_id(1)

    @pl.when(c == 0)
    def _():
        s_scr[...] = s0_ref[...]

    _rw_chunks((fwd, bwd), (of_ref, ob_ref), s_scr, nb)

    @pl.when(c == nc - 1)
    def _():
        sfin_ref[...] = s_scr[...]


def _rw_scan(rwp, lwp, s_init):
    bsz, length, _ = rwp.shape
    cs = RW_CHUNK
    nc = length // cs
    ng = RW_HEADS // RW_GROUP
    nb = RW_BATCH if bsz % RW_BATCH == 0 else 1

    def arr(d, i):
        if d == 0:
            return pl.BlockSpec((nb, cs, BR_W), lambda b, c: (b, c, i))
        return pl.BlockSpec((nb, cs, BR_W), lambda b, c: (b, nc - 1 - c, i))

    def specs(d):
        return [arr(d, RW_R), arr(d, RW_V), arr(d, RW_KK), arr(d, d),
                arr(d, RW_KD + 2 * d), arr(d, RW_BD + 2 * d)]

    def operands():
        return [rwp, rwp, rwp, lwp, rwp, rwp]

    sspec = pl.BlockSpec((nb, 2, ng, RW_GW, RW_GW), lambda b, c: (b, 0, 0, 0, 0))
    oshape = jax.ShapeDtypeStruct((bsz, length, BR_W), F32)
    return pl.pallas_call(
        functools.partial(_rw_scan_kernel, nc=nc, nb=nb),
        grid=(bsz // nb, nc),
        in_specs=specs(0) + specs(1) + [sspec],
        out_specs=[pl.BlockSpec((nb, cs, BR_W), lambda b, c: (b, c, 0)),
                   pl.BlockSpec((nb, cs, BR_W), lambda b, c: (b, nc - 1 - c, 0)), sspec],
        out_shape=[oshape, oshape, jax.ShapeDtypeStruct((bsz, 2, ng, RW_GW, RW_GW), F32)],
        scratch_shapes=[pltpu.VMEM((nb, 2, ng, RW_GW, RW_GW), F32)],
        compiler_params=_cp("parallel", "arbitrary"),
        name="rwkv_scan",
    )(*operands(), *operands(), s_init)


def _rw_readout_kernel(of_ref, ob_ref, g_ref, bonus_ref, lng_ref, lnb_ref, mean_ref, out_ref):
    o = of_ref[0] + ob_ref[0]
    mean_m = mean_ref[...]
    mu = _x_dot01(o, mean_m)
    dlt = o - mu
    var = _x_dot01(dlt * dlt, mean_m)
    on = dlt * lax.rsqrt(var + RW_GN_EPS) * lng_ref[...] + lnb_ref[...]
    out_ref[0] = ((on + bonus_ref[0].astype(F32)) * g_ref[0].astype(F32)).astype(out_ref.dtype)


def _rw_readout(o_f, o_b, rwp, lp):
    bsz, length, w = o_f.shape
    tr = _tile(length, 256, 16)
    ospec = pl.BlockSpec((1, tr, w), lambda b, j: (b, j, 0))
    return pl.pallas_call(
        _rw_readout_kernel,
        grid=(bsz, length // tr),
        in_specs=[ospec, ospec,
                  pl.BlockSpec((1, tr, w), lambda b, j: (b, j, RW_G)),
                  pl.BlockSpec((1, tr, w), lambda b, j: (b, j, RW_BONUS)),
                  pl.BlockSpec((1, w), lambda b, j: (0, 0)),
                  pl.BlockSpec((1, w), lambda b, j: (0, 0)),
                  pl.BlockSpec((w, w), lambda b, j: (0, 0))],
        out_specs=ospec,
        out_shape=jax.ShapeDtypeStruct((bsz, length, w), BF16),
        compiler_params=_cp("parallel", "parallel"),
        name="rwkv_readout",
    )(o_f, o_b, rwp, rwp, lp['rw_lng'], lp['rw_lnb'], lp['mean_hd'])


def _ssd_prep_kernel(z_ref, zp_ref, zn_ref, cw_ref, cb_ref, dtb_ref, xbc_ref, dt_ref, *, nt):
    j = pl.program_id(1)
    x = z_ref[0, :, BR_W:BR_W + SSD_XBC]
    prow = jnp.where(j > 0, zp_ref[0, 7:8, BR_W:BR_W + SSD_XBC], 0.0)
    nrow = jnp.where(j < nt - 1, zn_ref[0, 0:1, BR_W:BR_W + SSD_XBC], 0.0)
    xbc_ref[0] = _silu(_conv3(x, prow, nrow, cw_ref, cb_ref))
    dt_raw = z_ref[0, :, BR_W + SSD_XBC:BR_W + SSD_XBC + LANES]
    dt_ref[0, 0] = _softplus(dt_raw + dtb_ref[0:1, :])
    dt_ref[0, 1] = _softplus(pltpu.roll(dt_raw, LANES - SSD_HEADS, 1) + dtb_ref[1:2, :])


def _ssd_prep(z, lp):
    bsz, length, fin = z.shape
    tr = _tile(length, 256, 8)
    nt = length // tr
    rb = tr // 8
    nh = length // 8
    return pl.pallas_call(
        functools.partial(_ssd_prep_kernel, nt=nt),
        grid=(bsz, nt),
        in_specs=[pl.BlockSpec((1, tr, fin), lambda b, j: (b, j, 0)),
                  pl.BlockSpec((1, 8, fin), lambda b, j: (b, jnp.maximum(j * rb - 1, 0), 0)),
                  pl.BlockSpec((1, 8, fin), lambda b, j: (b, jnp.minimum((j + 1) * rb, nh - 1), 0)),
                  pl.BlockSpec((3, SSD_XBC), lambda b, j: (0, 0)),
                  pl.BlockSpec((1, SSD_XBC), lambda b, j: (0, 0)),
                  pl.BlockSpec((2, LANES), lambda b, j: (0, 0))],
        out_specs=[pl.BlockSpec((1, tr, SSD_XBC), lambda b, j: (b, j, 0)),
                   pl.BlockSpec((1, 2, tr, LANES), lambda b, j: (b, 0, j, 0))],
        out_shape=[jax.ShapeDtypeStruct((bsz, length, SSD_XBC), F32),
                   jax.ShapeDtypeStruct((bsz, 2, length, LANES), F32)],
        compiler_params=_cp("parallel", "parallel"),
        name="ssd_prep",
    )(z, z, z, lp['ssd_cw'], lp['ssd_cb'], lp['ssd_dtb'])


def _ssd_chunks(dir_refs, a_ref, y_refs, h_scr, nb):
    q = SSD_CHUNK
    n = SSD_STATE
    rep = SSD_HEADS // SSD_GROUPS
    gw = SSD_GW
    ti = lax.broadcasted_iota(jnp.int32, (q, q), 0)
    ii = lax.broadcasted_iota(jnp.int32, (q, q), 1)
    ri = lax.broadcasted_iota(jnp.int32, (rep * q, gw), 0)
    ci = lax.broadcasted_iota(jnp.int32, (rep * q, gw), 1)
    head_mask = (ri >> int(math.log2(q))) == (ci >> int(math.log2(HD)))

    def per_head(x, h0):
        return jnp.concatenate([jnp.broadcast_to(x[:, h0 + h:h0 + h + 1], (x.shape[0], HD))
                                for h in range(rep)], axis=1)

    work = []
    for bb in range(nb):
        for d, (xbc_ref, dt_ref) in enumerate(dir_refs):
            incl = (ii - ti) * (1 - 2 * d) <= 0
            tri = jnp.where(incl, 1.0, 0.0).astype(BF16)
            dt = dt_ref[bb, 0]
            dta = dt * a_ref[d]
            acs = _dot01(tri, dta)
            acs_t = acs.T
            tot = jnp.sum(dta, axis=0, keepdims=True)
            e_acs = jnp.exp(acs)
            e_end = jnp.exp(tot - acs)
            e_tot = jnp.exp(tot)
            xbc = xbc_ref[bb]
            for g in range(SSD_GROUPS):
                h0 = g * rep
                bg = xbc[:, BR_W + g * n:BR_W + (g + 1) * n]
                cg = xbc[:, BR_W + SSD_GROUPS * n + g * n:BR_W + SSD_GROUPS * n + (g + 1) * n]
                xdt = xbc[:, g * gw:(g + 1) * gw] * per_head(dt, h0)
                dec = jnp.concatenate(
                    [jnp.exp(jnp.where(incl, acs[:, h0 + h:h0 + h + 1] - acs_t[h0 + h:h0 + h + 1, :], NEG))
                     for h in range(rep)], axis=1)
                work.append(dict(bb=bb, d=d, g=g, bg=bg, cg=cg, xdt=xdt, dec=dec,
                                 ht=h_scr[bb, d, g], e_acs=per_head(e_acs, h0),
                                 e_end=per_head(e_end, h0), e_tot=per_head(e_tot, h0)))
    for w in work:
        w['cb'] = _dot_nt(w['cg'], w['bg'])
    for w in work:
        w['yoff'] = _dot(w['cg'], w['ht'])
    for w in work:
        w['st'] = _dot_tn(w['bg'], w['xdt'] * w['e_end'])
    for w in work:
        xe = jnp.where(head_mask, jnp.concatenate([w['xdt']] * rep, axis=0), 0.0)
        scores = jnp.concatenate([w['cb']] * rep, axis=1) * w['dec']
        y = _dot(scores, xe) + w['e_acs'] * w['yoff']
        y_refs[w['d']][w['bb'], :, w['g'] * gw:(w['g'] + 1) * gw] = y
        h_scr[w['bb'], w['d'], w['g']] = w['ht'] * w['e_tot'] + w['st']


def _ssd_scan_kernel(xf_ref, dtf_ref, xb_ref, dtb_ref, a_ref, h0_ref, yf_ref, yb_ref, hfin_ref,
                     h_scr, *, nc, nb):
    c = pl.program_id(1)

    @pl.when(c == 0)
    def _():
        h_scr[...] = h0_ref[...]

    _ssd_chunks(((xf_ref, dtf_ref), (xb_ref, dtb_ref)), a_ref, (yf_ref, yb_ref), h_scr, nb)

    @pl.when(c == nc - 1)
    def _():
        hfin_ref[...] = h_scr[...]


def _ssd_scan(xbc, dt, a_neg, h_init):
    bsz, length, _ = xbc.shape
    q = SSD_CHUNK
    nc = length // q
    nb = RW_BATCH if bsz % RW_BATCH == 0 else 1
    hspec = pl.BlockSpec((nb, 2, SSD_GROUPS, SSD_STATE, SSD_GW), lambda b, c: (b, 0, 0, 0, 0))
    oshape = jax.ShapeDtypeStruct((bsz, length, BR_W), F32)
    return pl.pallas_call(
        functools.partial(_ssd_scan_kernel, nc=nc, nb=nb),
        grid=(bsz // nb, nc),
        in_specs=[pl.BlockSpec((nb, q, SSD_XBC), lambda b, c: (b, c, 0)),
                  pl.BlockSpec((nb, 1, q, LANES), lambda b, c: (b, 0, c, 0)),
                  pl.BlockSpec((nb, q, SSD_XBC), lambda b, c: (b, nc - 1 - c, 0)),
                  pl.BlockSpec((nb, 1, q, LANES), lambda b, c: (b, 1, nc - 1 - c, 0)),
                  pl.BlockSpec((2, 1, LANES), lambda b, c: (0, 0, 0)),
                  hspec],
        out_specs=[pl.BlockSpec((nb, q, BR_W), lambda b, c: (b, c, 0)),
                   pl.BlockSpec((nb, q, BR_W), lambda b, c: (b, nc - 1 - c, 0)), hspec],
        out_shape=[oshape, oshape,
                   jax.ShapeDtypeStruct((bsz, 2, SSD_GROUPS, SSD_STATE, SSD_GW), F32)],
        scratch_shapes=[pltpu.VMEM((nb, 2, SSD_GROUPS, SSD_STATE, SSD_GW), F32)],
        compiler_params=_cp("parallel", "arbitrary"),
        name="ssd_scan",
    )(xbc, dt, xbc, dt, a_neg, h_init)


def _ssd_readout_kernel(yf_ref, yb_ref, xs_ref, z_ref, dsk_ref, ng_ref, out_ref):
    y = yf_ref[0] + yb_ref[0] + dsk_ref[...] * xs_ref[0]
    y = y * _silu(z_ref[0])
    y = y * lax.rsqrt(jnp.mean(y * y, axis=-1, keepdims=True) + NORM_EPS) * ng_ref[...]
    out_ref[0] = y.astype(out_ref.dtype)


def _ssd_readout(y_f, y_b, xbc, z, lp):
    bsz, length, w = y_f.shape
    tr = _tile(length, 256, 16)
    ospec = pl.BlockSpec((1, tr, w), lambda b, j: (b, j, 0))
    return pl.pallas_call(
        _ssd_readout_kernel,
        grid=(bsz, length // tr),
        in_specs=[ospec, ospec, ospec, ospec,
                  pl.BlockSpec((1, w), lambda b, j: (0, 0)),
                  pl.BlockSpec((1, w), lambda b, j: (0, 0))],
        out_specs=ospec,
        out_shape=jax.ShapeDtypeStruct((bsz, length, w), BF16),
        compiler_params=_cp("parallel", "parallel"),
        name="ssd_readout",
    )(y_f, y_b, xbc, z, lp['ssd_dskip'], lp['ssd_ng'])


def _qkv_proj_kernel(h_ref, w_ref, cos_ref, sin_ref, q_ref, k_ref, v_ref, *, rope):
    w = BR_W
    z = jnp.dot(h_ref[0], w_ref[...], preferred_element_type=F32)

    def rot(x):
        if not rope:
            return x
        lane = lax.broadcasted_iota(jnp.int32, (1, LANES), 1)
        first = (lane & 31) < 16
        outs = []
        for cb in range(w // LANES):
            xb = x[:, cb * LANES:(cb + 1) * LANES]
            sw = jnp.where(first, pltpu.roll(xb, LANES - 16, 1), pltpu.roll(xb, 16, 1))
            outs.append(xb * cos_ref[:, cb * LANES:(cb + 1) * LANES]
                        + sw * sin_ref[:, cb * LANES:(cb + 1) * LANES])
        return jnp.concatenate(outs, axis=1)

    q_ref[0] = (rot(z[:, 0:w]) * (HD ** -0.5)).astype(q_ref.dtype)
    k_ref[0] = rot(z[:, w:2 * w]).astype(k_ref.dtype)
    v_ref[0] = z[:, 2 * w:3 * w].astype(v_ref.dtype)


def _qkv_proj(h, w_da, cos, sin, rope):
    bsz, length, d = h.shape
    tr = _tile(length, 512, 16)
    w = BR_W
    ospec = pl.BlockSpec((1, tr, w), lambda b, j: (b, j, 0))
    oshape = jax.ShapeDtypeStruct((bsz, length, w), BF16)
    return pl.pallas_call(
        functools.partial(_qkv_proj_kernel, rope=rope),
        grid=(bsz, length // tr),
        in_specs=[pl.BlockSpec((1, tr, d), lambda b, j: (b, j, 0)),
                  pl.BlockSpec((d, 3 * w), lambda b, j: (0, 0)),
                  pl.BlockSpec((tr, w), lambda b, j: (j, 0)),
                  pl.BlockSpec((tr, w), lambda b, j: (j, 0))],
        out_specs=[ospec, ospec, ospec],
        out_shape=[oshape, oshape, oshape],
        compiler_params=_cp("parallel", "parallel"),
        name="qkv_proj",
    )(h, w_da, cos, sin)


def _da_attn_kernel(*refs, nsrc, lam_init):
    lam_ref, q_ref = refs[0], refs[1]
    kv = refs[2:2 + 2 * nsrc]
    g_ref, o_ref = refs[2 + 2 * nsrc], refs[3 + 2 * nsrc]
    lane = lax.broadcasted_iota(jnp.int32, (1, LANES), 1)
    lam = lam_ref[0:1, 0:1]
    heads = [slice(hh * LANES, (hh + 1) * LANES) for hh in range(DA_HPS)]
    scores = {}
    for hh, hs in enumerate(heads):
        q = q_ref[0, :, hs]
        zero = jnp.zeros_like(q)
        for m in range(2):
            qm = jnp.where((lane < HD) if m == 0 else (lane >= HD), q, zero)
            scores[hh, m] = [_dot_nt(qm, kv[2 * i][0, :, hs]) for i in range(nsrc)]
    for hh, hs in enumerate(heads):
        probs = []
        for m in range(2):
            ss = scores[hh, m]
            mx = functools.reduce(jnp.maximum, [jnp.max(s, axis=-1, keepdims=True) for s in ss])
            es = [jnp.exp(s - mx) for s in ss]
            den = functools.reduce(lambda a, b: a + b,
                                   [jnp.sum(e, axis=-1, keepdims=True) for e in es])
            inv = 1.0 / den
            probs.append([e * inv for e in es])
        o = None
        for i in range(nsrc):
            a = probs[0][i] - lam * probs[1][i]
            t = _dot(a, kv[2 * i + 1][0, :, hs])
            o = t if o is None else o + t
        y = o * lax.rsqrt(jnp.mean(o * o, axis=-1, keepdims=True) + DA_SUBLN_EPS) * g_ref[...]
        o_ref[0, :, hs] = (y * (1.0 - lam_init)).astype(o_ref.dtype)


def _da_attn(q, ks, vs, lam, subln_g, lam_init):
    bsz, lq, _ = q.shape
    tq = _tile(lq, 256, 16)
    nsrc = len(ks)
    hw = DA_HPS * LANES
    in_specs = [pl.BlockSpec((1, LANES), lambda b, h, j: (0, 0)),
                pl.BlockSpec((1, tq, hw), lambda b, h, j: (b, j, h))]
    args = [lam, q]
    for k, v in zip(ks, vs):
        lk = k.shape[1]
        in_specs.append(pl.BlockSpec((1, lk, hw), lambda b, h, j: (b, 0, h)))
        in_specs.append(pl.BlockSpec((1, lk, hw), lambda b, h, j: (b, 0, h)))
        args += [k, v]
    in_specs.append(pl.BlockSpec((1, LANES), lambda b, h, j: (0, 0)))
    args.append(subln_g)
    return pl.pallas_call(
        functools.partial(_da_attn_kernel, nsrc=nsrc, lam_init=lam_init),
        grid=(bsz, DA_HEADS // DA_HPS, lq // tq),
        in_specs=in_specs,
        out_specs=pl.BlockSpec((1, tq, hw), lambda b, h, j: (b, j, h)),
        out_shape=jax.ShapeDtypeStruct((bsz, lq, BR_W), BF16),
        compiler_params=_cp("parallel", "parallel", "parallel"),
        name="diff_attn",
    )(*args)


def _na_kb(i, rows):
    return jnp.clip(i * NA_QROWS - NA_WIN_R // 2, 0, rows - NA_KROWS)


def _na_attn_kernel(q_ref, k_ref, v_ref, kc_ref, vc_ref, *rest, rows, nsub):
    bias_refs, o_ref = rest[:nsub], rest[nsub]
    i = pl.program_id(2)
    tq = NA_QROWS * GRID_W
    nk = NA_KROWS * GRID_W
    kc = kc_ref[0]
    vc = vc_ref[0]
    lane = lax.broadcasted_iota(jnp.int32, (1, LANES), 1)
    for j in range(nsub):
        start = pl.multiple_of(_na_kb(i * nsub + j, rows) * GRID_W, GRID_W)
        q = q_ref[0, j * tq:(j + 1) * tq, :]
        zero = jnp.zeros_like(q)
        kw = k_ref[0, pl.ds(start, nk), :]
        vw = v_ref[0, pl.ds(start, nk), :]
        outs = []
        for h in range(2):
            qm = jnp.where((lane < HD) if h == 0 else (lane >= HD), q, zero)
            sw = _dot_nt(qm, kw) + bias_refs[j][h, 0]
            sc = _dot_nt(qm, kc)
            mx = jnp.maximum(jnp.max(sw, axis=-1, keepdims=True), jnp.max(sc, axis=-1, keepdims=True))
            ew = jnp.exp(sw - mx)
            ec = jnp.exp(sc - mx)
            inv = 1.0 / (jnp.sum(ew, axis=-1, keepdims=True) + jnp.sum(ec, axis=-1, keepdims=True))
            outs.append((_dot(ew, vw) + _dot(ec, vc)) * inv)
        o_ref[0, j * tq:(j + 1) * tq, :] = jnp.where(lane < HD, outs[0], outs[1]).astype(o_ref.dtype)


def _na_attn(qkv_l, qkv_c, bias, pat_id):
    q_l, k_l, v_l = qkv_l
    _, k_c, v_c = qkv_c
    bsz, length, _ = q_l.shape
    lc = k_c.shape[1]
    rows = length // GRID_W
    tq = NA_QROWS * GRID_W
    nblk = length // tq
    nsub = next(n for n in (4, 2, 1) if nblk % n == 0)

    def pat(i):
        out = pat_id[0]
        for j in range(1, nblk):
            if pat_id[j] != pat_id[j - 1]:
                out = out + jnp.where(i >= j, pat_id[j] - pat_id[j - 1], 0)
        return out

    def bias_spec(j):
        return pl.BlockSpec((2, 1, tq, NA_KROWS * GRID_W),
                            lambda b, p, i: (p, pat(i * nsub + j), 0, 0))

    qspec = pl.BlockSpec((1, nsub * tq, LANES), lambda b, p, i: (b, i, p))
    whole = lambda n: pl.BlockSpec((1, n, LANES), lambda b, p, i: (b, 0, p))
    return pl.pallas_call(
        functools.partial(_na_attn_kernel, rows=rows, nsub=nsub),
        grid=(bsz, NA_HEADS // 2, nblk // nsub),
        in_specs=[qspec, whole(length), whole(length), whole(lc), whole(lc)]
                 + [bias_spec(j) for j in range(nsub)],
        out_specs=qspec,
        out_shape=jax.ShapeDtypeStruct((bsz, length, BR_W), BF16),
        compiler_params=_cp("parallel", "parallel", "parallel"),
        name="na_attn",
    )(q_l, k_l, v_l, k_c, v_c, *([bias] * nsub))


def _ctx_attn_kernel(q_ref, k_ref, v_ref, o_ref):
    q = q_ref[0]
    zero = jnp.zeros_like(q)
    k = k_ref[0]
    v = v_ref[0]
    lane = lax.broadcasted_iota(jnp.int32, (1, LANES), 1)
    outs = []
    for h in range(2):
        qm = jnp.where((lane < HD) if h == 0 else (lane >= HD), q, zero)
        s = _dot_nt(qm, k)
        e = jnp.exp(s - jnp.max(s, axis=-1, keepdims=True))
        outs.append(_dot(e, v) * (1.0 / jnp.sum(e, axis=-1, keepdims=True)))
    o_ref[0] = jnp.where(lane < HD, outs[0], outs[1]).astype(o_ref.dtype)


def _ctx_attn(qkv_c):
    q_c, k_c, v_c = qkv_c
    bsz, lc, _ = q_c.shape
    spec = pl.BlockSpec((1, lc, LANES), lambda b, p: (b, 0, p))
    return pl.pallas_call(
        _ctx_attn_kernel,
        grid=(bsz, NA_HEADS // 2),
        in_specs=[spec, spec, spec],
        out_specs=spec,
        out_shape=jax.ShapeDtypeStruct((bsz, lc, BR_W), BF16),
        compiler_params=_cp("parallel", "parallel"),
        name="ctx_attn",
    )(q_c, k_c, v_c)


def _rope_tables(length):
    n_freq = HD // 4
    t = np.arange(length)
    pos = np.stack([t // GRID_W, t % GRID_W], axis=-1).astype(np.float32)
    inv = (ROPE_BASE ** (-np.arange(n_freq, dtype=np.float32) / n_freq)).astype(np.float32)
    lane = np.arange(BR_W)
    which = (lane % HD) // (HD // 2)
    ang = pos[:, which] * inv[lane % n_freq][None, :]
    sign = np.where((lane % (HD // 2)) < n_freq, -1.0, 1.0).astype(np.float32)
    return jnp.asarray(np.cos(ang), F32), jnp.asarray(np.sin(ang) * sign[None, :], F32)


def _na_bias(rpb, length):
    rows = length // GRID_W
    wr = min(NA_WIN_R, rows)
    nblk = rows // NA_QROWS
    qr = np.arange(rows)
    rstart = np.clip(qr - wr // 2, 0, rows - wr)
    kb = np.clip(np.arange(nblk) * NA_QROWS - NA_WIN_R // 2, 0, rows - NA_KROWS)
    qrow = (np.arange(nblk)[:, None] * NA_QROWS + np.arange(NA_QROWS)[None, :])
    krow = kb[:, None] + np.arange(NA_KROWS)[None, :]
    dr = krow[:, None, :] - qrow[:, :, None] + NA_WIN_R - 1
    rvalid = (krow[:, None, :] >= rstart[qrow][:, :, None]) & (krow[:, None, :] < rstart[qrow][:, :, None] + wr)
    dr = np.where(rvalid, dr, -1)
    pats, pat_id = [], []
    for i in range(nblk):
        for j, pdr in enumerate(pats):
            if np.array_equal(pdr, dr[i]):
                pat_id.append(j)
                break
        else:
            pat_id.append(len(pats))
            pats.append(dr[i])
    pdr = np.stack(pats)
    n_dr = 2 * NA_WIN_R - 1
    oh_dr = (pdr[..., None] == np.arange(n_dr)).astype(np.float32)
    cid = np.arange(GRID_W)
    cstart = np.clip(cid - NA_WIN_C // 2, 0, GRID_W - NA_WIN_C)
    in_win = (cid[None, :] >= cstart[:, None]) & (cid[None, :] < cstart[:, None] + NA_WIN_C)
    ci = np.clip(cid[None, :] - cid[:, None], -(NA_WIN_C - 1), NA_WIN_C - 1) + NA_WIN_C - 1
    n_ci = 2 * NA_WIN_C - 1
    oh_ci = (ci[..., None] == np.arange(n_ci)).astype(np.float32)
    t1 = jnp.einsum('hdc,qkc->hdqk', rpb.astype(F32), jnp.asarray(oh_ci),
                    precision=lax.Precision.HIGHEST)
    big = jnp.einsum('brkd,hdxy->hbrxky', jnp.asarray(oh_dr), t1,
                     precision=lax.Precision.HIGHEST)
    valid = (pdr >= 0)[:, :, None, :, None] & in_win[None, None, :, None, :]
    big = jnp.where(jnp.asarray(valid)[None], big, NEG)
    big = big.reshape(rpb.shape[0], len(pats), NA_QROWS * GRID_W, NA_KROWS * GRID_W)
    return big, tuple(pat_id)


def _block_diag(blocks):
    n = len(blocks)
    r, c = blocks[0].shape
    out = jnp.zeros((n * r, n * c), blocks[0].dtype)
    for i, blk in enumerate(blocks):
        out = out.at[i * r:(i + 1) * r, i * c:(i + 1) * c].set(blk)
    return out


def _layer_params(p, i):
    w = BR_W
    w_in = p['w_in'][i]
    o1, o2, o3 = RW_IN, RW_IN + SSD_IN, RW_IN + SSD_IN + DA_IN
    d = w_in.shape[0]
    w_ssd = jnp.concatenate([w_in[:, o1:o2], jnp.zeros((d, SSD_IN_PAD - SSD_IN), w_in.dtype)], axis=1)
    head_id = np.arange(w) // HD
    ones_hd = jnp.asarray((head_id[:, None] == head_id[None, :]).astype(np.float32), BF16)
    lam_p = p['da_lambda'][i].astype(F32)
    lam_init = 0.8 - 0.6 * math.exp(-0.3 * i)
    lam = jnp.exp(jnp.sum(lam_p[0] * lam_p[1])) - jnp.exp(jnp.sum(lam_p[2] * lam_p[3])) + lam_init
    dtb = p['ssd_dt_bias'][i]
    return {
        'w_rw': w_in[:, :o1].astype(BF16), 'w_ssd': w_ssd.astype(BF16),
        'w_da': w_in[:, o2:o3].astype(BF16), 'w_na': w_in[:, o3:].astype(BF16),
        'rw_mu': p['rw_mu'][i], 'rw_w0': p['rw_w0'][i],
        'rw_wup': _block_diag([p['rw_w_up'][i, 0], p['rw_w_up'][i, 1]]).astype(BF16),
        'rw_a0': p['rw_a0'][i],
        'rw_aup': _block_diag([p['rw_a_up'][i, 0], p['rw_a_up'][i, 1]]).astype(BF16),
        'rw_gup': p['rw_g_up'][i].astype(BF16),
        'rw_kk': p['rw_k_k'][i].reshape(1, w), 'rw_ka': p['rw_k_a'][i].reshape(1, w),
        'rw_rk': p['rw_r_k'][i].reshape(1, w),
        'rw_lng': p['rw_ln_g'][i].reshape(1, w), 'rw_lnb': p['rw_ln_b'][i].reshape(1, w),
        'ones_hd': ones_hd, 'mean_hd': (ones_hd.astype(F32) / HD).astype(BF16),
        'ssd_cw': p['ssd_conv_w'][i], 'ssd_cb': p['ssd_conv_b'][i].reshape(1, SSD_XBC),
        'ssd_dtb': jnp.pad(dtb, ((0, 0), (0, LANES - SSD_HEADS))),
        'ssd_a': jnp.pad(-jnp.exp(p['ssd_a_log'][i].astype(F32)), ((0, 0), (0, LANES - SSD_HEADS))).reshape(2, 1, LANES),
        'ssd_dskip': jnp.repeat(p['ssd_d'][i], HD).reshape(1, w),
        'ssd_ng': p['ssd_norm_g'][i].reshape(1, w),
        'da_lam': jnp.broadcast_to(lam.reshape(1, 1), (1, LANES)).astype(F32),
        'da_lam_init': lam_init,
        'da_g': p['da_subln_g'][i].reshape(1, 2 * HD),
        'na_rpb': p['na_rpb'][i],
        'w_gate': p['w_gate'][i].astype(BF16),
        'gate_b': p['gate_b'][i][:, None, :],
        'w_br': p['w_br'][i].astype(BF16), 'w_out': p['w_out'][i].astype(BF16),
        'ffn_up': p['ffn_up'][i].astype(BF16), 'ffn_cw': p['ffn_conv_w'][i],
        'ffn_cb': p['ffn_conv_b'][i], 'ffn_down': p['ffn_down'][i].astype(BF16),
    }


def _project(h, lp):
    bsz, length, d = h.shape
    h2 = h.reshape(bsz * length, d)
    return tuple(_matmul(h2, lp[n], F32).reshape(bsz, length, -1)
                 for n in ('w_rw', 'w_ssd'))


def _ffn(x, h2, lp, gate):
    act = _ffn_up(h2, lp['ffn_up'], lp['ffn_cw'], lp['ffn_cb'])
    return _matmul_residual(act, lp['ffn_down'], x, gate)


def kernel(x, c, ctx, c_ctx, ada_w, ada_b, norm1_g, norm2_g, w_in, rw_mu, rw_w0, rw_w_up, rw_a0, rw_a_up, rw_g_up, rw_k_k, rw_k_a, rw_r_k, rw_ln_g, rw_ln_b, ssd_conv_w, ssd_conv_b, ssd_dt_bias, ssd_a_log, ssd_d, ssd_norm_g, da_lambda, da_subln_g, na_rpb, w_gate, gate_b, w_br, w_out, ffn_up, ffn_conv_w, ffn_conv_b, ffn_down, final_norm_g):
    p = dict(w_in=w_in, rw_mu=rw_mu, rw_w0=rw_w0, rw_w_up=rw_w_up, rw_a0=rw_a0, rw_a_up=rw_a_up,
             rw_g_up=rw_g_up, rw_k_k=rw_k_k, rw_k_a=rw_k_a, rw_r_k=rw_r_k, rw_ln_g=rw_ln_g,
             rw_ln_b=rw_ln_b, ssd_conv_w=ssd_conv_w, ssd_conv_b=ssd_conv_b, ssd_dt_bias=ssd_dt_bias,
             ssd_a_log=ssd_a_log, ssd_d=ssd_d, ssd_norm_g=ssd_norm_g, da_lambda=da_lambda,
             da_subln_g=da_subln_g, na_rpb=na_rpb, w_gate=w_gate, gate_b=gate_b, w_br=w_br,
             w_out=w_out, ffn_up=ffn_up, ffn_conv_w=ffn_conv_w, ffn_conv_b=ffn_conv_b,
             ffn_down=ffn_down)
    bsz, seq, d = x.shape
    depth = ada_w.shape[0]
    lctx = ctx.shape[1]
    mrows = -(-(bsz + 1) // 16) * 16
    cc = jnp.zeros((mrows, d), F32).at[:bsz].set(c).at[bsz].set(c_ctx)
    cos, sin = _rope_tables(seq)
    xl, xc = x, ctx
    for i in range(depth):
        last = i == depth - 1
        lp = _layer_params(p, i)
        mod = _modulation(cc, ada_w[i], ada_b[i]).reshape(mrows, 6, d)
        ml = [mod[:bsz, n][:, None, :] for n in range(6)]
        mc = [mod[bsz:bsz + 1, n][:, None, :] for n in range(6)]

        hl = _norm_mod(xl, norm1_g[i], ml[0], ml[1])
        hc = _norm_mod(xc, norm1_g[i], mc[0], mc[1])
        zl = _project(hl, lp)
        zc = _project(hc, lp)

        rp_c, lw_c = _rw_prep(zc[0], lp)
        rp_l, lw_l = _rw_prep(zl[0], lp)
        s_zero = jnp.zeros((bsz, 2, RW_HEADS // RW_GROUP, RW_GW, RW_GW), F32)
        of_c, ob_c, s_c = _rw_scan(rp_c, lw_c, s_zero)
        of_l, ob_l, _ = _rw_scan(rp_l, lw_l, s_c)
        a_l = _rw_readout(of_l, ob_l, rp_l, lp)

        xbc_c, dt_c = _ssd_prep(zc[1], lp)
        xbc_l, dt_l = _ssd_prep(zl[1], lp)
        h_zero = jnp.zeros((bsz, 2, SSD_GROUPS, SSD_STATE, SSD_GW), F32)
        yf_c, yb_c, hfin_c = _ssd_scan(xbc_c, dt_c, lp['ssd_a'], h_zero)
        yf_l, yb_l, _ = _ssd_scan(xbc_l, dt_l, lp['ssd_a'], hfin_c)
        b_l = _ssd_readout(yf_l, yb_l, xbc_l, zl[1], lp)

        q_l, k_l, v_l = _qkv_proj(hl, lp['w_da'], cos, sin, True)
        q_c, k_c, v_c = _qkv_proj(hc, lp['w_da'], cos, sin, False)
        c_l = _da_attn(q_l, [k_l, k_c], [v_l, v_c], lp['da_lam'], lp['da_g'], lp['da_lam_init'])

        na_l = _qkv_proj(hl, lp['w_na'], cos, sin, False)
        na_c = _qkv_proj(hc, lp['w_na'], cos, sin, False)
        d_l = _na_attn(na_l, na_c, *_na_bias(lp['na_rpb'], seq))

        merged = _gated_merge(hl.reshape(bsz * seq, d),
                              [t.reshape(bsz * seq, BR_W) for t in (a_l, b_l, c_l, d_l)],
                              lp['w_gate'], lp['gate_b'], lp['w_br']).reshape(bsz, seq, d)
        xl, hl2 = _matmul_residual_norm(merged, lp['w_out'], xl, ml[2], norm2_g[i], ml[3], ml[4])
        xl = _ffn(xl, hl2, lp, ml[5])

        if not last:
            a_c = _rw_readout(of_c, ob_c, rp_c, lp)
            b_c = _ssd_readout(yf_c, yb_c, xbc_c, zc[1], lp)
            c_c = _da_attn(q_c, [k_c], [v_c], lp['da_lam'], lp['da_g'], lp['da_lam_init'])
            d_c = _ctx_attn(na_c)
            merged_c = _gated_merge(hc.reshape(bsz * lctx, d),
                                    [t.reshape(bsz * lctx, BR_W) for t in (a_c, b_c, c_c, d_c)],
                                    lp['w_gate'], lp['gate_b'], lp['w_br']).reshape(bsz, lctx, d)
            xc, hc2 = _matmul_residual_norm(merged_c, lp['w_out'], xc, mc[2], norm2_g[i], mc[3], mc[4])
            xc = _ffn(xc, hc2, lp, mc[5])
    return _final_norm(xl, final_norm_g)
```

```python
import functools
import math

import numpy as np
import jax
import jax.numpy as jnp
from jax import lax
from jax.experimental import pallas as pl
from jax.experimental.pallas import tpu as pltpu

F32 = jnp.float32
BF16 = jnp.bfloat16

GRID_W = 64
NORM_EPS = 1e-6
HD = 64
BR_W = 512
RW_HEADS = 8
RW_RANK = 64
RW_GATE_RANK = 128
RW_GN_EPS = 64e-5
RW_IN = 3 * BR_W + 4 * RW_RANK + RW_GATE_RANK
RW_CHUNK = 64
RW_NARR = 9
RW_R, RW_V, RW_KK, RW_KD, RW_BD, RW_G, RW_BONUS = 0, 1, 2, 3, 4, 7, 8
RW_GROUP = 2
RW_GW = RW_GROUP * HD
RW_BATCH = 2
SSD_HEADS = 8
SSD_GROUPS = 2
SSD_STATE = 128
SSD_XBC = BR_W + 2 * SSD_GROUPS * SSD_STATE
SSD_IN = BR_W + SSD_XBC + 2 * SSD_HEADS
SSD_IN_PAD = BR_W + SSD_XBC + 128
SSD_CHUNK = 128
SSD_GW = SSD_HEADS // SSD_GROUPS * HD
DA_HEADS = 4
DA_HPS = 4
DA_IN = 3 * BR_W
DA_SUBLN_EPS = 1e-5
NA_HEADS = 8
NA_WIN_R = 8
NA_WIN_C = 16
NA_QROWS = 4
NA_KROWS = 12
ROPE_BASE = 10000.0
NEG = -1e30
LANES = 128
VMEM_LIMIT = 56 * 1024 * 1024


def _cp(*sem):
    return pltpu.CompilerParams(dimension_semantics=sem, vmem_limit_bytes=VMEM_LIMIT)


def _tile(n, pref, mult):
    t = min(n, pref)
    t -= t % mult
    while t >= mult:
        if n % t == 0:
            return t
        t -= mult
    return n


def _dot(a, b):
    return jnp.dot(a.astype(BF16), b.astype(BF16), preferred_element_type=F32)


def _dot_nt(a, b):
    return lax.dot_general(a.astype(BF16), b.astype(BF16), (((1,), (1,)), ((), ())),
                           preferred_element_type=F32)


def _dot_tn(a, b):
    return lax.dot_general(a.astype(BF16), b.astype(BF16), (((0,), (0,)), ((), ())),
                           preferred_element_type=F32)


def _split3(x):
    hi = x.astype(BF16)
    r1 = x - hi.astype(F32)
    mid = r1.astype(BF16)
    lo = (r1 - mid.astype(F32)).astype(BF16)
    return hi, mid, lo


def _dot01(m01, x):
    hi, mid, lo = _split3(x)
    return (jnp.dot(m01, hi, preferred_element_type=F32)
            + jnp.dot(m01, mid, preferred_element_type=F32)
            + jnp.dot(m01, lo, preferred_element_type=F32))


def _x_dot01(x, m01):
    hi, mid, lo = _split3(x)
    return (jnp.dot(hi, m01, preferred_element_type=F32)
            + jnp.dot(mid, m01, preferred_element_type=F32)
            + jnp.dot(lo, m01, preferred_element_type=F32))


def _sigmoid(x):
    return 1.0 / (1.0 + jnp.exp(-x))


def _silu(x):
    return x * _sigmoid(x)


def _softplus(x):
    return jnp.maximum(x, 0.0) + jnp.log(1.0 + jnp.exp(-jnp.abs(x)))


def _mod_kernel(c_ref, w_ref, b_ref, o_ref):
    a = _silu(c_ref[...])
    o_ref[...] = _dot(a, w_ref[...]) + b_ref[...]


def _modulation(cc, w, b):
    m, d = cc.shape
    n = w.shape[1]
    tn = _tile(n, 1024, LANES)
    return pl.pallas_call(
        _mod_kernel,
        grid=(n // tn,),
        in_specs=[pl.BlockSpec((m, d), lambda j: (0, 0)),
                  pl.BlockSpec((d, tn), lambda j: (0, j)),
                  pl.BlockSpec((1, tn), lambda j: (0, j))],
        out_specs=pl.BlockSpec((m, tn), lambda j: (0, j)),
        out_shape=jax.ShapeDtypeStruct((m, n), F32),
        compiler_params=_cp("parallel"),
        name="adaln_mod",
    )(cc, w, b.reshape(1, n))


def _norm_mod_kernel(x_ref, g_ref, sh_ref, sc_ref, o_ref):
    x = x_ref[0]
    y = x * lax.rsqrt(jnp.mean(x * x, axis=-1, keepdims=True) + NORM_EPS) * g_ref[...]
    o_ref[0] = (y * (1.0 + sc_ref[0]) + sh_ref[0]).astype(o_ref.dtype)


def _norm_mod(x, g, shift, scale):
    bsz, length, d = x.shape
    tr = _tile(length, 256, 16)
    bm = shift.shape[0]
    mod_map = (lambda b, j: (b, 0, 0)) if bm == bsz else (lambda b, j: (0, 0, 0))
    return pl.pallas_call(
        _norm_mod_kernel,
        grid=(bsz, length // tr),
        in_specs=[pl.BlockSpec((1, tr, d), lambda b, j: (b, j, 0)),
                  pl.BlockSpec((1, d), lambda b, j: (0, 0)),
                  pl.BlockSpec((1, 1, d), mod_map),
                  pl.BlockSpec((1, 1, d), mod_map)],
        out_specs=pl.BlockSpec((1, tr, d), lambda b, j: (b, j, 0)),
        out_shape=jax.ShapeDtypeStruct((bsz, length, d), BF16),
        compiler_params=_cp("parallel", "parallel"),
        name="norm_mod",
    )(x, g.reshape(1, d), shift, scale)


def _final_norm_kernel(x_ref, g_ref, o_ref):
    x = x_ref[0]
    o_ref[0] = x * lax.rsqrt(jnp.mean(x * x, axis=-1, keepdims=True) + NORM_EPS) * g_ref[...]


def _final_norm(x, g):
    bsz, length, d = x.shape
    tr = _tile(length, 256, 8)
    return pl.pallas_call(
        _final_norm_kernel,
        grid=(bsz, length // tr),
        in_specs=[pl.BlockSpec((1, tr, d), lambda b, j: (b, j, 0)),
                  pl.BlockSpec((1, d), lambda b, j: (0, 0))],
        out_specs=pl.BlockSpec((1, tr, d), lambda b, j: (b, j, 0)),
        out_shape=jax.ShapeDtypeStruct((bsz, length, d), F32),
        compiler_params=_cp("parallel", "parallel"),
        name="final_norm",
    )(x, g.reshape(1, d))


def _mm_kernel(a_ref, w_ref, o_ref):
    o_ref[...] = jnp.dot(a_ref[...], w_ref[...], preferred_element_type=F32).astype(o_ref.dtype)


def _matmul(a, w, out_dtype, tm_pref=1024, tn_pref=2048):
    m, k = a.shape
    n = w.shape[1]
    tm = _tile(m, tm_pref, 16)
    tn = _tile(n, tn_pref, LANES)
    w_mode = dict(pipeline_mode=pl.Buffered(1)) if tn == n else {}
    return pl.pallas_call(
        _mm_kernel,
        grid=(m // tm, n // tn),
        in_specs=[pl.BlockSpec((tm, k), lambda i, j: (i, 0)),
                  pl.BlockSpec((k, tn), lambda i, j: (0, j), **w_mode)],
        out_specs=pl.BlockSpec((tm, tn), lambda i, j: (i, j)),
        out_shape=jax.ShapeDtypeStruct((m, n), out_dtype),
        compiler_params=_cp("parallel", "parallel"),
        name="matmul",
    )(a, w)


def _mm_res_kernel(a_ref, w_ref, x_ref, g_ref, o_ref):
    y = jnp.dot(a_ref[0], w_ref[...], preferred_element_type=F32)
    o_ref[0] = x_ref[0] + g_ref[0] * y


def _matmul_residual(a, w, x, gate):
    bsz, length, k = a.shape
    n = w.shape[1]
    tm = _tile(length, 1024, 16)
    tn = _tile(n, 512, LANES)
    bm = gate.shape[0]
    gmap = (lambda b, i, j: (b, 0, j)) if bm == bsz else (lambda b, i, j: (0, 0, j))
    return pl.pallas_call(
        _mm_res_kernel,
        grid=(bsz, length // tm, n // tn),
        in_specs=[pl.BlockSpec((1, tm, k), lambda b, i, j: (b, i, 0)),
                  pl.BlockSpec((k, tn), lambda b, i, j: (0, j)),
                  pl.BlockSpec((1, tm, tn), lambda b, i, j: (b, i, j)),
                  pl.BlockSpec((1, 1, tn), gmap)],
        out_specs=pl.BlockSpec((1, tm, tn), lambda b, i, j: (b, i, j)),
        out_shape=jax.ShapeDtypeStruct((bsz, length, n), F32),
        compiler_params=_cp("parallel", "parallel", "parallel"),
        name="matmul_residual",
    )(a, w, x, gate)


def _mm_res_norm_kernel(a_ref, w_ref, x_ref, g_ref, ng_ref, sh_ref, sc_ref, xo_ref, h_ref):
    xn = x_ref[0] + g_ref[0] * jnp.dot(a_ref[0], w_ref[...], preferred_element_type=F32)
    xo_ref[0] = xn
    y = xn * lax.rsqrt(jnp.mean(xn * xn, axis=-1, keepdims=True) + NORM_EPS) * ng_ref[...]
    h_ref[0] = (y * (1.0 + sc_ref[0]) + sh_ref[0]).astype(h_ref.dtype)


def _matmul_residual_norm(a, w, x, gate, norm_g, shift, scale):
    bsz, length, k = a.shape
    n = w.shape[1]
    tm = _tile(length, 512, 16)
    bm = gate.shape[0]
    mmap = (lambda b, i: (b, 0, 0)) if bm == bsz else (lambda b, i: (0, 0, 0))
    row = pl.BlockSpec((1, tm, n), lambda b, i: (b, i, 0))
    mod = pl.BlockSpec((1, 1, n), mmap)
    return pl.pallas_call(
        _mm_res_norm_kernel,
        grid=(bsz, length // tm),
        in_specs=[pl.BlockSpec((1, tm, k), lambda b, i: (b, i, 0)),
                  pl.BlockSpec((k, n), lambda b, i: (0, 0)),
                  row, mod, pl.BlockSpec((1, n), lambda b, i: (0, 0)), mod, mod],
        out_specs=[row, row],
        out_shape=[jax.ShapeDtypeStruct((bsz, length, n), F32),
                   jax.ShapeDtypeStruct((bsz, length, n), BF16)],
        compiler_params=_cp("parallel", "parallel"),
        name="outproj_norm",
    )(a, w, x, gate, norm_g.reshape(1, n), shift, scale)


def _merge_kernel(h_ref, o0_ref, o1_ref, o2_ref, o3_ref, wg_ref, gb_ref, wbr_ref, out_ref):
    h = h_ref[...]
    acc = None
    for n, o_ref in enumerate((o0_ref, o1_ref, o2_ref, o3_ref)):
        gate = _sigmoid(jnp.dot(h, wg_ref[n], preferred_element_type=F32) + gb_ref[n])
        term = gate * jnp.dot(o_ref[...], wbr_ref[n], preferred_element_type=F32)
        acc = term if acc is None else acc + term
    out_ref[...] = acc.astype(out_ref.dtype)


def _gated_merge(h, branches, wg, gb, wbr):
    m, d = h.shape
    nb = len(branches)
    tm = _tile(m, 1024, 16)
    tn = _tile(d, 512, LANES)
    bspec = pl.BlockSpec((tm, BR_W), lambda i, j: (i, 0))
    return pl.pallas_call(
        _merge_kernel,
        grid=(m // tm, d // tn),
        in_specs=[pl.BlockSpec((tm, d), lambda i, j: (i, 0)), bspec, bspec, bspec, bspec,
                  pl.BlockSpec((nb, d, tn), lambda i, j: (0, 0, j)),
                  pl.BlockSpec((nb, 1, tn), lambda i, j: (0, 0, j)),
                  pl.BlockSpec((nb, BR_W, tn), lambda i, j: (0, 0, j))],
        out_specs=pl.BlockSpec((tm, tn), lambda i, j: (i, j)),
        out_shape=jax.ShapeDtypeStruct((m, d), BF16),
        compiler_params=_cp("parallel", "parallel"),
        name="gated_merge",
    )(h, *branches, wg, gb, wbr)


def _conv3(x, prev_row, next_row, w_ref, b_ref):
    rows = x.shape[0]
    rid = lax.broadcasted_iota(jnp.int32, x.shape, 0)
    prev = jnp.where(rid == 0, prev_row, pltpu.roll(x, 1, 0))
    nxt = jnp.where(rid == rows - 1, next_row, pltpu.roll(x, rows - 1, 0))
    return b_ref[...] + prev * w_ref[0:1, :] + x * w_ref[1:2, :] + nxt * w_ref[2:3, :]


def _ffn_up_kernel(*refs, nt, halo):
    if nt == 1:
        a_ref = refs[0]
        refs = refs[1:]
        halo = 0
        a_ext = a_ref[0]
    else:
        a_ref, ap_ref, an_ref = refs[0:3]
        refs = refs[3:]
        a_ext = jnp.concatenate([ap_ref[0], a_ref[0], an_ref[0]], axis=0)
    wg_ref, wv_ref, cwg_ref, cbg_ref, cwv_ref, cbv_ref, o_ref = refs
    i = pl.program_id(1)
    tm = a_ref.shape[1]
    n_ext = tm + 2 * halo
    rid = lax.broadcasted_iota(jnp.int32, (tm, 1), 0)
    no_prev = jnp.logical_and(i == 0, rid == 0)
    no_next = jnp.logical_and(i == nt - 1, rid == tm - 1)

    def branch(w_ref, cw_ref, cb_ref):
        u = jnp.dot(a_ext, w_ref[...], preferred_element_type=F32)
        um1 = jnp.where(no_prev, 0.0, pltpu.roll(u, 1, 0)[halo:halo + tm])
        up1 = jnp.where(no_next, 0.0, pltpu.roll(u, n_ext - 1, 0)[halo:halo + tm])
        return (cb_ref[...] + um1 * cw_ref[0:1, :] + u[halo:halo + tm] * cw_ref[1:2, :]
                + up1 * cw_ref[2:3, :])

    gate = branch(wg_ref, cwg_ref, cbg_ref)
    val = branch(wv_ref, cwv_ref, cbv_ref)
    o_ref[0] = (_silu(gate) * val).astype(o_ref.dtype)


def _ffn_up(h, w_up, conv_w, conv_b):
    bsz, length, d = h.shape
    f2 = w_up.shape[1]
    f = f2 // 2
    halo = 16
    tm = _tile(length, 1024, halo)
    tc = _tile(f, 512, LANES)
    nt = length // tm
    nc = f // tc
    nh = length // halo
    rb = tm // halo

    def wspec(rows, off):
        return pl.BlockSpec((rows, tc), lambda b, i, c: (0, c + off))

    a_specs = [pl.BlockSpec((1, tm, d), lambda b, i, c: (b, i, 0))]
    if nt > 1:
        a_specs += [
            pl.BlockSpec((1, halo, d), lambda b, i, c: (b, jnp.maximum(i * rb - 1, 0), 0)),
            pl.BlockSpec((1, halo, d), lambda b, i, c: (b, jnp.minimum((i + 1) * rb, nh - 1), 0))]
    cb = conv_b.reshape(1, f2)
    return pl.pallas_call(
        functools.partial(_ffn_up_kernel, nt=nt, halo=halo),
        grid=(bsz, nt, nc),
        in_specs=a_specs + [wspec(d, 0), wspec(d, nc), wspec(3, 0), wspec(1, 0), wspec(3, nc),
                            wspec(1, nc)],
        out_specs=pl.BlockSpec((1, tm, tc), lambda b, i, c: (b, i, c)),
        out_shape=jax.ShapeDtypeStruct((bsz, length, f), BF16),
        compiler_params=_cp("parallel", "parallel", "parallel"),
        name="ffn_up",
    )(*([h] * len(a_specs)), w_up, w_up, conv_w, cb, conv_w, cb)


def _rw_prep_kernel(z_ref, zp_ref, zn_ref, mu_ref, w0_ref, wup_ref, a0_ref, aup_ref, gup_ref,
                    kk_ref, ka_ref, rk_ref, ones_ref, o_ref, lw_out_ref, *, nt):
    j = pl.program_id(1)
    p = z_ref[0]
    rows = p.shape[0]
    rid = lax.broadcasted_iota(jnp.int32, p.shape, 0)
    prow = jnp.where(j > 0, zp_ref[0, 7:8, :], 0.0)
    nrow = jnp.where(j < nt - 1, zn_ref[0, 0:1, :], 0.0)
    prev = jnp.where(rid == 0, prow, pltpu.roll(p, 1, 0))
    nxt = jnp.where(rid == rows - 1, nrow, pltpu.roll(p, rows - 1, 0))
    ps = p + mu_ref[0:1, :] * (prev - p) + mu_ref[1:2, :] * (nxt - p)

    w = BR_W
    r = ps[:, 0:w]
    k = ps[:, w:2 * w]
    v = ps[:, 2 * w:3 * w]
    wd = ps[:, 3 * w:3 * w + 128]
    ad = ps[:, 3 * w + 128:3 * w + 256]
    gd = ps[:, 3 * w + 256:3 * w + 384]

    wraw = _dot(jnp.tanh(wd), wup_ref[...])
    araw = _dot(ad, aup_ref[...])
    g = _dot(_sigmoid(gd), gup_ref[...])
    ones = ones_ref[...]
    kkv = k * kk_ref[...]
    ss = _x_dot01(kkv * kkv, ones)
    kkn = kkv / jnp.maximum(jnp.sqrt(ss), 1e-12)

    def put(i, x):
        o_ref[0, :, i * w:(i + 1) * w] = x.astype(o_ref.dtype)

    put(RW_R, r)
    put(RW_V, v)
    put(RW_KK, kkn)
    ksum = None
    for d in range(2):
        lw = -math.exp(-0.5) * _sigmoid(w0_ref[d:d + 1, :] + wraw[:, d * w:(d + 1) * w])
        lw_out_ref[0, :, d * w:(d + 1) * w] = lw
        a = _sigmoid(a0_ref[d:d + 1, :] + araw[:, d * w:(d + 1) * w])
        kd = k * (1.0 + (a - 1.0) * ka_ref[...])
        put(RW_KD + 2 * d, kd)
        put(RW_BD + 2 * d, kkn * a)
        ksum = kd if ksum is None else ksum + kd
    put(RW_G, g)
    put(RW_BONUS, _x_dot01(r * rk_ref[...] * ksum, ones) * v)


def _rw_prep(z, lp):
    bsz, length, fin = z.shape
    tr = _tile(length, 256, 8)
    nt = length // tr
    rb = tr // 8
    nh = length // 8
    full = lambda a: pl.BlockSpec(a.shape, lambda b, j: (0,) * a.ndim)
    params = (lp['rw_mu'], lp['rw_w0'], lp['rw_wup'], lp['rw_a0'], lp['rw_aup'], lp['rw_gup'],
              lp['rw_kk'], lp['rw_ka'], lp['rw_rk'], lp['ones_hd'])
    return pl.pallas_call(
        functools.partial(_rw_prep_kernel, nt=nt),
        grid=(bsz, nt),
        in_specs=[pl.BlockSpec((1, tr, fin), lambda b, j: (b, j, 0)),
                  pl.BlockSpec((1, 8, fin), lambda b, j: (b, jnp.maximum(j * rb - 1, 0), 0)),
                  pl.BlockSpec((1, 8, fin), lambda b, j: (b, jnp.minimum((j + 1) * rb, nh - 1), 0))]
                 + [full(a) for a in params],
        out_specs=[pl.BlockSpec((1, tr, RW_NARR * BR_W), lambda b, j: (b, j, 0)),
                   pl.BlockSpec((1, tr, 2 * BR_W), lambda b, j: (b, j, 0))],
        out_shape=[jax.ShapeDtypeStruct((bsz, length, RW_NARR * BR_W), BF16),
                   jax.ShapeDtypeStruct((bsz, length, 2 * BR_W), F32)],
        compiler_params=_cp("parallel", "parallel"),
        name="rwkv_prep",
    )(z, z, z, *params)


def _rw_chunks(dir_refs, o_refs, s_scr, nb):
    cs = RW_CHUNK
    gw = RW_GW
    ng = RW_HEADS // RW_GROUP
    ri = lax.broadcasted_iota(jnp.int32, (gw, gw), 0)
    ci = lax.broadcasted_iota(jnp.int32, (gw, gw), 1)
    head_mask = (ri >> 6) == (ci >> 6)
    tw = lax.broadcasted_iota(jnp.int32, (cs, gw), 0)
    iw = lax.broadcasted_iota(jnp.int32, (cs, gw), 1) & (cs - 1)
    eye = jnp.where(iw == tw, 1.0, 0.0)
    same = {s: (tw >> int(math.log2(s))) == (iw >> int(math.log2(s))) for s in (8, 16, 32)}
    ti = lax.broadcasted_iota(jnp.int32, (cs, cs), 0)
    ii = lax.broadcasted_iota(jnp.int32, (cs, cs), 1)

    def expand(x):
        xb = x.astype(BF16)
        return jnp.where(head_mask, jnp.concatenate([xb] * RW_GROUP, axis=0), jnp.zeros_like(xb[0:1, 0:1]))

    def mm(a, b):
        return _dot(a, expand(b))

    chains = []
    for bb in range(nb):
        for d, (r_ref, v_ref, kk_ref, lw_ref, k_ref, b_ref) in enumerate(dir_refs):
            sgn = 1 - 2 * d
            rel = (iw - tw) * sgn
            tri = jnp.where((ii - ti) * sgn <= 0, 1.0, 0.0).astype(BF16)
            lw = lw_ref[bb]
            cum = _dot01(tri, lw)
            tot = jnp.sum(lw, axis=0, keepdims=True)
            kd, bd = k_ref[bb].astype(F32), b_ref[bb].astype(F32)
            qt = kk_ref[bb].astype(F32) * jnp.exp(cum - lw)
            rt = r_ref[bb].astype(F32) * jnp.exp(cum)
            e_neg = jnp.exp(-cum)
            e_end = jnp.exp(tot - cum)
            kt = kd * e_neg
            bt = bd * e_neg
            kh = kd * e_end
            bh = bd * e_end
            p_tot = jnp.exp(tot)
            v = v_ref[bb].astype(F32)
            for g in range(ng):
                sl = slice(g * gw, (g + 1) * gw)
                chains.append(dict(bb=bb, d=d, g=g, sl=sl, strict=rel < 0, incl=rel <= 0,
                                   q=qt[:, sl], r=rt[:, sl], v=v[:, sl], kt=kt[:, sl],
                                   bt=bt[:, sl], kh=kh[:, sl], bh=bh[:, sl], p_tot=p_tot[:, sl],
                                   s0=s_scr[bb, d, g]))

    for c in chains:
        c['kte'], c['bte'], c['ve'] = expand(c['kt']), expand(c['bt']), expand(c['v'])
        c['s0b'] = c['s0'].astype(BF16)
    for c in chains:
        c['lmat'] = jnp.where(c['strict'], _dot_nt(c['q'], c['bte']), 0.0)
        c['ld'] = jnp.where(same[8], c['lmat'], 0.0)
    for c in chains:
        c['ld2'] = mm(c['ld'], c['ld'])
    for c in chains:
        c['a_qk'] = jnp.where(c['strict'], _dot_nt(c['q'], c['kte']), 0.0)
    for c in chains:
        c['ld4'] = mm(c['ld2'], c['ld2'])
    for c in chains:
        c['x'] = mm(eye - c['ld'], eye + c['ld2'])
    for c in chains:
        c['a_rk'] = jnp.where(c['incl'], _dot_nt(c['r'], c['kte']), 0.0)
    for c in chains:
        c['x'] = mm(c['x'], eye + c['ld4'])
    for c in chains:
        c['a_rb'] = jnp.where(c['incl'], _dot_nt(c['r'], c['bte']), 0.0)
    for c in chains:
        c['rhs'] = _dot_nt(c['q'], c['s0b']) + _dot(c['a_qk'], c['ve'])
    for s in (8, 16, 32):
        off_mask = jnp.logical_not(same[s])
        if 2 * s < cs:
            off_mask = jnp.logical_and(same[2 * s], off_mask)
        for c in chains:
            c['t'] = mm(c['x'], jnp.where(off_mask, c['lmat'], 0.0))
        if s == 8:
            for c in chains:
                c['o'] = _dot_nt(c['r'], c['s0b']) + _dot(c['a_rk'], c['ve'])
        for c in chains:
            c['x'] = c['x'] - mm(c['t'], c['x'])
    for c in chains:
        c['sa'] = mm(c['x'], c['rhs'])
    for c in chains:
        o_refs[c['d']][c['bb'], :, c['sl']] = c['o'] - mm(c['a_rb'], c['sa'])
    for c in chains:
        upd = _dot_tn(jnp.concatenate([c['v'], c['sa']], axis=0),
                      jnp.concatenate([c['kh'], -c['bh']], axis=0))
        s_scr[c['bb'], c['d'], c['g']] = c['s0'] * c['p_tot'] + jnp.where(head_mask, upd, 0.0)


def _rw_scan_kernel(*refs, nc, nb):
    fwd, bwd = refs[0:6], refs[6:12]
    s0_ref, of_ref, ob_ref, sfin_ref, s_scr = refs[12:]
    c = pl.program_id(1)

    @pl.when(c == 0)
    def _():
        s_scr[...] = s0_ref[...]

    _rw_chunks((fwd, bwd), (of_ref, ob_ref), s_scr, nb)

    @pl.when(c == nc - 1)
    def _():
        sfin_ref[...] = s_scr[...]


def _rw_scan(rwp, lwp, s_init):
    bsz, length, _ = rwp.shape
    cs = RW_CHUNK
    nc = length // cs
    ng = RW_HEADS // RW_GROUP
    nb = RW_BATCH if bsz % RW_BATCH == 0 else 1

    def arr(d, i):
        if d == 0:
            return pl.BlockSpec((nb, cs, BR_W), lambda b, c: (b, c, i))
        return pl.BlockSpec((nb, cs, BR_W), lambda b, c: (b, nc - 1 - c, i))

    def specs(d):
        return [arr(d, RW_R), arr(d, RW_V), arr(d, RW_KK), arr(d, d),
                arr(d, RW_KD + 2 * d), arr(d, RW_BD + 2 * d)]

    def operands():
        return [rwp, rwp, rwp, lwp, rwp, rwp]

    sspec = pl.BlockSpec((nb, 2, ng, RW_GW, RW_GW), lambda b, c: (b, 0, 0, 0, 0))
    oshape = jax.ShapeDtypeStruct((bsz, length, BR_W), F32)
    return pl.pallas_call(
        functools.partial(_rw_scan_kernel, nc=nc, nb=nb),
        grid=(bsz // nb, nc),
        in_specs=specs(0) + specs(1) + [sspec],
        out_specs=[pl.BlockSpec((nb, cs, BR_W), lambda b, c: (b, c, 0)),
                   pl.BlockSpec((nb, cs, BR_W), lambda b, c: (b, nc - 1 - c, 0)), sspec],
        out_shape=[oshape, oshape, jax.ShapeDtypeStruct((bsz, 2, ng, RW_GW, RW_GW), F32)],
        scratch_shapes=[pltpu.VMEM((nb, 2, ng, RW_GW, RW_GW), F32)],
        compiler_params=_cp("parallel", "arbitrary"),
        name="rwkv_scan",
    )(*operands(), *operands(), s_init)


def _rw_readout_kernel(of_ref, ob_ref, g_ref, bonus_ref, lng_ref, lnb_ref, mean_ref, out_ref):
    o = of_ref[0] + ob_ref[0]
    mean_m = mean_ref[...]
    mu = _x_dot01(o, mean_m)
    dlt = o - mu
    var = _x_dot01(dlt * dlt, mean_m)
    on = dlt * lax.rsqrt(var + RW_GN_EPS) * lng_ref[...] + lnb_ref[...]
    out_ref[0] = ((on + bonus_ref[0].astype(F32)) * g_ref[0].astype(F32)).astype(out_ref.dtype)


def _rw_readout(o_f, o_b, rwp, lp):
    bsz, length, w = o_f.shape
    tr = _tile(length, 256, 16)
    ospec = pl.BlockSpec((1, tr, w), lambda b, j: (b, j, 0))
    return pl.pallas_call(
        _rw_readout_kernel,
        grid=(bsz, length // tr),
        in_specs=[ospec, ospec,
                  pl.BlockSpec((1, tr, w), lambda b, j: (b, j, RW_G)),
                  pl.BlockSpec((1, tr, w), lambda b, j: (b, j, RW_BONUS)),
                  pl.BlockSpec((1, w), lambda b, j: (0, 0)),
                  pl.BlockSpec((1, w), lambda b, j: (0, 0)),
                  pl.BlockSpec((w, w), lambda b, j: (0, 0))],
        out_specs=ospec,
        out_shape=jax.ShapeDtypeStruct((bsz, length, w), BF16),
        compiler_params=_cp("parallel", "parallel"),
        name="rwkv_readout",
    )(o_f, o_b, rwp, rwp, lp['rw_lng'], lp['rw_lnb'], lp['mean_hd'])


def _ssd_prep_kernel(z_ref, zp_ref, zn_ref, cw_ref, cb_ref, dtb_ref, xbc_ref, dt_ref, *, nt):
    j = pl.program_id(1)
    x = z_ref[0, :, BR_W:BR_W + SSD_XBC]
    prow = jnp.where(j > 0, zp_ref[0, 7:8, BR_W:BR_W + SSD_XBC], 0.0)
    nrow = jnp.where(j < nt - 1, zn_ref[0, 0:1, BR_W:BR_W + SSD_XBC], 0.0)
    xbc_ref[0] = _silu(_conv3(x, prow, nrow, cw_ref, cb_ref))
    dt_raw = z_ref[0, :, BR_W + SSD_XBC:BR_W + SSD_XBC + LANES]
    dt_ref[0, 0] = _softplus(dt_raw + dtb_ref[0:1, :])
    dt_ref[0, 1] = _softplus(pltpu.roll(dt_raw, LANES - SSD_HEADS, 1) + dtb_ref[1:2, :])


def _ssd_prep(z, lp):
    bsz, length, fin = z.shape
    tr = _tile(length, 256, 8)
    nt = length // tr
    rb = tr // 8
    nh = length // 8
    return pl.pallas_call(
        functools.partial(_ssd_prep_kernel, nt=nt),
        grid=(bsz, nt),
        in_specs=[pl.BlockSpec((1, tr, fin), lambda b, j: (b, j, 0)),
                  pl.BlockSpec((1, 8, fin), lambda b, j: (b, jnp.maximum(j * rb - 1, 0), 0)),
                  pl.BlockSpec((1, 8, fin), lambda b, j: (b, jnp.minimum((j + 1) * rb, nh - 1), 0)),
                  pl.BlockSpec((3, SSD_XBC), lambda b, j: (0, 0)),
                  pl.BlockSpec((1, SSD_XBC), lambda b, j: (0, 0)),
                  pl.BlockSpec((2, LANES), lambda b, j: (0, 0))],
        out_specs=[pl.BlockSpec((1, tr, SSD_XBC), lambda b, j: (b, j, 0)),
                   pl.BlockSpec((1, 2, tr, LANES), lambda b, j: (b, 0, j, 0))],
        out_shape=[jax.ShapeDtypeStruct((bsz, length, SSD_XBC), F32),
                   jax.ShapeDtypeStruct((bsz, 2, length, LANES), F32)],
        compiler_params=_cp("parallel", "parallel"),
        name="ssd_prep",
    )(z, z, z, lp['ssd_cw'], lp['ssd_cb'], lp['ssd_dtb'])


def _ssd_chunks(dir_refs, a_ref, y_refs, h_scr, nb):
    q = SSD_CHUNK
    n = SSD_STATE
    rep = SSD_HEADS // SSD_GROUPS
    gw = SSD_GW
    ti = lax.broadcasted_iota(jnp.int32, (q, q), 0)
    ii = lax.broadcasted_iota(jnp.int32, (q, q), 1)
    ri = lax.broadcasted_iota(jnp.int32, (rep * q, gw), 0)
    ci = lax.broadcasted_iota(jnp.int32, (rep * q, gw), 1)
    head_mask = (ri >> int(math.log2(q))) == (ci >> int(math.log2(HD)))

    def per_head(x, h0):
        return jnp.concatenate([jnp.broadcast_to(x[:, h0 + h:h0 + h + 1], (x.shape[0], HD))
                                for h in range(rep)], axis=1)

    work = []
    for bb in range(nb):
        for d, (xbc_ref, dt_ref) in enumerate(dir_refs):
            incl = (ii - ti) * (1 - 2 * d) <= 0
            tri = jnp.where(incl, 1.0, 0.0).astype(BF16)
            dt = dt_ref[bb, 0]
            dta = dt * a_ref[d]
            acs = _dot01(tri, dta)
            acs_t = acs.T
            tot = jnp.sum(dta, axis=0, keepdims=True)
            e_acs = jnp.exp(acs)
            e_end = jnp.exp(tot - acs)
            e_tot = jnp.exp(tot)
            xbc = xbc_ref[bb]
            for g in range(SSD_GROUPS):
                h0 = g * rep
                bg = xbc[:, BR_W + g * n:BR_W + (g + 1) * n]
                cg = xbc[:, BR_W + SSD_GROUPS * n + g * n:BR_W + SSD_GROUPS * n + (g + 1) * n]
                xdt = xbc[:, g * gw:(g + 1) * gw] * per_head(dt, h0)
                dec = jnp.concatenate(
                    [jnp.exp(jnp.where(incl, acs[:, h0 + h:h0 + h + 1] - acs_t[h0 + h:h0 + h + 1, :], NEG))
                     for h in range(rep)], axis=1)
                work.append(dict(bb=bb, d=d, g=g, bg=bg, cg=cg, xdt=xdt, dec=dec,
                                 ht=h_scr[bb, d, g], e_acs=per_head(e_acs, h0),
                                 e_end=per_head(e_end, h0), e_tot=per_head(e_tot, h0)))
    for w in work:
        w['cb'] = _dot_nt(w['cg'], w['bg'])
    for w in work:
        w['yoff'] = _dot(w['cg'], w['ht'])
    for w in work:
        w['st'] = _dot_tn(w['bg'], w['xdt'] * w['e_end'])
    for w in work:
        xe = jnp.where(head_mask, jnp.concatenate([w['xdt']] * rep, axis=0), 0.0)
        scores = jnp.concatenate([w['cb']] * rep, axis=1) * w['dec']
        y = _dot(scores, xe) + w['e_acs'] * w['yoff']
        y_refs[w['d']][w['bb'], :, w['g'] * gw:(w['g'] + 1) * gw] = y
        h_scr[w['bb'], w['d'], w['g']] = w['ht'] * w['e_tot'] + w['st']


def _ssd_scan_kernel(xf_ref, dtf_ref, xb_ref, dtb_ref, a_ref, h0_ref, yf_ref, yb_ref, hfin_ref,
                     h_scr, *, nc, nb):
    c = pl.program_id(1)

    @pl.when(c == 0)
    def _():
        h_scr[...] = h0_ref[...]

    _ssd_chunks(((xf_ref, dtf_ref), (xb_ref, dtb_ref)), a_ref, (yf_ref, yb_ref), h_scr, nb)

    @pl.when(c == nc - 1)
    def _():
        hfin_ref[...] = h_scr[...]


def _ssd_scan(xbc, dt, a_neg, h_init):
    bsz, length, _ = xbc.shape
    q = SSD_CHUNK
    nc = length // q
    nb = RW_BATCH if bsz % RW_BATCH == 0 else 1
    hspec = pl.BlockSpec((nb, 2, SSD_GROUPS, SSD_STATE, SSD_GW), lambda b, c: (b, 0, 0, 0, 0))
    oshape = jax.ShapeDtypeStruct((bsz, length, BR_W), F32)
    return pl.pallas_call(
        functools.partial(_ssd_scan_kernel, nc=nc, nb=nb),
        grid=(bsz // nb, nc),
        in_specs=[pl.BlockSpec((nb, q, SSD_XBC), lambda b, c: (b, c, 0)),
                  pl.BlockSpec((nb, 1, q, LANES), lambda b, c: (b, 0, c, 0)),
                  pl.BlockSpec((nb, q, SSD_XBC), lambda b, c: (b, nc - 1 - c, 0)),
                  pl.BlockSpec((nb, 1, q, LANES), lambda b, c: (b, 1, nc - 1 - c, 0)),
                  pl.BlockSpec((2, 1, LANES), lambda b, c: (0, 0, 0)),
                  hspec],
        out_specs=[pl.BlockSpec((nb, q, BR_W), lambda b, c: (b, c, 0)),
                   pl.BlockSpec((nb, q, BR_W), lambda b, c: (b, nc - 1 - c, 0)), hspec],
        out_shape=[oshape, oshape,
                   jax.ShapeDtypeStruct((bsz, 2, SSD_GROUPS, SSD_STATE, SSD_GW), F32)],
        scratch_shapes=[pltpu.VMEM((nb, 2, SSD_GROUPS, SSD_STATE, SSD_GW), F32)],
        compiler_params=_cp("parallel", "arbitrary"),
        name="ssd_scan",
    )(xbc, dt, xbc, dt, a_neg, h_init)


def _ssd_readout_kernel(yf_ref, yb_ref, xs_ref, z_ref, dsk_ref, ng_ref, out_ref):
    y = yf_ref[0] + yb_ref[0] + dsk_ref[...] * xs_ref[0]
    y = y * _silu(z_ref[0])
    y = y * lax.rsqrt(jnp.mean(y * y, axis=-1, keepdims=True) + NORM_EPS) * ng_ref[...]
    out_ref[0] = y.astype(out_ref.dtype)


def _ssd_readout(y_f, y_b, xbc, z, lp):
    bsz, length, w = y_f.shape
    tr = _tile(length, 256, 16)
    ospec = pl.BlockSpec((1, tr, w), lambda b, j: (b, j, 0))
    return pl.pallas_call(
        _ssd_readout_kernel,
        grid=(bsz, length // tr),
        in_specs=[ospec, ospec, ospec, ospec,
                  pl.BlockSpec((1, w), lambda b, j: (0, 0)),
                  pl.BlockSpec((1, w), lambda b, j: (0, 0))],
        out_specs=ospec,
        out_shape=jax.ShapeDtypeStruct((bsz, length, w), BF16),
        compiler_params=_cp("parallel", "parallel"),
        name="ssd_readout",
    )(y_f, y_b, xbc, z, lp['ssd_dskip'], lp['ssd_ng'])


def _qkv_proj_kernel(h_ref, w_ref, cos_ref, sin_ref, q_ref, k_ref, v_ref, *, rope):
    w = BR_W
    z = jnp.dot(h_ref[0], w_ref[...], preferred_element_type=F32)

    def rot(x):
        if not rope:
            return x
        lane = lax.broadcasted_iota(jnp.int32, (1, LANES), 1)
        first = (lane & 31) < 16
        outs = []
        for cb in range(w // LANES):
            xb = x[:, cb * LANES:(cb + 1) * LANES]
            sw = jnp.where(first, pltpu.roll(xb, LANES - 16, 1), pltpu.roll(xb, 16, 1))
            outs.append(xb * cos_ref[:, cb * LANES:(cb + 1) * LANES]
                        + sw * sin_ref[:, cb * LANES:(cb + 1) * LANES])
        return jnp.concatenate(outs, axis=1)

    q_ref[0] = (rot(z[:, 0:w]) * (HD ** -0.5)).astype(q_ref.dtype)
    k_ref[0] = rot(z[:, w:2 * w]).astype(k_ref.dtype)
    v_ref[0] = z[:, 2 * w:3 * w].astype(v_ref.dtype)


def _qkv_proj(h, w_da, cos, sin, rope):
    bsz, length, d = h.shape
    tr = _tile(length, 512, 16)
    w = BR_W
    ospec = pl.BlockSpec((1, tr, w), lambda b, j: (b, j, 0))
    oshape = jax.ShapeDtypeStruct((bsz, length, w), BF16)
    return pl.pallas_call(
        functools.partial(_qkv_proj_kernel, rope=rope),
        grid=(bsz, length // tr),
        in_specs=[pl.BlockSpec((1, tr, d), lambda b, j: (b, j, 0)),
                  pl.BlockSpec((d, 3 * w), lambda b, j: (0, 0), pipeline_mode=pl.Buffered(1)),
                  pl.BlockSpec((tr, w), lambda b, j: (j, 0)),
                  pl.BlockSpec((tr, w), lambda b, j: (j, 0))],
        out_specs=[ospec, ospec, ospec],
        out_shape=[oshape, oshape, oshape],
        compiler_params=_cp("parallel", "parallel"),
        name="qkv_proj",
    )(h, w_da, cos, sin)


def _da_attn_kernel(*refs, nsrc, lam_init):
    lam_ref, q_ref = refs[0], refs[1]
    kv = refs[2:2 + 2 * nsrc]
    g_ref, o_ref = refs[2 + 2 * nsrc], refs[3 + 2 * nsrc]
    lane = lax.broadcasted_iota(jnp.int32, (1, LANES), 1)
    lam = lam_ref[0:1, 0:1]
    heads = [slice(hh * LANES, (hh + 1) * LANES) for hh in range(DA_HPS)]
    scores = {}
    for hh, hs in enumerate(heads):
        q = q_ref[0, :, hs]
        zero = jnp.zeros_like(q)
        for m in range(2):
            qm = jnp.where((lane < HD) if m == 0 else (lane >= HD), q, zero)
            scores[hh, m] = [_dot_nt(qm, kv[2 * i][0, :, hs]) for i in range(nsrc)]
    for hh, hs in enumerate(heads):
        probs = []
        for m in range(2):
            ss = scores[hh, m]
            mx = functools.reduce(jnp.maximum, [jnp.max(s, axis=-1, keepdims=True) for s in ss])
            es = [jnp.exp(s - mx) for s in ss]
            den = functools.reduce(lambda a, b: a + b,
                                   [jnp.sum(e, axis=-1, keepdims=True) for e in es])
            inv = 1.0 / den
            probs.append([e * inv for e in es])
        o = None
        for i in range(nsrc):
            a = probs[0][i] - lam * probs[1][i]
            t = _dot(a, kv[2 * i + 1][0, :, hs])
            o = t if o is None else o + t
        y = o * lax.rsqrt(jnp.mean(o * o, axis=-1, keepdims=True) + DA_SUBLN_EPS) * g_ref[...]
        o_ref[0, :, hs] = (y * (1.0 - lam_init)).astype(o_ref.dtype)


def _da_attn(q, ks, vs, lam, subln_g, lam_init):
    bsz, lq, _ = q.shape
    tq = _tile(lq, 256, 16)
    nsrc = len(ks)
    hw = DA_HPS * LANES
    in_specs = [pl.BlockSpec((1, LANES), lambda b, h, j: (0, 0)),
                pl.BlockSpec((1, tq, hw), lambda b, h, j: (b, j, h))]
    args = [lam, q]
    for k, v in zip(ks, vs):
        lk = k.shape[1]
        in_specs.append(pl.BlockSpec((1, lk, hw), lambda b, h, j: (b, 0, h)))
        in_specs.append(pl.BlockSpec((1, lk, hw), lambda b, h, j: (b, 0, h)))
        args += [k, v]
    in_specs.append(pl.BlockSpec((1, LANES), lambda b, h, j: (0, 0)))
    args.append(subln_g)
    return pl.pallas_call(
        functools.partial(_da_attn_kernel, nsrc=nsrc, lam_init=lam_init),
        grid=(bsz, DA_HEADS // DA_HPS, lq // tq),
        in_specs=in_specs,
        out_specs=pl.BlockSpec((1, tq, hw), lambda b, h, j: (b, j, h)),
        out_shape=jax.ShapeDtypeStruct((bsz, lq, BR_W), BF16),
        compiler_params=_cp("parallel", "parallel", "parallel"),
        name="diff_attn",
    )(*args)


def _na_kb(i, rows):
    return jnp.clip(i * NA_QROWS - NA_WIN_R // 2, 0, rows - NA_KROWS)


def _na_attn_kernel(q_ref, k_ref, v_ref, kc_ref, vc_ref, *rest, rows, nsub):
    bias_refs, o_ref = rest[:nsub], rest[nsub]
    i = pl.program_id(2)
    tq = NA_QROWS * GRID_W
    nk = NA_KROWS * GRID_W
    kc = kc_ref[0]
    vc = vc_ref[0]
    lane = lax.broadcasted_iota(jnp.int32, (1, LANES), 1)
    for j in range(nsub):
        start = pl.multiple_of(_na_kb(i * nsub + j, rows) * GRID_W, GRID_W)
        q = q_ref[0, j * tq:(j + 1) * tq, :]
        zero = jnp.zeros_like(q)
        kw = k_ref[0, pl.ds(start, nk), :]
        vw = v_ref[0, pl.ds(start, nk), :]
        outs = []
        for h in range(2):
            qm = jnp.where((lane < HD) if h == 0 else (lane >= HD), q, zero)
            sw = _dot_nt(qm, kw) + bias_refs[j][h, 0]
            sc = _dot_nt(qm, kc)
            mx = jnp.maximum(jnp.max(sw, axis=-1, keepdims=True), jnp.max(sc, axis=-1, keepdims=True))
            ew = jnp.exp(sw - mx)
            ec = jnp.exp(sc - mx)
            inv = 1.0 / (jnp.sum(ew, axis=-1, keepdims=True) + jnp.sum(ec, axis=-1, keepdims=True))
            outs.append((_dot(ew, vw) + _dot(ec, vc)) * inv)
        o_ref[0, j * tq:(j + 1) * tq, :] = jnp.where(lane < HD, outs[0], outs[1]).astype(o_ref.dtype)


def _na_attn(qkv_l, qkv_c, bias, pat_id):
    q_l, k_l, v_l = qkv_l
    _, k_c, v_c = qkv_c
    bsz, length, _ = q_l.shape
    lc = k_c.shape[1]
    rows = length // GRID_W
    tq = NA_QROWS * GRID_W
    nblk = length // tq
    nsub = next(n for n in (4, 2, 1) if nblk % n == 0)

    def pat(i):
        out = pat_id[0]
        for j in range(1, nblk):
            if pat_id[j] != pat_id[j - 1]:
                out = out + jnp.where(i >= j, pat_id[j] - pat_id[j - 1], 0)
        return out

    def bias_spec(j):
        return pl.BlockSpec((2, 1, tq, NA_KROWS * GRID_W),
                            lambda b, p, i: (p, pat(i * nsub + j), 0, 0))

    qspec = pl.BlockSpec((1, nsub * tq, LANES), lambda b, p, i: (b, i, p))
    whole = lambda n: pl.BlockSpec((1, n, LANES), lambda b, p, i: (b, 0, p))
    return pl.pallas_call(
        functools.partial(_na_attn_kernel, rows=rows, nsub=nsub),
        grid=(bsz, NA_HEADS // 2, nblk // nsub),
        in_specs=[qspec, whole(length), whole(length), whole(lc), whole(lc)]
                 + [bias_spec(j) for j in range(nsub)],
        out_specs=qspec,
        out_shape=jax.ShapeDtypeStruct((bsz, length, BR_W), BF16),
        compiler_params=_cp("parallel", "parallel", "parallel"),
        name="na_attn",
    )(q_l, k_l, v_l, k_c, v_c, *([bias] * nsub))


def _ctx_attn_kernel(q_ref, k_ref, v_ref, o_ref):
    q = q_ref[0]
    zero = jnp.zeros_like(q)
    k = k_ref[0]
    v = v_ref[0]
    lane = lax.broadcasted_iota(jnp.int32, (1, LANES), 1)
    outs = []
    for h in range(2):
        qm = jnp.where((lane < HD) if h == 0 else (lane >= HD), q, zero)
        s = _dot_nt(qm, k)
        e = jnp.exp(s - jnp.max(s, axis=-1, keepdims=True))
        outs.append(_dot(e, v) * (1.0 / jnp.sum(e, axis=-1, keepdims=True)))
    o_ref[0] = jnp.where(lane < HD, outs[0], outs[1]).astype(o_ref.dtype)


def _ctx_attn(qkv_c):
    q_c, k_c, v_c = qkv_c
    bsz, lc, _ = q_c.shape
    spec = pl.BlockSpec((1, lc, LANES), lambda b, p: (b, 0, p))
    return pl.pallas_call(
        _ctx_attn_kernel,
        grid=(bsz, NA_HEADS // 2),
        in_specs=[spec, spec, spec],
        out_specs=spec,
        out_shape=jax.ShapeDtypeStruct((bsz, lc, BR_W), BF16),
        compiler_params=_cp("parallel", "parallel"),
        name="ctx_attn",
    )(q_c, k_c, v_c)


def _rope_tables(length):
    n_freq = HD // 4
    t = np.arange(length)
    pos = np.stack([t // GRID_W, t % GRID_W], axis=-1).astype(np.float32)
    inv = (ROPE_BASE ** (-np.arange(n_freq, dtype=np.float32) / n_freq)).astype(np.float32)
    lane = np.arange(BR_W)
    which = (lane % HD) // (HD // 2)
    ang = pos[:, which] * inv[lane % n_freq][None, :]
    sign = np.where((lane % (HD // 2)) < n_freq, -1.0, 1.0).astype(np.float32)
    return jnp.asarray(np.cos(ang), F32), jnp.asarray(np.sin(ang) * sign[None, :], F32)


def _na_bias(rpb, length):
    rows = length // GRID_W
    wr = min(NA_WIN_R, rows)
    nblk = rows // NA_QROWS
    qr = np.arange(rows)
    rstart = np.clip(qr - wr // 2, 0, rows - wr)
    kb = np.clip(np.arange(nblk) * NA_QROWS - NA_WIN_R // 2, 0, rows - NA_KROWS)
    qrow = (np.arange(nblk)[:, None] * NA_QROWS + np.arange(NA_QROWS)[None, :])
    krow = kb[:, None] + np.arange(NA_KROWS)[None, :]
    dr = krow[:, None, :] - qrow[:, :, None] + NA_WIN_R - 1
    rvalid = (krow[:, None, :] >= rstart[qrow][:, :, None]) & (krow[:, None, :] < rstart[qrow][:, :, None] + wr)
    dr = np.where(rvalid, dr, -1)
    pats, pat_id = [], []
    for i in range(nblk):
        for j, pdr in enumerate(pats):
            if np.array_equal(pdr, dr[i]):
                pat_id.append(j)
                break
        else:
            pat_id.append(len(pats))
            pats.append(dr[i])
    pdr = np.stack(pats)
    n_dr = 2 * NA_WIN_R - 1
    oh_dr = (pdr[..., None] == np.arange(n_dr)).astype(np.float32)
    cid = np.arange(GRID_W)
    cstart = np.clip(cid - NA_WIN_C // 2, 0, GRID_W - NA_WIN_C)
    in_win = (cid[None, :] >= cstart[:, None]) & (cid[None, :] < cstart[:, None] + NA_WIN_C)
    ci = np.clip(cid[None, :] - cid[:, None], -(NA_WIN_C - 1), NA_WIN_C - 1) + NA_WIN_C - 1
    n_ci = 2 * NA_WIN_C - 1
    oh_ci = (ci[..., None] == np.arange(n_ci)).astype(np.float32)
    t1 = jnp.einsum('hdc,qkc->hdqk', rpb.astype(F32), jnp.asarray(oh_ci),
                    precision=lax.Precision.HIGHEST)
    big = jnp.einsum('brkd,hdxy->hbrxky', jnp.asarray(oh_dr), t1,
                     precision=lax.Precision.HIGHEST)
    valid = (pdr >= 0)[:, :, None, :, None] & in_win[None, None, :, None, :]
    big = jnp.where(jnp.asarray(valid)[None], big, NEG)
    big = big.reshape(rpb.shape[0], len(pats), NA_QROWS * GRID_W, NA_KROWS * GRID_W)
    return big, tuple(pat_id)


def _block_diag(blocks):
    n = len(blocks)
    r, c = blocks[0].shape
    out = jnp.zeros((n * r, n * c), blocks[0].dtype)
    for i, blk in enumerate(blocks):
        out = out.at[i * r:(i + 1) * r, i * c:(i + 1) * c].set(blk)
    return out


def _layer_params(p, i):
    w = BR_W
    w_in = p['w_in'][i]
    o1, o2, o3 = RW_IN, RW_IN + SSD_IN, RW_IN + SSD_IN + DA_IN
    d = w_in.shape[0]
    w_ssd = jnp.concatenate([w_in[:, o1:o2], jnp.zeros((d, SSD_IN_PAD - SSD_IN), w_in.dtype)], axis=1)
    head_id = np.arange(w) // HD
    ones_hd = jnp.asarray((head_id[:, None] == head_id[None, :]).astype(np.float32), BF16)
    lam_p = p['da_lambda'][i].astype(F32)
    lam_init = 0.8 - 0.6 * math.exp(-0.3 * i)
    lam = jnp.exp(jnp.sum(lam_p[0] * lam_p[1])) - jnp.exp(jnp.sum(lam_p[2] * lam_p[3])) + lam_init
    dtb = p['ssd_dt_bias'][i]
    return {
        'w_rw': w_in[:, :o1].astype(BF16), 'w_ssd': w_ssd.astype(BF16),
        'w_da': w_in[:, o2:o3].astype(BF16), 'w_na': w_in[:, o3:].astype(BF16),
        'rw_mu': p['rw_mu'][i], 'rw_w0': p['rw_w0'][i],
        'rw_wup': _block_diag([p['rw_w_up'][i, 0], p['rw_w_up'][i, 1]]).astype(BF16),
        'rw_a0': p['rw_a0'][i],
        'rw_aup': _block_diag([p['rw_a_up'][i, 0], p['rw_a_up'][i, 1]]).astype(BF16),
        'rw_gup': p['rw_g_up'][i].astype(BF16),
        'rw_kk': p['rw_k_k'][i].reshape(1, w), 'rw_ka': p['rw_k_a'][i].reshape(1, w),
        'rw_rk': p['rw_r_k'][i].reshape(1, w),
        'rw_lng': p['rw_ln_g'][i].reshape(1, w), 'rw_lnb': p['rw_ln_b'][i].reshape(1, w),
        'ones_hd': ones_hd, 'mean_hd': (ones_hd.astype(F32) / HD).astype(BF16),
        'ssd_cw': p['ssd_conv_w'][i], 'ssd_cb': p['ssd_conv_b'][i].reshape(1, SSD_XBC),
        'ssd_dtb': jnp.pad(dtb, ((0, 0), (0, LANES - SSD_HEADS))),
        'ssd_a': jnp.pad(-jnp.exp(p['ssd_a_log'][i].astype(F32)), ((0, 0), (0, LANES - SSD_HEADS))).reshape(2, 1, LANES),
        'ssd_dskip': jnp.repeat(p['ssd_d'][i], HD).reshape(1, w),
        'ssd_ng': p['ssd_norm_g'][i].reshape(1, w),
        'da_lam': jnp.broadcast_to(lam.reshape(1, 1), (1, LANES)).astype(F32),
        'da_lam_init': lam_init,
        'da_g': p['da_subln_g'][i].reshape(1, 2 * HD),
        'na_rpb': p['na_rpb'][i],
        'w_gate': p['w_gate'][i].astype(BF16),
        'gate_b': p['gate_b'][i][:, None, :],
        'w_br': p['w_br'][i].astype(BF16), 'w_out': p['w_out'][i].astype(BF16),
        'ffn_up': p['ffn_up'][i].astype(BF16), 'ffn_cw': p['ffn_conv_w'][i],
        'ffn_cb': p['ffn_conv_b'][i], 'ffn_down': p['ffn_down'][i].astype(BF16),
    }


def _project(h, lp):
    bsz, length, d = h.shape
    h2 = h.reshape(bsz * length, d)
    return tuple(_matmul(h2, lp[n], F32).reshape(bsz, length, -1)
                 for n in ('w_rw', 'w_ssd'))


def _ffn(x, h2, lp, gate):
    act = _ffn_up(h2, lp['ffn_up'], lp['ffn_cw'], lp['ffn_cb'])
    return _matmul_residual(act, lp['ffn_down'], x, gate)


def kernel(x, c, ctx, c_ctx, ada_w, ada_b, norm1_g, norm2_g, w_in, rw_mu, rw_w0, rw_w_up, rw_a0, rw_a_up, rw_g_up, rw_k_k, rw_k_a, rw_r_k, rw_ln_g, rw_ln_b, ssd_conv_w, ssd_conv_b, ssd_dt_bias, ssd_a_log, ssd_d, ssd_norm_g, da_lambda, da_subln_g, na_rpb, w_gate, gate_b, w_br, w_out, ffn_up, ffn_conv_w, ffn_conv_b, ffn_down, final_norm_g):
    p = dict(w_in=w_in, rw_mu=rw_mu, rw_w0=rw_w0, rw_w_up=rw_w_up, rw_a0=rw_a0, rw_a_up=rw_a_up,
             rw_g_up=rw_g_up, rw_k_k=rw_k_k, rw_k_a=rw_k_a, rw_r_k=rw_r_k, rw_ln_g=rw_ln_g,
             rw_ln_b=rw_ln_b, ssd_conv_w=ssd_conv_w, ssd_conv_b=ssd_conv_b, ssd_dt_bias=ssd_dt_bias,
             ssd_a_log=ssd_a_log, ssd_d=ssd_d, ssd_norm_g=ssd_norm_g, da_lambda=da_lambda,
             da_subln_g=da_subln_g, na_rpb=na_rpb, w_gate=w_gate, gate_b=gate_b, w_br=w_br,
             w_out=w_out, ffn_up=ffn_up, ffn_conv_w=ffn_conv_w, ffn_conv_b=ffn_conv_b,
             ffn_down=ffn_down)
    bsz, seq, d = x.shape
    depth = ada_w.shape[0]
    lctx = ctx.shape[1]
    mrows = -(-(bsz + 1) // 16) * 16
    cc = jnp.zeros((mrows, d), F32).at[:bsz].set(c).at[bsz].set(c_ctx)
    cos, sin = _rope_tables(seq)
    xl, xc = x, ctx
    for i in range(depth):
        last = i == depth - 1
        lp = _layer_params(p, i)
        mod = _modulation(cc, ada_w[i], ada_b[i]).reshape(mrows, 6, d)
        ml = [mod[:bsz, n][:, None, :] for n in range(6)]
        mc = [mod[bsz:bsz + 1, n][:, None, :] for n in range(6)]

        hl = _norm_mod(xl, norm1_g[i], ml[0], ml[1])
        hc = _norm_mod(xc, norm1_g[i], mc[0], mc[1])
        zl = _project(hl, lp)
        zc = _project(hc, lp)

        rp_c, lw_c = _rw_prep(zc[0], lp)
        rp_l, lw_l = _rw_prep(zl[0], lp)
        s_zero = jnp.zeros((bsz, 2, RW_HEADS // RW_GROUP, RW_GW, RW_GW), F32)
        of_c, ob_c, s_c = _rw_scan(rp_c, lw_c, s_zero)
        of_l, ob_l, _ = _rw_scan(rp_l, lw_l, s_c)
        a_l = _rw_readout(of_l, ob_l, rp_l, lp)

        xbc_c, dt_c = _ssd_prep(zc[1], lp)
        xbc_l, dt_l = _ssd_prep(zl[1], lp)
        h_zero = jnp.zeros((bsz, 2, SSD_GROUPS, SSD_STATE, SSD_GW), F32)
        yf_c, yb_c, hfin_c = _ssd_scan(xbc_c, dt_c, lp['ssd_a'], h_zero)
        yf_l, yb_l, _ = _ssd_scan(xbc_l, dt_l, lp['ssd_a'], hfin_c)
        b_l = _ssd_readout(yf_l, yb_l, xbc_l, zl[1], lp)

        q_l, k_l, v_l = _qkv_proj(hl, lp['w_da'], cos, sin, True)
        q_c, k_c, v_c = _qkv_proj(hc, lp['w_da'], cos, sin, False)
        c_l = _da_attn(q_l, [k_l, k_c], [v_l, v_c], lp['da_lam'], lp['da_g'], lp['da_lam_init'])

        na_l = _qkv_proj(hl, lp['w_na'], cos, sin, False)
        na_c = _qkv_proj(hc, lp['w_na'], cos, sin, False)
        d_l = _na_attn(na_l, na_c, *_na_bias(lp['na_rpb'], seq))

        merged = _gated_merge(hl.reshape(bsz * seq, d),
                              [t.reshape(bsz * seq, BR_W) for t in (a_l, b_l, c_l, d_l)],
                              lp['w_gate'], lp['gate_b'], lp['w_br']).reshape(bsz, seq, d)
        xl, hl2 = _matmul_residual_norm(merged, lp['w_out'], xl, ml[2], norm2_g[i], ml[3], ml[4])
        xl = _ffn(xl, hl2, lp, ml[5])

        if not last:
            a_c = _rw_readout(of_c, ob_c, rp_c, lp)
            b_c = _ssd_readout(yf_c, yb_c, xbc_c, zc[1], lp)
            c_c = _da_attn(q_c, [k_c], [v_c], lp['da_lam'], lp['da_g'], lp['da_lam_init'])
            d_c = _ctx_attn(na_c)
            merged_c = _gated_merge(hc.reshape(bsz * lctx, d),
                                    [t.reshape(bsz * lctx, BR_W) for t in (a_c, b_c, c_c, d_c)],
                                    lp['w_gate'], lp['gate_b'], lp['w_br']).reshape(bsz, lctx, d)
            xc, hc2 = _matmul_residual_norm(merged_c, lp['w_out'], xc, mc[2], norm2_g[i], mc[3], mc[4])
            xc = _ffn(xc, hc2, lp, mc[5])
    return _final_norm(xl, final_norm_g)
```
